```python
import math
import jax
import jax.numpy as jnp
from jax import lax
import numpy as np

D_MODEL = 1024
BATCH = 4
SEQ = 4096
DEPTH = 2
DEC_BATCH = 32
DEC_SEQ = 4
PAST_LEN = 16384
PAGE_SIZE = 128

H_A = 4
DK_A = 64
DV_A = 128
GATE_RANK = 16
GATE_TAU = 16.0
CHUNK = 64
H_B = 8
P_B = 64
N_B = 64
G_B = 2
CONV_W = 4
D_B = H_B * P_B
CONV_DIM = D_B + 2 * G_B * N_B
POOL_WINDOWS = (2, 4, 8, 16)
POOL_GROUP = 128
D_C = len(POOL_WINDOWS) * POOL_GROUP
POOL_BUF = max(POOL_WINDOWS) - 1
H_D = 8
HD_D = 64
D_D = H_D * HD_D
MOBA_BLOCK = 256
MOBA_TOPK = 3
MOBA_QBLOCK = 32
ROT_DIM = HD_D // 4
ROPE_THETA = 500000.0
N_GROUPS = 4
EXP_PER_GROUP = 8
TOP_FINE = 2
D_EXPERT = 256
D_AB_OUT = H_A * DV_A + D_B
D_CD_OUT = D_C + D_D
DN_ALPHA = (2 * DEPTH) ** 0.25
DN_BETA = (8 * DEPTH) ** -0.25
LN_EPS = 1e-5
RMS_EPS = 1e-6

kernel_name = 'hybrid_gla_ssd_pool_moba_hmoe_step'


def _split(t, sizes):
    offs = np.cumsum(sizes)[:-1].tolist()
    return jnp.split(t, offs, axis=-1)


def _layer_norm(x, g, b):
    xf = x.astype(jnp.float32)
    mu = jnp.mean(xf, axis=-1, keepdims=True)
    var = jnp.mean(jnp.square(xf - mu), axis=-1, keepdims=True)
    return ((xf - mu) * lax.rsqrt(var + LN_EPS) * g.astype(jnp.float32) + b.astype(jnp.float32)).astype(x.dtype)


def _rms_norm(x, g):
    xf = x.astype(jnp.float32)
    return (xf * lax.rsqrt(jnp.mean(xf * xf, axis=-1, keepdims=True) + RMS_EPS) * g.astype(jnp.float32)).astype(x.dtype)


def _post_norm(x, sub, g, b):
    return _layer_norm(DN_ALPHA * x + sub, g, b)


def _chunk_len(t):
    return CHUNK if t % CHUNK == 0 else t


def _chunks(t, n, c):
    return jnp.moveaxis(t.reshape(t.shape[0], n, c, *t.shape[2:]), 1, 0)


def _unchunk(t):
    t = jnp.moveaxis(t, 0, 1)
    return t.reshape(t.shape[0], t.shape[1] * t.shape[2], *t.shape[3:])


def _gla(q, k, v, log_a, s0):
    bsz, t = q.shape[:2]
    c = _chunk_len(t)
    n = t // c
    causal = jnp.tril(jnp.ones((c, c), bool))
    f32 = jnp.float32
    xs = tuple(_chunks(a.astype(f32), n, c) for a in (q, k, v, log_a))

    def step(s, inp):
        qi, ki, vi, ai = inp
        b = jnp.cumsum(ai, axis=1)
        b_last = b[:, -1]
        q_dec = qi * jnp.exp(b)
        k_dec = ki * jnp.exp(-b)
        att = jnp.where(causal, jnp.einsum('bihk,bjhk->bhij', q_dec, k_dec), 0.0)
        o = jnp.einsum('bhij,bjhv->bihv', att, vi) + jnp.einsum('bihk,bhkv->bihv', q_dec, s)
        k_end = ki * jnp.exp(b_last[:, None] - b)
        s = s * jnp.exp(b_last)[..., None] + jnp.einsum('bjhk,bjhv->bhkv', k_end, vi)
        return s, o

    s, o = lax.scan(step, s0.astype(f32), xs)
    return _unchunk(o), s.astype(s0.dtype)


def _ssd(xs, dt, a, bm, cm, h0):
    bsz, t = xs.shape[:2]
    c = _chunk_len(t)
    n = t // c
    f32 = jnp.float32
    rep = H_B // G_B
    bh = jnp.repeat(bm.astype(f32), rep, axis=2)
    ch = jnp.repeat(cm.astype(f32), rep, axis=2)
    xdt = xs.astype(f32) * dt[..., None]
    la = dt * a
    causal = jnp.tril(jnp.ones((c, c), bool))[None, :, :, None]

    def step(h, inp):
        xi, lai, bi, ci = inp
        cum = jnp.cumsum(lai, axis=1)
        seg = jnp.where(causal, cum[:, :, None, :] - cum[:, None, :, :], -jnp.inf)
        w = jnp.einsum('bihn,bjhn->bijh', ci, bi) * jnp.exp(seg)
        y = jnp.einsum('bijh,bjhp->bihp', w, xi) + jnp.einsum('bihn,bhpn->bihp', ci, h) * jnp.exp(cum)[..., None]
        dec = jnp.exp(cum[:, -1:, :] - cum)
        h = h * jnp.exp(cum[:, -1, :])[..., None, None] + jnp.einsum('bjhp,bjhn->bhpn', xi * dec[..., None], bi)
        return h, y

    h, y = lax.scan(step, h0.astype(f32), tuple(_chunks(z, n, c) for z in (xdt, la, bh, ch)))
    return _unchunk(y), h.astype(h0.dtype)


def _causal_conv(u, buf, w, b):
    t = u.shape[1]
    ext = jnp.concatenate([buf, u], axis=1)
    y = sum(ext[:, i:i + t] * w[i] for i in range(CONV_W)) + b
    return y, ext[:, -(CONV_W - 1):]


def _mixer_ab(x, gla0, ssm0, conv0, w_in, w_gate2, b_gate, gla_norm, conv_w, conv_b, dt_bias, a_log, d_skip, ssm_norm, w_out):
    bsz, t, _ = x.shape
    q, k, v, r, g_lr, z, xbc, dtr = _split(x @ w_in, (H_A * DK_A, H_A * DK_A, H_A * DV_A, H_A * DV_A, GATE_RANK, D_B, CONV_DIM, H_B))
    q = q.reshape(bsz, t, H_A, DK_A) * (DK_A ** -0.5)
    k = k.reshape(bsz, t, H_A, DK_A)
    v = v.reshape(bsz, t, H_A, DV_A)
    log_a = jax.nn.log_sigmoid((g_lr @ w_gate2 + b_gate).astype(jnp.float32)).reshape(bsz, t, H_A, DK_A) / GATE_TAU
    o_a, gla_new = _gla(q, k, v, log_a, gla0)
    o_a = _rms_norm(o_a.astype(x.dtype), gla_norm).reshape(bsz, t, H_A * DV_A) * jax.nn.silu(r)
    xbc, conv_new = _causal_conv(xbc, conv0, conv_w, conv_b)
    xs, bm, cm = _split(jax.nn.silu(xbc), (D_B, G_B * N_B, G_B * N_B))
    dt = jax.nn.softplus(dtr.astype(jnp.float32) + dt_bias.astype(jnp.float32))
    a = -jnp.exp(a_log.astype(jnp.float32))
    xs = xs.reshape(bsz, t, H_B, P_B)
    y, ssm_new = _ssd(xs, dt, a, bm.reshape(bsz, t, G_B, N_B), cm.reshape(bsz, t, G_B, N_B), ssm0)
    y = (y + d_skip.astype(jnp.float32)[:, None] * xs.astype(jnp.float32)).astype(x.dtype).reshape(bsz, t, D_B)
    y = _rms_norm(y * jax.nn.silu(z), ssm_norm)
    out = jnp.concatenate([o_a, y], axis=-1) @ w_out
    return out, gla_new, ssm_new, conv_new


def _pool_mix(u, buf, start, w_pool, pool_scale):
    bsz, t, _ = u.shape
    ext = jnp.concatenate([buf, u], axis=1).astype(jnp.float32)
    cs = jnp.concatenate([jnp.zeros((bsz, 1, D_C), jnp.float32), jnp.cumsum(ext, axis=1)], axis=1)
    end = cs[:, POOL_BUF + 1:POOL_BUF + 1 + t]
    pos = start + jnp.arange(t)
    outs = []
    for gi, w in enumerate(POOL_WINDOWS):
        lo, hi = gi * POOL_GROUP, (gi + 1) * POOL_GROUP
        s = end[..., lo:hi] - cs[:, POOL_BUF + 1 - w:POOL_BUF + 1 - w + t, lo:hi]
        cnt = jnp.minimum(w, pos + 1).astype(jnp.float32)[None, :, None]
        outs.append(jnp.einsum('btc,cd->btd', s / cnt - ext[:, POOL_BUF:, lo:hi], w_pool[gi].astype(jnp.float32)))
    mixed = jnp.concatenate(outs, axis=-1) * pool_scale.astype(jnp.float32)
    return mixed.astype(u.dtype), ext[:, -POOL_BUF:].astype(u.dtype)


def _rope(t, pos):
    half = ROT_DIM // 2
    inv = ROPE_THETA ** (-jnp.arange(half, dtype=jnp.float32) / half)
    ang = pos.astype(jnp.float32)[:, None] * inv
    cos = jnp.cos(ang)[None, :, None, :]
    sin = jnp.sin(ang)[None, :, None, :]
    tr = t[..., :ROT_DIM].astype(jnp.float32)
    x1, x2 = tr[..., :half], tr[..., half:]
    rot = jnp.concatenate([x1 * cos - x2 * sin, x2 * cos + x1 * sin], axis=-1).astype(t.dtype)
    return jnp.concatenate([rot, t[..., ROT_DIM:]], axis=-1)


def _attend(s_sel, v_sel, s_own, v_own):
    lead = s_own.shape[:3]
    n_sel = s_sel.shape[3] * s_sel.shape[4]
    p = jax.nn.softmax(jnp.concatenate([s_sel.reshape(*lead, n_sel), s_own], axis=-1), axis=-1)
    return (jnp.einsum('bhqrk,bhqrkd->bqhd', p[..., :n_sel].reshape(s_sel.shape), v_sel)
            + jnp.einsum('bhqk,bkhd->bqhd', p[..., n_sel:], v_own))


def _moba_prompt(q, k, v):
    bsz, t, h, hd = q.shape
    f32 = jnp.float32
    n_blk = -(-t // MOBA_BLOCK)
    pad = n_blk * MOBA_BLOCK - t
    k_blk = jnp.pad(k, ((0, 0), (0, pad), (0, 0), (0, 0))).reshape(bsz, n_blk, MOBA_BLOCK, h, hd)
    v_blk = jnp.pad(v, ((0, 0), (0, pad), (0, 0), (0, 0))).reshape(bsz, n_blk, MOBA_BLOCK, h, hd)
    k_mean = k_blk.astype(f32).mean(axis=2)
    top = min(MOBA_TOPK, (t - 1) // MOBA_BLOCK)
    scale = hd ** -0.5
    b_i = jnp.arange(bsz)[:, None, None, None]
    h_i = jnp.arange(h)[None, :, None, None]

    def one_block(i):
        q0 = i * MOBA_QBLOCK
        qb = lax.dynamic_slice_in_dim(q, q0, MOBA_QBLOCK, axis=1).astype(f32)
        blk = q0 // MOBA_BLOCK
        own_k = lax.dynamic_index_in_dim(k_blk, blk, axis=1, keepdims=False).astype(f32)
        own_v = lax.dynamic_index_in_dim(v_blk, blk, axis=1, keepdims=False).astype(f32)
        q_pos = q0 + jnp.arange(MOBA_QBLOCK)
        k_pos = blk * MOBA_BLOCK + jnp.arange(MOBA_BLOCK)
        s_own = jnp.einsum('bqhd,bkhd->bhqk', qb, own_k) * scale
        s_own = jnp.where(k_pos[None, :] <= q_pos[:, None], s_own, -jnp.inf)
        if top == 0:
            return jnp.einsum('bhqk,bkhd->bqhd', jax.nn.softmax(s_own, axis=-1), own_v)
        gate = jnp.einsum('bqhd,bnhd->bhqn', qb, k_mean)
        gate = jnp.where(jnp.arange(n_blk) < blk, gate, -jnp.inf)
        _, idx = lax.top_k(gate, top)
        valid = jnp.arange(top) < blk
        idx = jnp.where(valid, idx, 0)
        k_sel = k_blk[b_i, idx, :, h_i].astype(f32)
        v_sel = v_blk[b_i, idx, :, h_i].astype(f32)
        s_sel = jnp.einsum('bqhd,bhqrkd->bhqrk', qb, k_sel) * scale
        s_sel = jnp.where(valid[:, None], s_sel, -jnp.inf)
        return _attend(s_sel, v_sel, s_own, own_v)

    o = lax.map(one_block, jnp.arange(t // MOBA_QBLOCK))
    return _unchunk(o).astype(q.dtype)


def _moba_sample(q, k, v, cache_k, cache_v, page_table):
    db, s, h, hd = q.shape
    f32 = jnp.float32
    n_pages = page_table.shape[1]
    past = n_pages * PAGE_SIZE
    ppb = MOBA_BLOCK // PAGE_SIZE
    n_full = past // MOBA_BLOCK
    own_pages = n_full * ppb
    n_own_past = past - n_full * MOBA_BLOCK
    scale = hd ** -0.5
    qf = q.astype(f32)
    tail = page_table[:, own_pages:]
    own_k = jnp.concatenate([cache_k[tail].reshape(db, n_own_past, h, hd), k], axis=1).astype(f32)
    own_v = jnp.concatenate([cache_v[tail].reshape(db, n_own_past, h, hd), v], axis=1).astype(f32)
    k_pos = jnp.arange(n_own_past + s)
    q_pos = n_own_past + jnp.arange(s)
    s_own = jnp.einsum('bqhd,bkhd->bhqk', qf, own_k) * scale
    s_own = jnp.where(k_pos[None, :] <= q_pos[:, None], s_own, -jnp.inf)
    top = min(MOBA_TOPK, n_full)
    if top == 0:
        return jnp.einsum('bhqk,bkhd->bqhd', jax.nn.softmax(s_own, axis=-1), own_v).astype(q.dtype)
    k_past = cache_k[page_table[:, :own_pages]].reshape(db, n_full, MOBA_BLOCK, h, hd)
    k_mean = k_past.astype(f32).mean(axis=2)
    gate = jnp.einsum('bqhd,bnhd->bhqn', qf, k_mean)
    _, idx = lax.top_k(gate, top)
    logical = idx[..., None] * ppb + jnp.arange(ppb)
    phys = page_table[jnp.arange(db)[:, None, None, None, None], logical]
    h_i = jnp.arange(h)[None, :, None, None, None]
    k_sel = cache_k[phys, :, h_i].reshape(db, h, s, top, MOBA_BLOCK, hd).astype(f32)
    v_sel = cache_v[phys, :, h_i].reshape(db, h, s, top, MOBA_BLOCK, hd).astype(f32)
    s_sel = jnp.einsum('bqhd,bhqrkd->bhqrk', qf, k_sel) * scale
    return _attend(s_sel, v_sel, s_own, own_v).astype(q.dtype)


def _cd_front(x, pool_buf, start, w_in, w_pool, pool_scale):
    bsz, t, _ = x.shape
    u, q, k, v = _split(x @ w_in, (D_C, D_D, D_D, D_D))
    pooled, pool_new = _pool_mix(u, pool_buf, start, w_pool, pool_scale)
    pos = start + jnp.arange(t)
    q = _rope(q.reshape(bsz, t, H_D, HD_D), pos)
    k = _rope(k.reshape(bsz, t, H_D, HD_D), pos)
    v = v.reshape(bsz, t, H_D, HD_D)
    return pooled, pool_new, q, k, v


def _mixer_cd_prompt(x, w_in, w_pool, pool_scale, w_out):
    bsz, t, _ = x.shape
    pooled, pool_new, q, k, v = _cd_front(x, jnp.zeros((bsz, POOL_BUF, D_C), x.dtype), 0, w_in, w_pool, pool_scale)
    o = _moba_prompt(q, k, v)
    out = jnp.concatenate([pooled, o.reshape(bsz, t, D_D)], axis=-1) @ w_out
    return out, pool_new, k, v


def _mixer_cd_sample(x, pool_buf, cache_k, cache_v, page_table, w_in, w_pool, pool_scale, w_out):
    bsz, t, _ = x.shape
    past = page_table.shape[1] * PAGE_SIZE
    pooled, pool_new, q, k, v = _cd_front(x, pool_buf, past, w_in, w_pool, pool_scale)
    o = _moba_sample(q, k, v, cache_k, cache_v, page_table)
    out = jnp.concatenate([pooled, o.reshape(bsz, t, D_D)], axis=-1) @ w_out
    return out, pool_new, k, v


def _hier_moe(x, w_coarse, b_coarse, w_fine, b_fine, w_gate_up, w_down):
    bsz, t, d = x.shape
    xt = x.reshape(bsz * t, d)
    lc = (xt @ w_coarse + b_coarse).astype(jnp.float32)
    grp = jnp.argmax(lc, axis=-1)
    p_grp = jnp.take_along_axis(jax.nn.softmax(lc, axis=-1), grp[:, None], axis=-1)
    lf = (jnp.einsum('nd,gde->nge', xt, w_fine) + b_fine).astype(jnp.float32)
    lf_sel = jnp.take_along_axis(lf, grp[:, None, None], axis=1)[:, 0]
    top_v, top_i = lax.top_k(lf_sel, TOP_FINE)
    w_top = jax.nn.softmax(top_v, axis=-1) * p_grp
    gates = (jax.nn.one_hot(top_i, EXP_PER_GROUP, dtype=jnp.float32) * w_top[..., None]).sum(axis=1)
    out = jnp.zeros_like(xt)
    for g in range(N_GROUPS):
        gate_g = jnp.where((grp == g)[:, None], gates, 0.0).astype(x.dtype)
        hg, hu = jnp.split(jnp.einsum('nd,edf->nef', xt, w_gate_up[g]), 2, axis=-1)
        out = out + jnp.einsum('nef,efd->nd', jax.nn.silu(hg) * hu * gate_g[..., None], w_down[g])
    return out.reshape(bsz, t, d)


def setup_inputs(seed: int = 0) -> dict:
    key = jax.random.key(seed)
    ks = jax.random.split(key, 40)

    def nrm(i, shape, scale):
        return jax.random.normal(ks[i], shape, jnp.float32) * scale

    n_pages = PAST_LEN // PAGE_SIZE
    n_phys = (DEC_BATCH * n_pages * 5) // 4
    d_ab_in = 2 * H_A * DK_A + 2 * H_A * DV_A + GATE_RANK + D_B + CONV_DIM + H_B
    d_cd_in = D_C + 3 * D_D
    page_table = jax.random.permutation(ks[8], n_phys)[:DEC_BATCH * n_pages].reshape(DEC_BATCH, n_pages).astype(jnp.int32)
    dt0 = jnp.exp(jax.random.uniform(ks[16], (H_B,), jnp.float32, math.log(1e-3), math.log(1e-1)))
    return {
        'x_prompt': nrm(0, (BATCH, SEQ, D_MODEL), 1.0),
        'x_sample': nrm(1, (DEC_BATCH, DEC_SEQ, D_MODEL), 1.0),
        'state_gla': nrm(2, (DEC_BATCH, H_A, DK_A, DV_A), 0.5),
        'state_ssm': nrm(3, (DEC_BATCH, H_B, P_B, N_B), 0.5),
        'state_conv': nrm(4, (DEC_BATCH, CONV_W - 1, CONV_DIM), 1.0),
        'state_pool': nrm(5, (DEC_BATCH, POOL_BUF, D_C), 1.0),
        'cache_k': nrm(6, (n_phys, PAGE_SIZE, H_D, HD_D), 1.0),
        'cache_v': nrm(7, (n_phys, PAGE_SIZE, H_D, HD_D), 1.0),
        'page_table': page_table,
        'ab_w_in': nrm(9, (D_MODEL, d_ab_in), D_MODEL ** -0.5),
        'ab_w_gate2': nrm(10, (GATE_RANK, H_A * DK_A), GATE_RANK ** -0.5),
        'ab_b_gate': nrm(11, (H_A * DK_A,), 0.1),
        'ab_gla_norm': 1.0 + nrm(12, (DV_A,), 0.02),
        'ab_conv_w': nrm(13, (CONV_W, CONV_DIM), CONV_W ** -0.5),
        'ab_conv_b': nrm(14, (CONV_DIM,), 0.02),
        'ab_dt_bias': dt0 + jnp.log(-jnp.expm1(-dt0)),
        'ab_a_log': jnp.log(jax.random.uniform(ks[15], (H_B,), jnp.float32, 1.0, 16.0)),
        'ab_d_skip': 1.0 + nrm(17, (H_B,), 0.02),
        'ab_ssm_norm': 1.0 + nrm(18, (D_B,), 0.02),
        'ab_w_out': nrm(19, (D_AB_OUT, D_MODEL), D_AB_OUT ** -0.5 * DN_BETA),
        'cd_w_in': nrm(20, (D_MODEL, d_cd_in), D_MODEL ** -0.5),
        'cd_w_pool': nrm(21, (len(POOL_WINDOWS), POOL_GROUP, POOL_GROUP), POOL_GROUP ** -0.5),
        'cd_pool_scale': 1.0 + nrm(22, (D_C,), 0.02),
        'cd_w_out': nrm(23, (D_CD_OUT, D_MODEL), D_CD_OUT ** -0.5 * DN_BETA),
        'moe_w_coarse': nrm(24, (DEPTH, D_MODEL, N_GROUPS), D_MODEL ** -0.5),
        'moe_b_coarse': nrm(25, (DEPTH, N_GROUPS), 0.01),
        'moe_w_fine': nrm(26, (DEPTH, N_GROUPS, D_MODEL, EXP_PER_GROUP), D_MODEL ** -0.5),
        'moe_b_fine': nrm(27, (DEPTH, N_GROUPS, EXP_PER_GROUP), 0.01),
        'moe_w_gate_up': nrm(28, (DEPTH, N_GROUPS, EXP_PER_GROUP, D_MODEL, 2 * D_EXPERT), D_MODEL ** -0.5),
        'moe_w_down': nrm(29, (DEPTH, N_GROUPS, EXP_PER_GROUP, D_EXPERT, D_MODEL), D_EXPERT ** -0.5 * DN_BETA),
        'ln_g': 1.0 + nrm(30, (DEPTH, 2, D_MODEL), 0.02),
        'ln_b': nrm(31, (DEPTH, 2, D_MODEL), 0.02),
    }


def reference(x_prompt, x_sample, state_gla, state_ssm, state_conv, state_pool, cache_k, cache_v, page_table,
              ab_w_in, ab_w_gate2, ab_b_gate, ab_gla_norm, ab_conv_w, ab_conv_b, ab_dt_bias, ab_a_log, ab_d_skip,
              ab_ssm_norm, ab_w_out, cd_w_in, cd_w_pool, cd_pool_scale, cd_w_out,
              moe_w_coarse, moe_b_coarse, moe_w_fine, moe_b_fine, moe_w_gate_up, moe_w_down, ln_g, ln_b):
    ab = (ab_w_in, ab_w_gate2, ab_b_gate, ab_gla_norm, ab_conv_w, ab_conv_b, ab_dt_bias, ab_a_log, ab_d_skip, ab_ssm_norm, ab_w_out)
    cd = (cd_w_in, cd_w_pool, cd_pool_scale, cd_w_out)
    xp, xs = x_prompt, x_sample
    bp = xp.shape[0]
    dtp = xp.dtype
    for l in range(DEPTH):
        if l % 2 == 0:
            mp, gla_p, ssm_p, conv_p = _mixer_ab(xp, jnp.zeros((bp, H_A, DK_A, DV_A), dtp), jnp.zeros((bp, H_B, P_B, N_B), dtp),
                                                 jnp.zeros((bp, CONV_W - 1, CONV_DIM), dtp), *ab)
            ms, gla_s, ssm_s, conv_s = _mixer_ab(xs, state_gla, state_ssm, state_conv, *ab)
        else:
            mp, pool_p, k_p, v_p = _mixer_cd_prompt(xp, *cd)
            ms, pool_s, k_s, v_s = _mixer_cd_sample(xs, state_pool, cache_k, cache_v, page_table, *cd)
        xp = _post_norm(xp, mp, ln_g[l, 0], ln_b[l, 0])
        xs = _post_norm(xs, ms, ln_g[l, 0], ln_b[l, 0])
        moe = (moe_w_coarse[l], moe_b_coarse[l], moe_w_fine[l], moe_b_fine[l], moe_w_gate_up[l], moe_w_down[l])
        xp = _post_norm(xp, _hier_moe(xp, *moe), ln_g[l, 1], ln_b[l, 1])
        xs = _post_norm(xs, _hier_moe(xs, *moe), ln_g[l, 1], ln_b[l, 1])
    return (xp, xs, gla_p, ssm_p, conv_p, pool_p, k_p, v_p, gla_s, ssm_s, conv_s, pool_s, k_s, v_s)
```

```python
import functools

import jax
import jax.numpy as jnp
import numpy as np
from jax import lax
from jax.experimental import pallas as pl
from jax.experimental.pallas import tpu as pltpu

F32 = jnp.float32
BF16 = jnp.bfloat16

D_MODEL = 1024
DEPTH = 2
PAGE_SIZE = 128
H_A, DK_A, DV_A = 4, 64, 128
GATE_RANK = 16
GATE_TAU = 16.0
CHUNK = 64
H_B, P_B, N_B, G_B = 8, 64, 64, 2
CONV_W = 4
D_B = H_B * P_B
CONV_DIM = D_B + 2 * G_B * N_B
POOL_WINDOWS = (2, 4, 8, 16)
POOL_GROUP = 128
D_C = len(POOL_WINDOWS) * POOL_GROUP
POOL_BUF = max(POOL_WINDOWS) - 1
H_D, HD_D = 8, 64
D_D = H_D * HD_D
MOBA_BLOCK = 256
MOBA_TOPK = 3
ROT_DIM = HD_D // 4
ROPE_THETA = 500000.0
N_GROUPS, EXP_PER_GROUP, TOP_FINE, D_EXPERT = 4, 8, 2, 256
N_EXPERTS = N_GROUPS * EXP_PER_GROUP
DN_ALPHA = (2 * DEPTH) ** 0.25
LN_EPS = 1e-5
RMS_EPS = 1e-6

V7X_LANES = 128
V7X_SUBLANES = 8
V7X_VMEM_LIMIT_BYTES = 56 * 1024 * 1024

NN = (((1,), (0,)), ((), ()))
NT = (((1,), (1,)), ((), ()))
TN = (((0,), (0,)), ((), ()))


def _params(n_axes):
    return pltpu.CompilerParams(dimension_semantics=("arbitrary",) * n_axes,
                                vmem_limit_bytes=V7X_VMEM_LIMIT_BYTES)


def _const_spec(shape):
    zeros = (0,) * len(shape)
    return pl.BlockSpec(shape, lambda *_: zeros, pipeline_mode=pl.Buffered(1))


def _split2(a):
    hi = a.astype(BF16)
    lo = (a - hi.astype(F32)).astype(BF16)
    return hi, lo


def _split3(a):
    hi = a.astype(BF16)
    r = a - hi.astype(F32)
    mid = r.astype(BF16)
    lo = (r - mid.astype(F32)).astype(BF16)
    return hi, mid, lo


def _mm(a, b, dims=NN):
    return lax.dot_general(a, b, dims, preferred_element_type=F32)


def _dot_hl(ah, al, bh, bl, dims=NN):
    return (_mm(ah, bl, dims) + _mm(al, bh, dims)) + _mm(ah, bh, dims)


def _dot3(a, b, dims=NN):
    ah, al = _split2(a)
    bh, bl = _split2(b)
    return _dot_hl(ah, al, bh, bl, dims)


def _dotx(a, e, dims=NN):
    h, m, l = _split3(a)
    return (_mm(l, e, dims) + _mm(m, e, dims)) + _mm(h, e, dims)


def _xdot(e, a, dims=NN):
    h, m, l = _split3(a)
    return (_mm(e, l, dims) + _mm(e, m, dims)) + _mm(e, h, dims)


def _silu(x):
    return x / (1.0 + jnp.exp(-x))


def _softplus(x):
    return jnp.maximum(x, 0.0) + jnp.log1p(jnp.exp(-jnp.abs(x)))


def _log_sigmoid(x):
    return jnp.minimum(x, 0.0) - jnp.log1p(jnp.exp(-jnp.abs(x)))


def _layer_norm(z, g, b):
    mu = jnp.mean(z, axis=-1, keepdims=True)
    zc = z - mu
    var = jnp.mean(zc * zc, axis=-1, keepdims=True)
    return zc * lax.rsqrt(var + LN_EPS) * g + b


def _iota(shape, dim):
    return lax.broadcasted_iota(jnp.int32, shape, dim)


def _proj_kernel(x_ref, wh_ref, wl_ref, o_ref, *, col_chunk):
    xh, xl = _split2(x_ref[...])
    m = o_ref.shape[1]
    for j in range(0, m, col_chunk):
        cs = slice(j, min(j + col_chunk, m))
        o_ref[:, cs] = _dot_hl(xh, xl, wh_ref[:, cs], wl_ref[:, cs])


def _proj(x, wh, wl, *, n_rows=None, tm=256, col_chunk=512):
    n = x.shape[0] if n_rows is None else n_rows
    k = x.shape[1]
    m = wh.shape[1]
    assert n % tm == 0
    return pl.pallas_call(
        functools.partial(_proj_kernel, col_chunk=col_chunk),
        grid=(n // tm,),
        in_specs=[pl.BlockSpec((tm, k), lambda i: (i, 0)), _const_spec((k, m)), _const_spec((k, m))],
        out_specs=pl.BlockSpec((tm, m), lambda i: (i, 0)),
        out_shape=jax.ShapeDtypeStruct((n, m), F32),
        compiler_params=_params(1),
        name="proj",
    )(x, wh, wl)


def _outproj_ln_kernel(a_ref, x_ref, wh_ref, wl_ref, g_ref, b_ref, o_ref):
    ah, al = _split2(a_ref[...])
    y = _dot_hl(ah, al, wh_ref[...], wl_ref[...])
    o_ref[...] = _layer_norm(DN_ALPHA * x_ref[...] + y, g_ref[...], b_ref[...])


def _outproj_ln(a, x, wh, wl, g, b, *, tm=256):
    n, k = a.shape
    d = wh.shape[1]
    assert n % tm == 0
    return pl.pallas_call(
        _outproj_ln_kernel,
        grid=(n // tm,),
        in_specs=[pl.BlockSpec((tm, k), lambda i: (i, 0)), pl.BlockSpec((tm, d), lambda i: (i, 0)),
                  _const_spec((k, d)), _const_spec((k, d)), _const_spec((1, d)), _const_spec((1, d))],
        out_specs=pl.BlockSpec((tm, d), lambda i: (i, 0)),
        out_shape=jax.ShapeDtypeStruct((n, d), F32),
        compiler_params=_params(1),
        name="outproj_ln",
    )(a, x, wh, wl, g, b)


SM_GLR0 = 0
SM_DT0 = GATE_RANK
CONV_HDR = V7X_SUBLANES


def _scan_kernel(qkvr_ref, z_ref, xbc_ref, sm_ref, s0_ref, h0_ref, c0_ref,
                 wg2h_ref, wg2l_ref, bg_ref, gnorm_ref, cw_ref, cb_ref, dtb_ref, alog_ref,
                 dskip_ref, snorm_ref,
                 oy_ref, s_out_ref, h_out_ref, c_out_ref,
                 st_sc, h_sc, ext_sc, o_sc, y_sc, *, tc, chunk, valid):
    t = pl.program_id(1)
    nt = pl.num_programs(1)
    n_chunks = tc // chunk
    shift = chunk.bit_length() - 1
    assert (1 << shift) == chunk and tc % chunk == 0

    @pl.when(t == 0)
    def _():
        st_sc[...] = s0_ref[0]
        h_sc[...] = h0_ref[0]
        ext_sc[0:CONV_HDR, :] = c0_ref[0]

    row = _iota((tc, tc), 0)
    col = _iota((tc, tc), 1)
    causal = ((row >> shift) == (col >> shift)) & (col <= row)
    l_tri = jnp.where(causal, 1.0, 0.0).astype(BF16)
    lane = _iota((tc, V7X_LANES), 1)
    rowv = _iota((tc, V7X_LANES), 0)
    dt_lane = (lane >= SM_DT0) & (lane < SM_DT0 + H_B)
    if valid < tc:
        row_ok = rowv < valid
        dt_lane = dt_lane & row_ok
    e8 = jnp.where((_iota((V7X_LANES, D_B), 1) >> 6) + SM_DT0 == _iota((V7X_LANES, D_B), 0), 1.0, 0.0).astype(BF16)

    sm = sm_ref[...]
    pre = _dot_hl(*_split2(sm), wg2h_ref[...], wg2l_ref[...])
    log_a = _log_sigmoid(pre + bg_ref[...]) / GATE_TAU
    if valid < tc:
        log_a = jnp.where(_iota((tc, H_A * DK_A), 0) < valid, log_a, 0.0)
    b_cum = _xdot(l_tri, log_a)
    dt_pad = jnp.where(dt_lane, _softplus(sm + dtb_ref[...]), 0.0)
    a_pad = -jnp.exp(alog_ref[...])
    la_pad = jnp.where(dt_lane, dt_pad * a_pad, 0.0)
    cum_pad = _xdot(l_tri, la_pad)
    dt_rep = _dotx(dt_pad, e8)
    cum_rep = _dotx(cum_pad, e8)

    ext_sc[CONV_HDR:CONV_HDR + tc, :] = xbc_ref[...]
    cw = cw_ref[...]
    acc = ext_sc[CONV_HDR - 3:CONV_HDR - 3 + tc, :] * cw[0:1, :]
    for i in range(1, CONV_W):
        acc = acc + ext_sc[CONV_HDR - 3 + i:CONV_HDR - 3 + i + tc, :] * cw[i:i + 1, :]
    xc = _silu(acc + cb_ref[...])

    @pl.when(t == nt - 1)
    def _():
        c_out_ref[0] = ext_sc[valid:valid + CONV_HDR, :]

    ext_sc[0:CONV_HDR, :] = ext_sc[tc:tc + CONV_HDR, :]

    xs = xc[:, 0:D_B]
    bm = xc[:, D_B:D_B + G_B * N_B]
    cm = xc[:, D_B + G_B * N_B:CONV_DIM]
    xdt = xs * dt_rep

    qkvr = qkvr_ref[...]
    q = qkvr[:, 0:H_A * DK_A] * (DK_A ** -0.5)
    k = qkvr[:, H_A * DK_A:2 * H_A * DK_A]
    v = qkvr[:, 2 * H_A * DK_A:2 * H_A * DK_A + H_A * DV_A]
    r = qkvr[:, 2 * H_A * DK_A + H_A * DV_A:]
    q_dec = q * jnp.exp(b_cum)
    k_dec = k * jnp.exp(-b_cum)
    for h in range(H_A):
        ks = slice(h * DK_A, (h + 1) * DK_A)
        vs = slice(h * DV_A, (h + 1) * DV_A)
        att = jnp.where(causal, _dot3(q_dec[:, ks], k_dec[:, ks], NT), 0.0)
        o_sc[:, vs] = _dot3(att, v[:, vs])
    for c in range(n_chunks):
        rows = slice(c * chunk, (c + 1) * chunk)
        last = slice((c + 1) * chunk - 1, (c + 1) * chunk)
        b_last = b_cum[last, :]
        k_end = k[rows, :] * jnp.exp(b_last - b_cum[rows, :])
        dec = jnp.exp(b_last)
        for h in range(H_A):
            ks = slice(h * DK_A, (h + 1) * DK_A)
            vs = slice(h * DV_A, (h + 1) * DV_A)
            st = st_sc[h]
            o_sc[rows, vs] = o_sc[rows, vs] + _dot3(q_dec[rows, ks], st, NT)
            st_sc[h] = st * dec[:, ks] + _dot3(v[rows, vs], k_end[:, ks], TN)

    for g in range(G_B):
        gs = slice(g * N_B, (g + 1) * N_B)
        cb = _dot3(cm[:, gs], bm[:, gs], NT)
        for hh in range(g * (H_B // G_B), (g + 1) * (H_B // G_B)):
            ps = slice(hh * P_B, (hh + 1) * P_B)
            ea = jnp.where(_iota((V7X_LANES, tc), 0) == SM_DT0 + hh, 1.0, 0.0).astype(BF16)
            m1 = jnp.where(_iota((tc, V7X_LANES), 1) == SM_DT0 + hh, 1.0, 0.0).astype(BF16)
            seg = _dotx(cum_pad, ea) - _xdot(m1, cum_pad, NT)
            w = cb * jnp.exp(jnp.where(causal, seg, -jnp.inf))
            y_sc[:, ps] = _dot3(w, xdt[:, ps])
    for c in range(n_chunks):
        rows = slice(c * chunk, (c + 1) * chunk)
        last = slice((c + 1) * chunk - 1, (c + 1) * chunk)
        cum_c = cum_rep[rows, :]
        cum_l = cum_rep[last, :]
        e_cum = jnp.exp(cum_c)
        x_dec = xdt[rows, :] * jnp.exp(cum_l - cum_c)
        e_last = jnp.exp(cum_l)
        for hh in range(H_B):
            g = hh // (H_B // G_B)
            gs = slice(g * N_B, (g + 1) * N_B)
            ps = slice(hh * P_B, (hh + 1) * P_B)
            hs = h_sc[hh]
            y_sc[rows, ps] = y_sc[rows, ps] + _dot3(cm[rows, gs], hs, NT) * e_cum[:, ps]
            h_sc[hh] = hs * e_last[:, ps] + _dot3(x_dec[:, ps], bm[rows, gs], TN)

    gn = gnorm_ref[...]
    for h in range(H_A):
        vs = slice(h * DV_A, (h + 1) * DV_A)
        oh = o_sc[:, vs]
        oh = oh * lax.rsqrt(jnp.mean(oh * oh, axis=-1, keepdims=True) + RMS_EPS) * gn
        oy_ref[:, vs] = oh * _silu(r[:, vs])
    y = (y_sc[...] + dskip_ref[...] * xs) * _silu(z_ref[...])
    y = y * lax.rsqrt(jnp.mean(y * y, axis=-1, keepdims=True) + RMS_EPS) * snorm_ref[...]
    oy_ref[:, H_A * DV_A:] = y

    @pl.when(t == nt - 1)
    def _():
        s_out_ref[0] = st_sc[...]
        h_out_ref[0] = h_sc[...]


def _scan_call(proj, s0t, h0, c0, prm, *, specs, nb, t_len, tc, chunk, valid):
    nt = t_len // tc
    assert t_len % tc == 0 and (valid == tc or nt == 1)
    rowspec = lambda c: pl.BlockSpec((tc, c), lambda b, t: (b * nt + t, 0))
    stspec = lambda shp: pl.BlockSpec((1,) + shp, lambda b, t: (b,) + (0,) * len(shp))
    names = ("wg2h", "wg2l", "bg", "gnorm", "cw", "cb", "dtb", "alog", "dskip", "snorm")
    consts = [prm[n] for n in names]
    return pl.pallas_call(
        functools.partial(_scan_kernel, tc=tc, chunk=chunk, valid=valid),
        grid=(nb, nt),
        in_specs=list(specs)
                 + [stspec((H_A, DV_A, DK_A)), stspec((H_B, P_B, N_B)), stspec((CONV_HDR, CONV_DIM))]
                 + [_const_spec(c.shape) for c in consts],
        out_specs=[rowspec(H_A * DV_A + D_B), stspec((H_A, DV_A, DK_A)), stspec((H_B, P_B, N_B)),
                   stspec((CONV_HDR, CONV_DIM))],
        out_shape=[jax.ShapeDtypeStruct((nb * t_len, H_A * DV_A + D_B), F32),
                   jax.ShapeDtypeStruct((nb, H_A, DV_A, DK_A), F32),
                   jax.ShapeDtypeStruct((nb, H_B, P_B, N_B), F32),
                   jax.ShapeDtypeStruct((nb, CONV_HDR, CONV_DIM), F32)],
        scratch_shapes=[pltpu.VMEM((H_A, DV_A, DK_A), F32), pltpu.VMEM((H_B, P_B, N_B), F32),
                        pltpu.VMEM((tc + CONV_HDR, CONV_DIM), F32),
                        pltpu.VMEM((tc, H_A * DV_A), F32), pltpu.VMEM((tc, D_B), F32)],
        compiler_params=_params(2),
        name="gla_ssd_scan",
    )(proj, proj, proj, proj, s0t, h0, c0, *consts)


AB_QKVR = 2 * H_A * DK_A + 2 * H_A * DV_A
AB_XBC0 = AB_QKVR
AB_SM0 = AB_XBC0 + CONV_DIM
AB_Z0 = 2560
AB_COLS = AB_Z0 + D_B


def _split_w(w):
    hi = w.astype(BF16)
    lo = (w - hi.astype(F32)).astype(BF16)
    return hi, lo


def _lane_pad(v, start, width=V7X_LANES):
    out = jnp.zeros((1, width), F32)
    return out.at[0, start:start + v.shape[0]].set(v.astype(F32))


def _ab_prepare(w_in, w_gate2, b_gate, gla_norm, conv_w, conv_b, dt_bias, a_log, d_skip, ssm_norm, w_out):
    o = np.cumsum([0, H_A * DK_A, H_A * DK_A, H_A * DV_A, H_A * DV_A, GATE_RANK, D_B, CONV_DIM, H_B])
    q0, gl0, z0, xbc0, dt0, end = o[0], o[4], o[5], o[6], o[7], o[8]
    k_dim = w_in.shape[0]
    w = jnp.concatenate([
        w_in[:, q0:gl0], w_in[:, xbc0:dt0], w_in[:, gl0:z0], w_in[:, dt0:end],
        jnp.zeros((k_dim, AB_Z0 - AB_SM0 - GATE_RANK - H_B), F32), w_in[:, z0:xbc0]], axis=1)
    assert w.shape[1] == AB_COLS
    wg2 = jnp.zeros((V7X_LANES, H_A * DK_A), F32).at[0:GATE_RANK].set(w_gate2)
    wg2h, wg2l = _split_w(wg2)
    prm = dict(
        wg2h=wg2h, wg2l=wg2l, bg=b_gate.reshape(1, -1), gnorm=gla_norm.reshape(1, -1),
        cw=conv_w, cb=conv_b.reshape(1, -1), dtb=_lane_pad(dt_bias, SM_DT0), alog=_lane_pad(a_log, SM_DT0),
        dskip=jnp.repeat(d_skip, P_B).reshape(1, -1), snorm=ssm_norm.reshape(1, -1))
    return _split_w(w), prm, _split_w(w_out)


def _scan_from_proj(proj, s0, h0, c0, prm, *, nb, t_len, tc, chunk, valid):
    nt = t_len // tc
    s0t = jnp.swapaxes(s0, -1, -2)
    c0p = jnp.zeros((nb, CONV_HDR, CONV_DIM), F32).at[:, CONV_HDR - (CONV_W - 1):].set(c0)
    colspec = lambda width, start: pl.BlockSpec((tc, width), lambda b, t: (b * nt + t, start // width))
    oy, s_new, h_new, c_new = _scan_call(
        proj, s0t, h0, c0p, prm,
        specs=[colspec(AB_QKVR, 0), colspec(D_B, AB_Z0), colspec(CONV_DIM, AB_XBC0), colspec(V7X_LANES, AB_SM0)],
        nb=nb, t_len=t_len, tc=tc, chunk=chunk, valid=valid)
    return oy, jnp.swapaxes(s_new, -1, -2), h_new, c_new[:, CONV_HDR - (CONV_W - 1):]


RT_FINE0 = N_GROUPS
RT_HALF = V7X_LANES // 2
MOE_TILE = 256


def _router_kernel(x_ref, wh_ref, wl_ref, b_ref, id_ref, gate_ref):
    xh, xl = _split2(x_ref[...])
    logits = _dot_hl(xh, xl, wh_ref[...], wl_ref[...]) + b_ref[...]
    shp = logits.shape
    lane = _iota(shp, 1)
    lane_f = lane.astype(F32)
    big = float(V7X_LANES)
    ninf = -jnp.inf

    def first_max(mask):
        v = jnp.max(jnp.where(mask, logits, ninf), axis=-1, keepdims=True)
        i = jnp.min(jnp.where(mask & (logits == v), lane_f, big), axis=-1, keepdims=True)
        return v, i

    is_c = lane < N_GROUPS
    mc, grp = first_max(is_c)
    p_grp = 1.0 / jnp.sum(jnp.where(is_c, jnp.exp(logits - mc), 0.0), axis=-1, keepdims=True)
    fine = (lane >= RT_FINE0) & (lane < RT_FINE0 + N_EXPERTS)
    cand = fine & (((lane - RT_FINE0) >> 3).astype(F32) == grp)
    v1, i1 = first_max(cand)
    v2, i2 = first_max(cand & (lane_f != i1))
    e = jnp.exp(v2 - v1)
    w1 = p_grp / (1.0 + e)
    w2 = p_grp * (e / (1.0 + e))
    first = lane < RT_HALF
    id_ref[...] = (jnp.where(first, i1, i2) - float(RT_FINE0)).astype(jnp.int32)
    gate_ref[...] = jnp.where(first, w1, w2)


def _router(x, wh, wl, b, *, tm):
    n, d = x.shape
    assert n % tm == 0
    return pl.pallas_call(
        _router_kernel,
        grid=(n // tm,),
        in_specs=[pl.BlockSpec((tm, d), lambda i: (i, 0)), _const_spec(wh.shape), _const_spec(wl.shape),
                  _const_spec(b.shape)],
        out_specs=[pl.BlockSpec((tm, V7X_LANES), lambda i: (i, 0))] * 2,
        out_shape=[jax.ShapeDtypeStruct((n, V7X_LANES), jnp.int32), jax.ShapeDtypeStruct((n, V7X_LANES), F32)],
        compiler_params=_params(1),
        name="moe_router",
    )(x, wh, wl, b)


def _expert_kernel(te_ref, tr_ref, src_hbm, dst_hbm, x_hbm, wgu_ref, wd_ref, y_hbm,
                   src_sm, dst_sm, xbuf, ybuf, sem_i, sem_g, sem_s):
    i = pl.program_id(0)
    rows = tr_ref[i]
    tile = xbuf.shape[0]

    def idx_copy(tab_hbm, tab_sm, k):
        return pltpu.make_async_copy(tab_hbm.at[pl.ds(i, 1)], tab_sm, sem_i.at[k])

    def gather_copy(tok, r):
        return pltpu.make_async_copy(x_hbm.at[pl.ds(tok, 1)], xbuf.at[pl.ds(r, 1)], sem_g)

    def scatter_copy(dst, r):
        return pltpu.make_async_copy(ybuf.at[pl.ds(r, 1)], y_hbm.at[pl.ds(dst, 1)], sem_s)

    @pl.when(rows > 0)
    def _():
        idx_copy(src_hbm, src_sm, 0).start()
        idx_copy(dst_hbm, dst_sm, 1).start()
        idx_copy(src_hbm, src_sm, 0).wait()
        idx_copy(dst_hbm, dst_sm, 1).wait()

        def start_gather(r, c):
            gather_copy(src_sm[0, r], r).start()
            return c

        def wait_gather(r, c):
            gather_copy(0, r).wait()
            return c

        lax.fori_loop(0, tile, start_gather, 0)
        lax.fori_loop(0, tile, wait_gather, 0)

        xh, xl = _split2(xbuf[...])
        wh, wl = _split2(wgu_ref[0])
        hgu = _dot_hl(xh, xl, wh, wl)
        act = _silu(hgu[:, :D_EXPERT]) * hgu[:, D_EXPERT:]
        ybuf[...] = _dot3(act, wd_ref[0])

        def start_scatter(r, c):
            scatter_copy(dst_sm[0, r], r).start()
            return c

        def wait_scatter(r, c):
            scatter_copy(0, r).wait()
            return c

        lax.fori_loop(0, rows, start_scatter, 0)
        lax.fori_loop(0, rows, wait_scatter, 0)


def _experts(x, w_gate_up, w_down, tile_expert, tile_rows, src_tab, dst_tab):
    n, d = x.shape
    n_tiles, tile = src_tab.shape
    f2 = w_gate_up.shape[-1]
    any_spec = pl.BlockSpec(memory_space=pl.ANY)
    grid_spec = pltpu.PrefetchScalarGridSpec(
        num_scalar_prefetch=2,
        grid=(n_tiles,),
        in_specs=[any_spec, any_spec, any_spec,
                  pl.BlockSpec((1, d, f2), lambda i, te, tr: (te[i], 0, 0)),
                  pl.BlockSpec((1, f2 // 2, d), lambda i, te, tr: (te[i], 0, 0))],
        out_specs=any_spec,
        scratch_shapes=[pltpu.SMEM((1, tile), jnp.int32), pltpu.SMEM((1, tile), jnp.int32),
                        pltpu.VMEM((tile, d), F32), pltpu.VMEM((tile, d), F32),
                        pltpu.SemaphoreType.DMA((2,)), pltpu.SemaphoreType.DMA(()), pltpu.SemaphoreType.DMA(())],
    )
    return pl.pallas_call(
        _expert_kernel,
        grid_spec=grid_spec,
        out_shape=jax.ShapeDtypeStruct((2 * n, d), F32),
        compiler_params=_params(1),
        name="moe_experts",
    )(tile_expert, tile_rows, src_tab, dst_tab, x, w_gate_up, w_down)


def _combine_ln_kernel(x_ref, y1_ref, y2_ref, gate_ref, g_ref, b_ref, o_ref):
    gate = gate_ref[...]
    moe = gate[:, 0:1] * y1_ref[...] + gate[:, RT_HALF:RT_HALF + 1] * y2_ref[...]
    o_ref[...] = _layer_norm(DN_ALPHA * x_ref[...] + moe, g_ref[...], b_ref[...])


def _combine_ln(x, y, gate, g, b, *, tm):
    n, d = x.shape
    nb = n // tm
    assert n % tm == 0
    row = lambda off: pl.BlockSpec((tm, d), lambda i: (i + off, 0))
    return pl.pallas_call(
        _combine_ln_kernel,
        grid=(nb,),
        in_specs=[row(0), row(0), row(nb), pl.BlockSpec((tm, V7X_LANES), lambda i: (i, 0)),
                  _const_spec((1, d)), _const_spec((1, d))],
        out_specs=row(0),
        out_shape=jax.ShapeDtypeStruct((n, d), F32),
        compiler_params=_params(1),
        name="moe_combine_ln",
    )(x, y, y, gate, g, b)


def _moe_plan(ids, n):
    eid = jnp.concatenate([ids[:, 0], ids[:, RT_HALF]])
    order = jnp.argsort(eid).astype(jnp.int32)
    counts = jnp.sum((eid[:, None] == jnp.arange(N_EXPERTS)[None, :]).astype(jnp.int32), axis=0)
    tiles_per = (counts + MOE_TILE - 1) // MOE_TILE
    tile_end = jnp.cumsum(tiles_per)
    cnt_beg = jnp.cumsum(counts) - counts
    n_tiles = -(-2 * n // MOE_TILE) + N_EXPERTS
    ti = jnp.arange(n_tiles)
    used = ti < tile_end[-1]
    te = jnp.minimum(jnp.searchsorted(tile_end, ti, side="right"), N_EXPERTS - 1)
    last_used = te[jnp.maximum(tile_end[-1] - 1, 0)]
    te = jnp.where(used, te, last_used).astype(jnp.int32)
    local = ti - (tile_end[te] - tiles_per[te])
    rows = jnp.where(used, jnp.clip(counts[te] - local * MOE_TILE, 0, MOE_TILE), 0).astype(jnp.int32)
    start = cnt_beg[te] + local * MOE_TILE
    pos = start[:, None] + jnp.minimum(jnp.arange(MOE_TILE)[None, :], jnp.maximum(rows - 1, 0)[:, None])
    asg = order[jnp.clip(pos, 0, 2 * n - 1)]
    return te, rows, (asg % n).astype(jnp.int32), asg.astype(jnp.int32)


def _moe_prepare(w_coarse, b_coarse, w_fine, b_fine):
    d = w_coarse.shape[0]
    wf = jnp.transpose(w_fine, (1, 0, 2)).reshape(d, N_EXPERTS)
    w = jnp.zeros((d, V7X_LANES), F32).at[:, :N_GROUPS].set(w_coarse).at[:, RT_FINE0:RT_FINE0 + N_EXPERTS].set(wf)
    b = jnp.zeros((1, V7X_LANES), F32).at[0, :N_GROUPS].set(b_coarse)
    b = b.at[0, RT_FINE0:RT_FINE0 + N_EXPERTS].set(b_fine.reshape(-1))
    return _split_w(w), b


def _moe_layer(x, w_coarse, b_coarse, w_fine, b_fine, w_gate_up, w_down, g, b, *, tm):
    n, d = x.shape
    (wrh, wrl), br = _moe_prepare(w_coarse, b_coarse, w_fine, b_fine)
    ids, gate = _router(x, wrh, wrl, br, tm=tm)
    te, rows, src, dst = _moe_plan(ids, n)
    y = _experts(x, w_gate_up.reshape(N_EXPERTS, d, 2 * D_EXPERT), w_down.reshape(N_EXPERTS, D_EXPERT, d),
                 te, rows, src, dst)
    return _combine_ln(x, y, gate, g, b, tm=tm)


def _cdproj_kernel(x_ref, wh_ref, wl_ref, cos_ref, sin_ref, perm_ref, u_ref, q_ref, k_ref, v_ref):
    xh, xl = _split2(x_ref[...])

    def col(j):
        cs = slice(j * D_C, (j + 1) * D_C)
        return _dot_hl(xh, xl, wh_ref[:, cs], wl_ref[:, cs])

    u_ref[...] = col(0)
    for j, ref in ((1, q_ref), (2, k_ref)):
        t = col(j)
        ref[...] = t * cos_ref[...] + _dotx(t, perm_ref[...]) * sin_ref[...]
    v_ref[...] = col(3)


def _cdproj(x, wh, wl, cos_t, sin_t, perm, *, n_rows, tm):
    k = x.shape[1]
    assert D_C == D_D and n_rows % tm == 0 and cos_t.shape[0] % tm == 0
    nt = cos_t.shape[0] // tm
    row = pl.BlockSpec((tm, D_D), lambda i: (i, 0))
    tab = pl.BlockSpec((tm, D_D), lambda i: (i % nt, 0))
    return pl.pallas_call(
        _cdproj_kernel,
        grid=(n_rows // tm,),
        in_specs=[pl.BlockSpec((tm, k), lambda i: (i, 0)), _const_spec(wh.shape), _const_spec(wl.shape),
                  tab, tab, _const_spec(perm.shape)],
        out_specs=[row] * 4,
        out_shape=[jax.ShapeDtypeStruct((n_rows, D_D), F32)] * 4,
        compiler_params=_params(1),
        name="cd_proj_rope",
    )(x, wh, wl, cos_t, sin_t, perm)


def _rope_tables(pos):
    half = ROT_DIM // 2
    inv = ROPE_THETA ** (-jnp.arange(half, dtype=F32) / half)
    ang = pos.astype(F32)[:, None] * inv
    cos, sin = jnp.cos(ang), jnp.sin(ang)
    n = pos.shape[0]
    rest = HD_D - ROT_DIM
    cos_h = jnp.concatenate([cos, cos, jnp.ones((n, rest), F32)], axis=-1)
    sin_h = jnp.concatenate([-sin, sin, jnp.zeros((n, rest), F32)], axis=-1)
    perm = np.zeros((D_D, D_D), np.float32)
    for dst in range(D_D):
        j = dst % HD_D
        if j < half:
            perm[dst + half, dst] = 1.0
        elif j < ROT_DIM:
            perm[dst - half, dst] = 1.0
    return jnp.tile(cos_h, (1, H_D)), jnp.tile(sin_h, (1, H_D)), jnp.asarray(perm, BF16)


POOL_HDR = 16


def _pool_kernel(u_ref, b0_ref, wph_ref, wpl_ref, sc_ref, o_ref, bo_ref, ext_sc, *, tc, valid, start):
    t = pl.program_id(1)
    nt = pl.num_programs(1)

    @pl.when(t == 0)
    def _():
        ext_sc[0:POOL_HDR, :] = b0_ref[0]

    ext_sc[POOL_HDR:POOL_HDR + tc, :] = u_ref[...]
    u = u_ref[...]
    rowi = _iota((tc, POOL_HDR + tc), 0) + POOL_HDR
    colj = _iota((tc, POOL_HDR + tc), 1)
    pos = (start + t * tc + _iota((tc, POOL_GROUP), 0)).astype(F32)
    for gi, w in enumerate(POOL_WINDOWS):
        ls = slice(gi * POOL_GROUP, (gi + 1) * POOL_GROUP)
        band = jnp.where((colj <= rowi) & (colj > rowi - w), 1.0, 0.0).astype(BF16)
        win = _xdot(band, ext_sc[:, ls])
        cnt = jnp.minimum(float(w), pos + 1.0)
        dh, dl = _split2(win / cnt - u[:, ls])
        o_ref[:, ls] = _dot_hl(dh, dl, wph_ref[gi], wpl_ref[gi]) * sc_ref[:, ls]

    @pl.when(t == nt - 1)
    def _():
        bo_ref[0] = ext_sc[valid:valid + POOL_HDR, :]

    ext_sc[0:POOL_HDR, :] = ext_sc[tc:tc + POOL_HDR, :]


def _pool(u, buf, wph, wpl, scale, *, nb, t_len, tc, valid, start):
    nt = t_len // tc
    assert t_len % tc == 0 and (valid == tc or nt == 1)
    b0 = jnp.zeros((nb, POOL_HDR, D_C), F32).at[:, POOL_HDR - POOL_BUF:].set(buf)
    row = pl.BlockSpec((tc, D_C), lambda b, t: (b * nt + t, 0))
    st = pl.BlockSpec((1, POOL_HDR, D_C), lambda b, t: (b, 0, 0))
    pooled, bnew = pl.pallas_call(
        functools.partial(_pool_kernel, tc=tc, valid=valid, start=start),
        grid=(nb, nt),
        in_specs=[row, st, _const_spec(wph.shape), _const_spec(wpl.shape), _const_spec(scale.shape)],
        out_specs=[row, st],
        out_shape=[jax.ShapeDtypeStruct((nb * t_len, D_C), F32), jax.ShapeDtypeStruct((nb, POOL_HDR, D_C), F32)],
        scratch_shapes=[pltpu.VMEM((POOL_HDR + tc, D_C), F32)],
        compiler_params=_params(2),
        name="pool_mix",
    )(u, b0, wph, wpl, scale)
    return pooled, bnew[:, POOL_HDR - POOL_BUF:]


def _top_blocks(gate, lane_f, valid, n_pick):
    sel = jnp.zeros(gate.shape, jnp.bool_)
    g = jnp.where(valid, gate, -jnp.inf)
    picks = []
    for _ in range(n_pick):
        v = jnp.max(g, axis=-1, keepdims=True)
        idx = jnp.min(jnp.where(g == v, lane_f, float(V7X_LANES)), axis=-1, keepdims=True)
        hit = (lane_f == idx) & valid
        sel = sel | hit
        g = jnp.where(hit, -jnp.inf, g)
        picks.append(idx)
    return sel, picks


def _moba_prompt_kernel(q_ref, k_ref, v_ref, o_ref, kmean_sc):
    i = pl.program_id(2)
    blk = MOBA_BLOCK
    t_len = k_ref.shape[0]
    scale = HD_D ** -0.5

    @pl.when(i == 0)
    def _():
        ind = jnp.where((_iota((V7X_LANES, t_len), 1) >> 8) == _iota((V7X_LANES, t_len), 0), 1.0, 0.0)
        kmean_sc[...] = _xdot(ind.astype(BF16), k_ref[...]) * (1.0 / blk)

    q = q_ref[...]
    causal = _iota((blk, blk), 1) <= _iota((blk, blk), 0)
    lane = _iota((blk, V7X_LANES), 1)
    lane_f = lane.astype(F32)
    own0 = pl.multiple_of(i * blk, blk)
    k_own = k_ref[pl.ds(own0, blk), :]
    v_own = v_ref[pl.ds(own0, blk), :]
    for hh in range(V7X_LANES // HD_D):
        hs = slice(hh * HD_D, (hh + 1) * HD_D)
        qh = q[:, hs]
        gate = _dot3(qh, kmean_sc[:, hs], NT)
        sel, _ = _top_blocks(gate, lane_f, lane < i, MOBA_TOPK)
        sel_f = jnp.where(sel, 1.0, 0.0)
        s = jnp.where(causal, _dot3(qh, k_own[:, hs], NT) * scale, -jnp.inf)
        m = jnp.max(s, axis=-1, keepdims=True)
        p = jnp.exp(s - m)
        l = jnp.sum(p, axis=-1, keepdims=True)
        acc = _dot3(p, v_own[:, hs])

        def body(j, carry, qh=qh, sel_f=sel_f, hs=hs):
            m, l, acc = carry
            r0 = pl.multiple_of(j * blk, blk)
            sj = _dot3(qh, k_ref[pl.ds(r0, blk), :][:, hs], NT) * scale
            on = jnp.sum(jnp.where(lane == j, sel_f, 0.0), axis=-1, keepdims=True) > 0.0
            sj = jnp.where(on, sj, -jnp.inf)
            m2 = jnp.maximum(m, jnp.max(sj, axis=-1, keepdims=True))
            a = jnp.exp(m - m2)
            pj = jnp.exp(sj - m2)
            return m2, a * l + jnp.sum(pj, axis=-1, keepdims=True), a * acc + _dot3(pj, v_ref[pl.ds(r0, blk), :][:, hs])

        m, l, acc = lax.fori_loop(0, i, body, (m, l, acc))
        o_ref[:, hs] = acc / l


def _moba_prompt(q, k, v, *, nb, t_len):
    blk = MOBA_BLOCK
    nq = t_len // blk
    assert t_len % blk == 0 and nq <= V7X_LANES
    qspec = pl.BlockSpec((blk, V7X_LANES), lambda b, hp, i: (b * nq + i, hp))
    kspec = pl.BlockSpec((t_len, V7X_LANES), lambda b, hp, i: (b, hp))
    return pl.pallas_call(
        _moba_prompt_kernel,
        grid=(nb, D_D // V7X_LANES, nq),
        in_specs=[qspec, kspec, kspec],
        out_specs=qspec,
        out_shape=jax.ShapeDtypeStruct((nb * t_len, D_D), F32),
        scratch_shapes=[pltpu.VMEM((V7X_LANES, V7X_LANES), F32)],
        compiler_params=_params(3),
        name="moba_prompt",
    )(q, k, v)


KM_PAGES = 16
PAGES_PER_BLOCK = MOBA_BLOCK // PAGE_SIZE


def _kmean_kernel(pt_ref, *refs):
    pages, o_ref = refs[:KM_PAGES], refs[KM_PAGES]
    for blk in range(KM_PAGES // PAGES_PER_BLOCK):
        tot = jnp.sum(pages[blk * PAGES_PER_BLOCK][0], axis=0)
        for p in range(1, PAGES_PER_BLOCK):
            tot = tot + jnp.sum(pages[blk * PAGES_PER_BLOCK + p][0], axis=0)
        o_ref[0, blk] = tot * (1.0 / MOBA_BLOCK)


def _block_means(cache_k, page_table):
    db, n_pages = page_table.shape
    assert n_pages % KM_PAGES == 0
    page_spec = lambda j: pl.BlockSpec((1, PAGE_SIZE, H_D, HD_D), lambda b, c, pt: (pt[b, c * KM_PAGES + j], 0, 0, 0))
    grid_spec = pltpu.PrefetchScalarGridSpec(
        num_scalar_prefetch=1,
        grid=(db, n_pages // KM_PAGES),
        in_specs=[page_spec(j) for j in range(KM_PAGES)],
        out_specs=pl.BlockSpec((1, KM_PAGES // PAGES_PER_BLOCK, H_D, HD_D), lambda b, c, pt: (b, c, 0, 0)),
    )
    return pl.pallas_call(
        _kmean_kernel,
        grid_spec=grid_spec,
        out_shape=jax.ShapeDtypeStruct((db, n_pages // PAGES_PER_BLOCK, H_D, HD_D), F32),
        compiler_params=_params(2),
        name="moba_block_means",
    )(page_table, *([cache_k] * KM_PAGES))


def _select_kernel(q_ref, km_ref, o_ref):
    q = q_ref[...]
    n_blk = km_ref.shape[2]
    lane_f = _iota((q.shape[0], n_blk), 1).astype(F32)
    out_lane = _iota((q.shape[0], V7X_LANES), 1)
    for h in range(H_D):
        gate = _dot3(q[:, h * HD_D:(h + 1) * HD_D], km_ref[0, h], NT)
        _, picks = _top_blocks(gate, lane_f, lane_f >= 0.0, MOBA_TOPK)
        res = jnp.zeros((q.shape[0], V7X_LANES), F32)
        for r, idx in enumerate(picks):
            res = jnp.where(out_lane == r, idx, res)
        o_ref[0, h] = res.astype(jnp.int32)


def _select_blocks(q, kmean_t):
    db = kmean_t.shape[0]
    return pl.pallas_call(
        _select_kernel,
        grid=(db,),
        in_specs=[pl.BlockSpec((SAMPLE_PAD, D_D), lambda b: (b, 0)),
                  pl.BlockSpec((1,) + kmean_t.shape[1:], lambda b: (b, 0, 0, 0))],
        out_specs=pl.BlockSpec((1, H_D, SAMPLE_PAD, V7X_LANES), lambda b: (b, 0, 0, 0)),
        out_shape=jax.ShapeDtypeStruct((db, H_D, SAMPLE_PAD, V7X_LANES), jnp.int32),
        compiler_params=_params(1),
        name="moba_select",
    )(q, kmean_t)


HEADS_PER_STEP = V7X_LANES // HD_D


def _attend_kernel(phys_ref, q_ref, kn_ref, vn_ref, ck_hbm, cv_hbm, o_ref, kbuf, vbuf, sem, *, n_q):
    b = pl.program_id(0)
    hp = pl.program_id(1)
    scale = HD_D ** -0.5
    sel_keys = MOBA_TOPK * MOBA_BLOCK

    def page_copy(hh, s, r, pg, which):
        h = hp * HEADS_PER_STEP + hh
        flat = (((b * H_D + h) * n_q + s) * MOBA_TOPK + r) * PAGES_PER_BLOCK + pg
        src, dst = (ck_hbm, kbuf) if which == 0 else (cv_hbm, vbuf)
        rows = pl.ds((r * PAGES_PER_BLOCK + pg) * PAGE_SIZE, PAGE_SIZE)
        return pltpu.make_async_copy(src.at[phys_ref[flat], :, h, :], dst.at[hh, s, rows, :], sem.at[which])

    combos = [(hh, s, r, pg, which) for hh in range(HEADS_PER_STEP) for s in range(n_q)
              for r in range(MOBA_TOPK) for pg in range(PAGES_PER_BLOCK) for which in range(2)]
    for c in combos:
        page_copy(*c).start()
    for c in combos:
        page_copy(*c).wait()

    q = q_ref[...]
    rows_n = q.shape[0]
    rowi = _iota((rows_n, rows_n), 0)
    coli = _iota((rows_n, rows_n), 1)
    own_ok = (coli <= rowi) & (coli < n_q)
    rsel = _iota((rows_n, HD_D), 0)
    for hh in range(HEADS_PER_STEP):
        hs = slice(hh * HD_D, (hh + 1) * HD_D)
        qh = q[:, hs]
        s_own = jnp.where(own_ok, _dot3(qh, kn_ref[...][:, hs], NT) * scale, -jnp.inf)
        m_own = jnp.max(s_own, axis=-1, keepdims=True)
        out = jnp.zeros((rows_n, HD_D), F32)
        for s in range(n_q):
            s_sel = _dot3(qh, kbuf[hh, s], NT) * scale
            m = jnp.maximum(m_own, jnp.max(s_sel, axis=-1, keepdims=True))
            p_sel = jnp.exp(s_sel - m)
            p_own = jnp.exp(s_own - m)
            l = jnp.sum(p_sel, axis=-1, keepdims=True) + jnp.sum(p_own, axis=-1, keepdims=True)
            o_s = (_dot3(p_sel, vbuf[hh, s]) + _dot3(p_own, vn_ref[...][:, hs])) / l
            out = jnp.where(rsel == s, o_s, out)
        o_ref[:, hs] = out
    del sel_keys


def _moba_sample_attend(phys, q, k_new, v_new, cache_k, cache_v, *, db, n_q):
    row = pl.BlockSpec((SAMPLE_PAD, V7X_LANES), lambda b, hp, ph: (b, hp))
    any_spec = pl.BlockSpec(memory_space=pl.ANY)
    buf = pltpu.VMEM((HEADS_PER_STEP, n_q, MOBA_TOPK * MOBA_BLOCK, HD_D), F32)
    grid_spec = pltpu.PrefetchScalarGridSpec(
        num_scalar_prefetch=1,
        grid=(db, D_D // V7X_LANES),
        in_specs=[row, row, row, any_spec, any_spec],
        out_specs=row,
        scratch_shapes=[buf, buf, pltpu.SemaphoreType.DMA((2,))],
    )
    return pl.pallas_call(
        functools.partial(_attend_kernel, n_q=n_q),
        grid_spec=grid_spec,
        out_shape=jax.ShapeDtypeStruct((db * SAMPLE_PAD, D_D), F32),
        compiler_params=_params(2),
        name="moba_sample_attend",
    )(phys, q, k_new, v_new, cache_k, cache_v)


def _outproj2_ln_kernel(a1_ref, a2_ref, x_ref, w1h_ref, w1l_ref, w2h_ref, w2l_ref, g_ref, b_ref, o_ref):
    y = _dot_hl(*_split2(a1_ref[...]), w1h_ref[...], w1l_ref[...])
    y = y + _dot_hl(*_split2(a2_ref[...]), w2h_ref[...], w2l_ref[...])
    o_ref[...] = _layer_norm(DN_ALPHA * x_ref[...] + y, g_ref[...], b_ref[...])


def _outproj2_ln(a1, a2, x, w1, w2, g, b, *, n_rows, tm):
    d = x.shape[1]
    assert n_rows % tm == 0
    row = lambda c: pl.BlockSpec((tm, c), lambda i: (i, 0))
    consts = [w1[0], w1[1], w2[0], w2[1], g, b]
    return pl.pallas_call(
        _outproj2_ln_kernel,
        grid=(n_rows // tm,),
        in_specs=[row(a1.shape[1]), row(a2.shape[1]), row(d)] + [_const_spec(c.shape) for c in consts],
        out_specs=row(d),
        out_shape=jax.ShapeDtypeStruct((n_rows, d), F32),
        compiler_params=_params(1),
        name="outproj2_ln",
    )(a1, a2, x, *consts)


SAMPLE_PAD = 16


def _pad_rows(x, t_pad):
    nb, t, d = x.shape
    return jnp.zeros((nb, t_pad, d), x.dtype).at[:, :t].set(x).reshape(nb * t_pad, d)


def kernel(x_prompt, x_sample, state_gla, state_ssm, state_conv, state_pool, cache_k, cache_v, page_table, ab_w_in, ab_w_gate2, ab_b_gate, ab_gla_norm, ab_conv_w, ab_conv_b, ab_dt_bias, ab_a_log, ab_d_skip, ab_ssm_norm, ab_w_out, cd_w_in, cd_w_pool, cd_pool_scale, cd_w_out, moe_w_coarse, moe_b_coarse, moe_w_fine, moe_b_fine, moe_w_gate_up, moe_w_down, ln_g, ln_b):
    bp, tp, d = x_prompt.shape
    bs, ts, _ = x_sample.shape
    (wih, wil), ab_prm, (woh, wol) = _ab_prepare(ab_w_in, ab_w_gate2, ab_b_gate, ab_gla_norm, ab_conv_w,
                                                 ab_conv_b, ab_dt_bias, ab_a_log, ab_d_skip, ab_ssm_norm, ab_w_out)
    n_p, n_s = bp * tp, bs * ts
    n_pages = page_table.shape[1]
    past = n_pages * PAGE_SIZE
    assert past % MOBA_BLOCK == 0 and past // MOBA_BLOCK >= MOBA_TOPK and ts <= SAMPLE_PAD
    zeros = lambda *s: jnp.zeros(s, F32)
    ln = lambda l, j: (ln_g[l, j].reshape(1, d), ln_b[l, j].reshape(1, d))
    unpad = lambda rows: rows.reshape(bs, SAMPLE_PAD, -1)[:, :ts]
    moe = lambda l, x: _moe_layer(x, moe_w_coarse[l], moe_b_coarse[l], moe_w_fine[l], moe_b_fine[l],
                                  moe_w_gate_up[l], moe_w_down[l], *ln(l, 1), tm=V7X_LANES)

    xp = x_prompt.reshape(n_p, d)
    xs = _pad_rows(x_sample, SAMPLE_PAD)
    proj_p = _proj(xp, wih, wil)
    oy_p, gla_p, ssm_p, conv_p = _scan_from_proj(
        proj_p, zeros(bp, H_A, DK_A, DV_A), zeros(bp, H_B, P_B, N_B), zeros(bp, CONV_W - 1, CONV_DIM), ab_prm,
        nb=bp, t_len=tp, tc=256, chunk=CHUNK, valid=256)
    xp = _outproj_ln(oy_p, xp, woh, wol, *ln(0, 0))
    proj_s = _proj(xs, wih, wil)
    oy_s, gla_s, ssm_s, conv_s = _scan_from_proj(
        proj_s, state_gla, state_ssm, state_conv, ab_prm,
        nb=bs, t_len=SAMPLE_PAD, tc=SAMPLE_PAD, chunk=SAMPLE_PAD, valid=ts)
    xs = _outproj_ln(oy_s, xs, woh, wol, *ln(0, 0))
    x_all = moe(0, jnp.concatenate([xp, unpad(xs).reshape(n_s, d)], axis=0))

    wch, wcl = _split_w(cd_w_in)
    wph, wpl = _split_w(cd_w_pool)
    pscale = cd_pool_scale.reshape(1, D_C)
    wo_c, wo_d = _split_w(cd_w_out[:D_C]), _split_w(cd_w_out[D_C:])
    cos_p, sin_p, perm = _rope_tables(jnp.arange(tp))
    u, q, k, v = _cdproj(x_all, wch, wcl, cos_p, sin_p, perm, n_rows=n_p, tm=256)
    pooled, pool_p = _pool(u, zeros(bp, POOL_BUF, D_C), wph, wpl, pscale, nb=bp, t_len=tp, tc=256, valid=256, start=0)
    att = _moba_prompt(q, k, v, nb=bp, t_len=tp)
    xp = _outproj2_ln(pooled, att, x_all, wo_c, wo_d, *ln(1, 0), n_rows=n_p, tm=256)
    k_p, v_p = k.reshape(bp, tp, H_D, HD_D), v.reshape(bp, tp, H_D, HD_D)

    xs = _pad_rows(x_all[n_p:].reshape(bs, ts, d), SAMPLE_PAD)
    cos_s, sin_s, _ = _rope_tables(past + jnp.arange(SAMPLE_PAD))
    n_rows_s = bs * SAMPLE_PAD
    u, q, k, v = _cdproj(xs, wch, wcl, jnp.tile(cos_s, (bs, 1)), jnp.tile(sin_s, (bs, 1)), perm,
                         n_rows=n_rows_s, tm=256)
    pooled, pool_s = _pool(u, state_pool, wph, wpl, pscale, nb=bs, t_len=SAMPLE_PAD, tc=SAMPLE_PAD, valid=ts,
                           start=past)
    kmean = _block_means(cache_k, page_table)
    picks = _select_blocks(q, jnp.transpose(kmean, (0, 2, 1, 3)))[:, :, :ts, :MOBA_TOPK]
    logical = picks[..., None] * PAGES_PER_BLOCK + jnp.arange(PAGES_PER_BLOCK)
    phys = page_table[jnp.arange(bs)[:, None, None, None, None], logical].reshape(-1)
    att = _moba_sample_attend(phys, q, k, v, cache_k, cache_v, db=bs, n_q=ts)
    xs = _outproj2_ln(pooled, att, xs, wo_c, wo_d, *ln(1, 0), n_rows=n_rows_s, tm=256)
    k_s, v_s = unpad(k).reshape(bs, ts, H_D, HD_D), unpad(v).reshape(bs, ts, H_D, HD_D)
    x_all = moe(1, jnp.concatenate([xp, unpad(xs).reshape(n_s, d)], axis=0))

    return (x_all[:n_p].reshape(bp, tp, d), x_all[n_p:].reshape(bs, ts, d), gla_p, ssm_p, conv_p, pool_p, k_p, v_p,
            gla_s, ssm_s, conv_s, pool_s, k_s, v_s)
```

```python
import functools

import jax
import jax.numpy as jnp
import numpy as np
from jax import lax
from jax.experimental import pallas as pl
from jax.experimental.pallas import tpu as pltpu

F32 = jnp.float32
BF16 = jnp.bfloat16

D_MODEL = 1024
DEPTH = 2
PAGE_SIZE = 128
H_A, DK_A, DV_A = 4, 64, 128
GATE_RANK = 16
GATE_TAU = 16.0
CHUNK = 64
H_B, P_B, N_B, G_B = 8, 64, 64, 2
CONV_W = 4
D_B = H_B * P_B
CONV_DIM = D_B + 2 * G_B * N_B
POOL_WINDOWS = (2, 4, 8, 16)
POOL_GROUP = 128
D_C = len(POOL_WINDOWS) * POOL_GROUP
POOL_BUF = max(POOL_WINDOWS) - 1
H_D, HD_D = 8, 64
D_D = H_D * HD_D
MOBA_BLOCK = 256
MOBA_TOPK = 3
ROT_DIM = HD_D // 4
ROPE_THETA = 500000.0
N_GROUPS, EXP_PER_GROUP, TOP_FINE, D_EXPERT = 4, 8, 2, 256
N_EXPERTS = N_GROUPS * EXP_PER_GROUP
DN_ALPHA = (2 * DEPTH) ** 0.25
LN_EPS = 1e-5
RMS_EPS = 1e-6

V7X_LANES = 128
V7X_SUBLANES = 8
V7X_VMEM_LIMIT_BYTES = 56 * 1024 * 1024

NN = (((1,), (0,)), ((), ()))
NT = (((1,), (1,)), ((), ()))
TN = (((0,), (0,)), ((), ()))


def _params(n_axes):
    return pltpu.CompilerParams(dimension_semantics=("arbitrary",) * n_axes,
                                vmem_limit_bytes=V7X_VMEM_LIMIT_BYTES)


def _const_spec(shape):
    zeros = (0,) * len(shape)
    return pl.BlockSpec(shape, lambda *_: zeros, pipeline_mode=pl.Buffered(1))


def _split2(a):
    hi = a.astype(BF16)
    lo = (a - hi.astype(F32)).astype(BF16)
    return hi, lo


def _split3(a):
    hi = a.astype(BF16)
    r = a - hi.astype(F32)
    mid = r.astype(BF16)
    lo = (r - mid.astype(F32)).astype(BF16)
    return hi, mid, lo


def _mm(a, b, dims=NN):
    return lax.dot_general(a, b, dims, preferred_element_type=F32)


def _dot_hl(ah, al, bh, bl, dims=NN):
    return (_mm(ah, bl, dims) + _mm(al, bh, dims)) + _mm(ah, bh, dims)


def _dot3(a, b, dims=NN):
    ah, al = _split2(a)
    bh, bl = _split2(b)
    return _dot_hl(ah, al, bh, bl, dims)


def _dotx(a, e, dims=NN):
    h, m, l = _split3(a)
    return (_mm(l, e, dims) + _mm(m, e, dims)) + _mm(h, e, dims)


def _xdot(e, a, dims=NN):
    h, m, l = _split3(a)
    return (_mm(e, l, dims) + _mm(e, m, dims)) + _mm(e, h, dims)


def _silu(x):
    return x / (1.0 + jnp.exp(-x))


def _softplus(x):
    return jnp.maximum(x, 0.0) + jnp.log1p(jnp.exp(-jnp.abs(x)))


def _log_sigmoid(x):
    return jnp.minimum(x, 0.0) - jnp.log1p(jnp.exp(-jnp.abs(x)))


def _layer_norm(z, g, b):
    mu = jnp.mean(z, axis=-1, keepdims=True)
    zc = z - mu
    var = jnp.mean(zc * zc, axis=-1, keepdims=True)
    return zc * lax.rsqrt(var + LN_EPS) * g + b


def _iota(shape, dim):
    return lax.broadcasted_iota(jnp.int32, shape, dim)


def _proj_kernel(x_ref, wh_ref, wl_ref, o_ref, *, col_chunk):
    xh, xl = _split2(x_ref[...])
    m = o_ref.shape[1]
    for j in range(0, m, col_chunk):
        cs = slice(j, min(j + col_chunk, m))
        o_ref[:, cs] = _dot_hl(xh, xl, wh_ref[:, cs], wl_ref[:, cs])


def _proj(x, wh, wl, *, n_rows=None, tm=256, col_chunk=512):
    n = x.shape[0] if n_rows is None else n_rows
    k = x.shape[1]
    m = wh.shape[1]
    assert n % tm == 0
    return pl.pallas_call(
        functools.partial(_proj_kernel, col_chunk=col_chunk),
        grid=(n // tm,),
        in_specs=[pl.BlockSpec((tm, k), lambda i: (i, 0)), _const_spec((k, m)), _const_spec((k, m))],
        out_specs=pl.BlockSpec((tm, m), lambda i: (i, 0)),
        out_shape=jax.ShapeDtypeStruct((n, m), F32),
        compiler_params=_params(1),
        name="proj",
    )(x, wh, wl)


def _outproj_ln_kernel(a_ref, x_ref, wh_ref, wl_ref, g_ref, b_ref, o_ref):
    ah, al = _split2(a_ref[...])
    y = _dot_hl(ah, al, wh_ref[...], wl_ref[...])
    o_ref[...] = _layer_norm(DN_ALPHA * x_ref[...] + y, g_ref[...], b_ref[...])


def _outproj_ln(a, x, wh, wl, g, b, *, tm=256):
    n, k = a.shape
    d = wh.shape[1]
    assert n % tm == 0
    return pl.pallas_call(
        _outproj_ln_kernel,
        grid=(n // tm,),
        in_specs=[pl.BlockSpec((tm, k), lambda i: (i, 0)), pl.BlockSpec((tm, d), lambda i: (i, 0)),
                  _const_spec((k, d)), _const_spec((k, d)), _const_spec((1, d)), _const_spec((1, d))],
        out_specs=pl.BlockSpec((tm, d), lambda i: (i, 0)),
        out_shape=jax.ShapeDtypeStruct((n, d), F32),
        compiler_params=_params(1),
        name="outproj_ln",
    )(a, x, wh, wl, g, b)


SM_GLR0 = 0
SM_DT0 = GATE_RANK
CONV_HDR = V7X_SUBLANES


def _scan_kernel(qkvr_ref, z_ref, xbc_ref, sm_ref, s0_ref, h0_ref, c0_ref,
                 wg2h_ref, wg2l_ref, bg_ref, gnorm_ref, cw_ref, cb_ref, dtb_ref, alog_ref,
                 dskip_ref, snorm_ref,
                 oy_ref, s_out_ref, h_out_ref, c_out_ref,
                 st_sc, h_sc, ext_sc, o_sc, y_sc, *, tc, chunk, valid):
    t = pl.program_id(1)
    nt = pl.num_programs(1)
    n_chunks = tc // chunk
    shift = chunk.bit_length() - 1
    assert (1 << shift) == chunk and tc % chunk == 0

    @pl.when(t == 0)
    def _():
        st_sc[...] = s0_ref[0]
        h_sc[...] = h0_ref[0]
        ext_sc[0:CONV_HDR, :] = c0_ref[0]

    row = _iota((tc, tc), 0)
    col = _iota((tc, tc), 1)
    causal = ((row >> shift) == (col >> shift)) & (col <= row)
    l_tri = jnp.where(causal, 1.0, 0.0).astype(BF16)
    lane = _iota((tc, V7X_LANES), 1)
    rowv = _iota((tc, V7X_LANES), 0)
    dt_lane = (lane >= SM_DT0) & (lane < SM_DT0 + H_B)
    if valid < tc:
        row_ok = rowv < valid
        dt_lane = dt_lane & row_ok
    e8 = jnp.where((_iota((V7X_LANES, D_B), 1) >> 6) + SM_DT0 == _iota((V7X_LANES, D_B), 0), 1.0, 0.0).astype(BF16)

    sm = sm_ref[...]
    pre = _dot_hl(*_split2(sm), wg2h_ref[...], wg2l_ref[...])
    log_a = _log_sigmoid(pre + bg_ref[...]) / GATE_TAU
    if valid < tc:
        log_a = jnp.where(_iota((tc, H_A * DK_A), 0) < valid, log_a, 0.0)
    b_cum = _xdot(l_tri, log_a)
    dt_pad = jnp.where(dt_lane, _softplus(sm + dtb_ref[...]), 0.0)
    a_pad = -jnp.exp(alog_ref[...])
    la_pad = jnp.where(dt_lane, dt_pad * a_pad, 0.0)
    cum_pad = _xdot(l_tri, la_pad)
    dt_rep = _dotx(dt_pad, e8)
    cum_rep = _dotx(cum_pad, e8)

    ext_sc[CONV_HDR:CONV_HDR + tc, :] = xbc_ref[...]
    cw = cw_ref[...]
    acc = ext_sc[CONV_HDR - 3:CONV_HDR - 3 + tc, :] * cw[0:1, :]
    for i in range(1, CONV_W):
        acc = acc + ext_sc[CONV_HDR - 3 + i:CONV_HDR - 3 + i + tc, :] * cw[i:i + 1, :]
    xc = _silu(acc + cb_ref[...])

    @pl.when(t == nt - 1)
    def _():
        c_out_ref[0] = ext_sc[valid:valid + CONV_HDR, :]

    ext_sc[0:CONV_HDR, :] = ext_sc[tc:tc + CONV_HDR, :]

    xs = xc[:, 0:D_B]
    bm = xc[:, D_B:D_B + G_B * N_B]
    cm = xc[:, D_B + G_B * N_B:CONV_DIM]
    xdt = xs * dt_rep

    qkvr = qkvr_ref[...]
    q = qkvr[:, 0:H_A * DK_A] * (DK_A ** -0.5)
    k = qkvr[:, H_A * DK_A:2 * H_A * DK_A]
    v = qkvr[:, 2 * H_A * DK_A:2 * H_A * DK_A + H_A * DV_A]
    r = qkvr[:, 2 * H_A * DK_A + H_A * DV_A:]
    q_dec = q * jnp.exp(b_cum)
    k_dec = k * jnp.exp(-b_cum)
    for h in range(H_A):
        ks = slice(h * DK_A, (h + 1) * DK_A)
        vs = slice(h * DV_A, (h + 1) * DV_A)
        att = jnp.where(causal, _dot3(q_dec[:, ks], k_dec[:, ks], NT), 0.0)
        o_sc[:, vs] = _dot3(att, v[:, vs])
    for c in range(n_chunks):
        rows = slice(c * chunk, (c + 1) * chunk)
        last = slice((c + 1) * chunk - 1, (c + 1) * chunk)
        b_last = b_cum[last, :]
        k_end = k[rows, :] * jnp.exp(b_last - b_cum[rows, :])
        dec = jnp.exp(b_last)
        for h in range(H_A):
            ks = slice(h * DK_A, (h + 1) * DK_A)
            vs = slice(h * DV_A, (h + 1) * DV_A)
            st = st_sc[h]
            o_sc[rows, vs] = o_sc[rows, vs] + _dot3(q_dec[rows, ks], st, NT)
            st_sc[h] = st * dec[:, ks] + _dot3(v[rows, vs], k_end[:, ks], TN)

    for g in range(G_B):
        gs = slice(g * N_B, (g + 1) * N_B)
        cb = _dot3(cm[:, gs], bm[:, gs], NT)
        for hh in range(g * (H_B // G_B), (g + 1) * (H_B // G_B)):
            ps = slice(hh * P_B, (hh + 1) * P_B)
            ea = jnp.where(_iota((V7X_LANES, tc), 0) == SM_DT0 + hh, 1.0, 0.0).astype(BF16)
            m1 = jnp.where(_iota((tc, V7X_LANES), 1) == SM_DT0 + hh, 1.0, 0.0).astype(BF16)
            seg = _dotx(cum_pad, ea) - _xdot(m1, cum_pad, NT)
            w = cb * jnp.exp(jnp.where(causal, seg, -jnp.inf))
            y_sc[:, ps] = _dot3(w, xdt[:, ps])
    for c in range(n_chunks):
        rows = slice(c * chunk, (c + 1) * chunk)
        last = slice((c + 1) * chunk - 1, (c + 1) * chunk)
        cum_c = cum_rep[rows, :]
        cum_l = cum_rep[last, :]
        e_cum = jnp.exp(cum_c)
        x_dec = xdt[rows, :] * jnp.exp(cum_l - cum_c)
        e_last = jnp.exp(cum_l)
        for hh in range(H_B):
            g = hh // (H_B // G_B)
            gs = slice(g * N_B, (g + 1) * N_B)
            ps = slice(hh * P_B, (hh + 1) * P_B)
            hs = h_sc[hh]
            y_sc[rows, ps] = y_sc[rows, ps] + _dot3(cm[rows, gs], hs, NT) * e_cum[:, ps]
            h_sc[hh] = hs * e_last[:, ps] + _dot3(x_dec[:, ps], bm[rows, gs], TN)

    gn = gnorm_ref[...]
    for h in range(H_A):
        vs = slice(h * DV_A, (h + 1) * DV_A)
        oh = o_sc[:, vs]
        oh = oh * lax.rsqrt(jnp.mean(oh * oh, axis=-1, keepdims=True) + RMS_EPS) * gn
        oy_ref[:, vs] = oh * _silu(r[:, vs])
    y = (y_sc[...] + dskip_ref[...] * xs) * _silu(z_ref[...])
    y = y * lax.rsqrt(jnp.mean(y * y, axis=-1, keepdims=True) + RMS_EPS) * snorm_ref[...]
    oy_ref[:, H_A * DV_A:] = y

    @pl.when(t == nt - 1)
    def _():
        s_out_ref[0] = st_sc[...]
        h_out_ref[0] = h_sc[...]


def _scan_call(proj, s0t, h0, c0, prm, *, specs, nb, t_len, tc, chunk, valid):
    nt = t_len // tc
    assert t_len % tc == 0 and (valid == tc or nt == 1)
    rowspec = lambda c: pl.BlockSpec((tc, c), lambda b, t: (b * nt + t, 0))
    stspec = lambda shp: pl.BlockSpec((1,) + shp, lambda b, t: (b,) + (0,) * len(shp))
    names = ("wg2h", "wg2l", "bg", "gnorm", "cw", "cb", "dtb", "alog", "dskip", "snorm")
    consts = [prm[n] for n in names]
    return pl.pallas_call(
        functools.partial(_scan_kernel, tc=tc, chunk=chunk, valid=valid),
        grid=(nb, nt),
        in_specs=list(specs)
                 + [stspec((H_A, DV_A, DK_A)), stspec((H_B, P_B, N_B)), stspec((CONV_HDR, CONV_DIM))]
                 + [_const_spec(c.shape) for c in consts],
        out_specs=[rowspec(H_A * DV_A + D_B), stspec((H_A, DV_A, DK_A)), stspec((H_B, P_B, N_B)),
                   stspec((CONV_HDR, CONV_DIM))],
        out_shape=[jax.ShapeDtypeStruct((nb * t_len, H_A * DV_A + D_B), F32),
                   jax.ShapeDtypeStruct((nb, H_A, DV_A, DK_A), F32),
                   jax.ShapeDtypeStruct((nb, H_B, P_B, N_B), F32),
                   jax.ShapeDtypeStruct((nb, CONV_HDR, CONV_DIM), F32)],
        scratch_shapes=[pltpu.VMEM((H_A, DV_A, DK_A), F32), pltpu.VMEM((H_B, P_B, N_B), F32),
                        pltpu.VMEM((tc + CONV_HDR, CONV_DIM), F32),
                        pltpu.VMEM((tc, H_A * DV_A), F32), pltpu.VMEM((tc, D_B), F32)],
        compiler_params=_params(2),
        name="gla_ssd_scan",
    )(proj, proj, proj, proj, s0t, h0, c0, *consts)


AB_QKVR = 2 * H_A * DK_A + 2 * H_A * DV_A
AB_XBC0 = AB_QKVR
AB_SM0 = AB_XBC0 + CONV_DIM
AB_Z0 = 2560
AB_COLS = AB_Z0 + D_B


def _split_w(w):
    hi = w.astype(BF16)
    lo = (w - hi.astype(F32)).astype(BF16)
    return hi, lo


def _lane_pad(v, start, width=V7X_LANES):
    out = jnp.zeros((1, width), F32)
    return out.at[0, start:start + v.shape[0]].set(v.astype(F32))


def _ab_prepare(w_in, w_gate2, b_gate, gla_norm, conv_w, conv_b, dt_bias, a_log, d_skip, ssm_norm, w_out):
    o = np.cumsum([0, H_A * DK_A, H_A * DK_A, H_A * DV_A, H_A * DV_A, GATE_RANK, D_B, CONV_DIM, H_B])
    q0, gl0, z0, xbc0, dt0, end = o[0], o[4], o[5], o[6], o[7], o[8]
    k_dim = w_in.shape[0]
    w = jnp.concatenate([
        w_in[:, q0:gl0], w_in[:, xbc0:dt0], w_in[:, gl0:z0], w_in[:, dt0:end],
        jnp.zeros((k_dim, AB_Z0 - AB_SM0 - GATE_RANK - H_B), F32), w_in[:, z0:xbc0]], axis=1)
    assert w.shape[1] == AB_COLS
    wg2 = jnp.zeros((V7X_LANES, H_A * DK_A), F32).at[0:GATE_RANK].set(w_gate2)
    wg2h, wg2l = _split_w(wg2)
    prm = dict(
        wg2h=wg2h, wg2l=wg2l, bg=b_gate.reshape(1, -1), gnorm=gla_norm.reshape(1, -1),
        cw=conv_w, cb=conv_b.reshape(1, -1), dtb=_lane_pad(dt_bias, SM_DT0), alog=_lane_pad(a_log, SM_DT0),
        dskip=jnp.repeat(d_skip, P_B).reshape(1, -1), snorm=ssm_norm.reshape(1, -1))
    return _split_w(w), prm, _split_w(w_out)


def _scan_from_proj(proj, s0, h0, c0, prm, *, nb, t_len, tc, chunk, valid):
    nt = t_len // tc
    s0t = jnp.swapaxes(s0, -1, -2)
    c0p = jnp.zeros((nb, CONV_HDR, CONV_DIM), F32).at[:, CONV_HDR - (CONV_W - 1):].set(c0)
    colspec = lambda width, start: pl.BlockSpec((tc, width), lambda b, t: (b * nt + t, start // width))
    oy, s_new, h_new, c_new = _scan_call(
        proj, s0t, h0, c0p, prm,
        specs=[colspec(AB_QKVR, 0), colspec(D_B, AB_Z0), colspec(CONV_DIM, AB_XBC0), colspec(V7X_LANES, AB_SM0)],
        nb=nb, t_len=t_len, tc=tc, chunk=chunk, valid=valid)
    return oy, jnp.swapaxes(s_new, -1, -2), h_new, c_new[:, CONV_HDR - (CONV_W - 1):]


RT_FINE0 = N_GROUPS
RT_HALF = V7X_LANES // 2
MOE_TILE = 256


def _router_kernel(x_ref, wh_ref, wl_ref, b_ref, id_ref, gate_ref):
    xh, xl = _split2(x_ref[...])
    logits = _dot_hl(xh, xl, wh_ref[...], wl_ref[...]) + b_ref[...]
    shp = logits.shape
    lane = _iota(shp, 1)
    lane_f = lane.astype(F32)
    big = float(V7X_LANES)
    ninf = -jnp.inf

    def first_max(mask):
        v = jnp.max(jnp.where(mask, logits, ninf), axis=-1, keepdims=True)
        i = jnp.min(jnp.where(mask & (logits == v), lane_f, big), axis=-1, keepdims=True)
        return v, i

    is_c = lane < N_GROUPS
    mc, grp = first_max(is_c)
    p_grp = 1.0 / jnp.sum(jnp.where(is_c, jnp.exp(logits - mc), 0.0), axis=-1, keepdims=True)
    fine = (lane >= RT_FINE0) & (lane < RT_FINE0 + N_EXPERTS)
    cand = fine & (((lane - RT_FINE0) >> 3).astype(F32) == grp)
    v1, i1 = first_max(cand)
    v2, i2 = first_max(cand & (lane_f != i1))
    e = jnp.exp(v2 - v1)
    w1 = p_grp / (1.0 + e)
    w2 = p_grp * (e / (1.0 + e))
    first = lane < RT_HALF
    id_ref[...] = (jnp.where(first, i1, i2) - float(RT_FINE0)).astype(jnp.int32)
    gate_ref[...] = jnp.where(first, w1, w2)


def _router(x, wh, wl, b, *, tm):
    n, d = x.shape
    assert n % tm == 0
    return pl.pallas_call(
        _router_kernel,
        grid=(n // tm,),
        in_specs=[pl.BlockSpec((tm, d), lambda i: (i, 0)), _const_spec(wh.shape), _const_spec(wl.shape),
                  _const_spec(b.shape)],
        out_specs=[pl.BlockSpec((tm, V7X_LANES), lambda i: (i, 0))] * 2,
        out_shape=[jax.ShapeDtypeStruct((n, V7X_LANES), jnp.int32), jax.ShapeDtypeStruct((n, V7X_LANES), F32)],
        compiler_params=_params(1),
        name="moe_router",
    )(x, wh, wl, b)


DMA_UNROLL = 8


def _expert_kernel(te_ref, tr_ref, src_hbm, dst_hbm, x_hbm, wgu_ref, wd_ref, y_hbm,
                   src_sm, dst_sm, xbuf, ybuf, sem_i, sem_g, sem_s, *, three_pass):
    i = pl.program_id(0)
    rows = tr_ref[i]
    tile = xbuf.shape[0]

    def idx_copy(tab_hbm, tab_sm, k):
        return pltpu.make_async_copy(tab_hbm.at[pl.ds(i, 1)], tab_sm, sem_i.at[k])

    def gather_copy(tok, r):
        return pltpu.make_async_copy(x_hbm.at[pl.ds(tok, 1)], xbuf.at[pl.ds(r, 1)], sem_g)

    def scatter_copy(dst, r):
        return pltpu.make_async_copy(ybuf.at[pl.ds(r, 1)], y_hbm.at[pl.ds(dst, 1)], sem_s)

    @pl.when(i == 0)
    def _():
        ybuf[...] = jnp.zeros(ybuf.shape, F32)
        spare = pltpu.make_async_copy(ybuf, y_hbm.at[pl.ds(y_hbm.shape[0] - tile, tile)], sem_s)
        spare.start()
        spare.wait()

    @pl.when(rows > 0)
    def _():
        idx_copy(src_hbm, src_sm, 0).start()
        idx_copy(dst_hbm, dst_sm, 1).start()
        idx_copy(src_hbm, src_sm, 0).wait()
        idx_copy(dst_hbm, dst_sm, 1).wait()

        def start_gather(r, c):
            gather_copy(src_sm[0, r], r).start()
            return c

        lax.fori_loop(0, tile, start_gather, 0, unroll=DMA_UNROLL)
        pltpu.make_async_copy(x_hbm.at[pl.ds(0, tile)], xbuf, sem_g).wait()

        if three_pass:
            xh, xl = _split2(xbuf[...])
            wh, wl = _split2(wgu_ref[0])
            hgu = _dot_hl(xh, xl, wh, wl)
        else:
            hgu = _mm(xbuf[...].astype(BF16), wgu_ref[0].astype(BF16))
        act = _silu(hgu[:, :D_EXPERT]) * hgu[:, D_EXPERT:]
        if three_pass:
            ybuf[...] = _dot3(act, wd_ref[0])
        else:
            ybuf[...] = _mm(act.astype(BF16), wd_ref[0].astype(BF16))

        def start_scatter(r, c):
            scatter_copy(dst_sm[0, r], r).start()
            return c

        lax.fori_loop(0, tile, start_scatter, 0, unroll=DMA_UNROLL)
        pltpu.make_async_copy(ybuf, y_hbm.at[pl.ds(0, tile)], sem_s).wait()


def _experts(x, w_gate_up, w_down, tile_expert, tile_rows, src_tab, dst_tab, *, three_pass):
    n, d = x.shape
    n_tiles, tile = src_tab.shape
    f2 = w_gate_up.shape[-1]
    any_spec = pl.BlockSpec(memory_space=pl.ANY)
    grid_spec = pltpu.PrefetchScalarGridSpec(
        num_scalar_prefetch=2,
        grid=(n_tiles,),
        in_specs=[any_spec, any_spec, any_spec,
                  pl.BlockSpec((1, d, f2), lambda i, te, tr: (te[i], 0, 0)),
                  pl.BlockSpec((1, f2 // 2, d), lambda i, te, tr: (te[i], 0, 0))],
        out_specs=any_spec,
        scratch_shapes=[pltpu.SMEM((1, tile), jnp.int32), pltpu.SMEM((1, tile), jnp.int32),
                        pltpu.VMEM((tile, d), F32), pltpu.VMEM((tile, d), F32),
                        pltpu.SemaphoreType.DMA((2,)), pltpu.SemaphoreType.DMA(()), pltpu.SemaphoreType.DMA(())],
    )
    return pl.pallas_call(
        functools.partial(_expert_kernel, three_pass=three_pass),
        grid_spec=grid_spec,
        out_shape=jax.ShapeDtypeStruct((2 * n + tile, d), F32),
        compiler_params=_params(1),
        name="moe_experts",
    )(tile_expert, tile_rows, src_tab, dst_tab, x, w_gate_up, w_down)


def _combine_ln_kernel(x_ref, y1_ref, y2_ref, gate_ref, g_ref, b_ref, o_ref):
    gate = gate_ref[...]
    moe = gate[:, 0:1] * y1_ref[...] + gate[:, RT_HALF:RT_HALF + 1] * y2_ref[...]
    o_ref[...] = _layer_norm(DN_ALPHA * x_ref[...] + moe, g_ref[...], b_ref[...])


def _combine_ln(x, y, gate, g, b, *, tm):
    n, d = x.shape
    nb = n // tm
    assert n % tm == 0
    row = lambda off: pl.BlockSpec((tm, d), lambda i: (i + off, 0))
    return pl.pallas_call(
        _combine_ln_kernel,
        grid=(nb,),
        in_specs=[row(0), row(0), row(nb), pl.BlockSpec((tm, V7X_LANES), lambda i: (i, 0)),
                  _const_spec((1, d)), _const_spec((1, d))],
        out_specs=row(0),
        out_shape=jax.ShapeDtypeStruct((n, d), F32),
        compiler_params=_params(1),
        name="moe_combine_ln",
    )(x, y, y, gate, g, b)


def _moe_plan(ids, n):
    eid = jnp.concatenate([ids[:, 0], ids[:, RT_HALF]])
    order = jnp.argsort(eid).astype(jnp.int32)
    counts = jnp.sum((eid[:, None] == jnp.arange(N_EXPERTS)[None, :]).astype(jnp.int32), axis=0)
    tiles_per = (counts + MOE_TILE - 1) // MOE_TILE
    tile_end = jnp.cumsum(tiles_per)
    cnt_beg = jnp.cumsum(counts) - counts
    n_tiles = -(-2 * n // MOE_TILE) + N_EXPERTS
    ti = jnp.arange(n_tiles)
    used = ti < tile_end[-1]
    te = jnp.minimum(jnp.searchsorted(tile_end, ti, side="right"), N_EXPERTS - 1)
    last_used = te[jnp.maximum(tile_end[-1] - 1, 0)]
    te = jnp.where(used, te, last_used).astype(jnp.int32)
    local = ti - (tile_end[te] - tiles_per[te])
    rows = jnp.where(used, jnp.clip(counts[te] - local * MOE_TILE, 0, MOE_TILE), 0).astype(jnp.int32)
    start = cnt_beg[te] + local * MOE_TILE
    slot = jnp.arange(MOE_TILE)[None, :]
    pos = start[:, None] + jnp.minimum(slot, jnp.maximum(rows - 1, 0)[:, None])
    asg = order[jnp.clip(pos, 0, 2 * n - 1)]
    dst = jnp.where(slot < rows[:, None], asg, 2 * n + slot)
    return te, rows, (asg % n).astype(jnp.int32), dst.astype(jnp.int32)


def _moe_prepare(w_coarse, b_coarse, w_fine, b_fine):
    d = w_coarse.shape[0]
    wf = jnp.transpose(w_fine, (1, 0, 2)).reshape(d, N_EXPERTS)
    w = jnp.zeros((d, V7X_LANES), F32).at[:, :N_GROUPS].set(w_coarse).at[:, RT_FINE0:RT_FINE0 + N_EXPERTS].set(wf)
    b = jnp.zeros((1, V7X_LANES), F32).at[0, :N_GROUPS].set(b_coarse)
    b = b.at[0, RT_FINE0:RT_FINE0 + N_EXPERTS].set(b_fine.reshape(-1))
    return _split_w(w), b


def _moe_layer(x, w_coarse, b_coarse, w_fine, b_fine, w_gate_up, w_down, g, b, *, tm, three_pass):
    n, d = x.shape
    (wrh, wrl), br = _moe_prepare(w_coarse, b_coarse, w_fine, b_fine)
    ids, gate = _router(x, wrh, wrl, br, tm=tm)
    te, rows, src, dst = _moe_plan(ids, n)
    y = _experts(x, w_gate_up.reshape(N_EXPERTS, d, 2 * D_EXPERT), w_down.reshape(N_EXPERTS, D_EXPERT, d),
                 te, rows, src, dst, three_pass=three_pass)
    return _combine_ln(x, y, gate, g, b, tm=tm)


def _cdproj_kernel(x_ref, wh_ref, wl_ref, cos_ref, sin_ref, perm_ref, u_ref, q_ref, k_ref, v_ref):
    xh, xl = _split2(x_ref[...])

    def col(j):
        cs = slice(j * D_C, (j + 1) * D_C)
        return _dot_hl(xh, xl, wh_ref[:, cs], wl_ref[:, cs])

    u_ref[...] = col(0)
    for j, ref in ((1, q_ref), (2, k_ref)):
        t = col(j)
        ref[...] = t * cos_ref[...] + _dotx(t, perm_ref[...]) * sin_ref[...]
    v_ref[...] = col(3)


def _cdproj(x, wh, wl, cos_t, sin_t, perm, *, n_rows, tm):
    k = x.shape[1]
    assert D_C == D_D and n_rows % tm == 0 and cos_t.shape[0] % tm == 0
    nt = cos_t.shape[0] // tm
    row = pl.BlockSpec((tm, D_D), lambda i: (i, 0))
    tab = pl.BlockSpec((tm, D_D), lambda i: (i % nt, 0))
    return pl.pallas_call(
        _cdproj_kernel,
        grid=(n_rows // tm,),
        in_specs=[pl.BlockSpec((tm, k), lambda i: (i, 0)), _const_spec(wh.shape), _const_spec(wl.shape),
                  tab, tab, _const_spec(perm.shape)],
        out_specs=[row] * 4,
        out_shape=[jax.ShapeDtypeStruct((n_rows, D_D), F32)] * 4,
        compiler_params=_params(1),
        name="cd_proj_rope",
    )(x, wh, wl, cos_t, sin_t, perm)


def _rope_tables(pos):
    half = ROT_DIM // 2
    inv = ROPE_THETA ** (-jnp.arange(half, dtype=F32) / half)
    ang = pos.astype(F32)[:, None] * inv
    cos, sin = jnp.cos(ang), jnp.sin(ang)
    n = pos.shape[0]
    rest = HD_D - ROT_DIM
    cos_h = jnp.concatenate([cos, cos, jnp.ones((n, rest), F32)], axis=-1)
    sin_h = jnp.concatenate([-sin, sin, jnp.zeros((n, rest), F32)], axis=-1)
    perm = np.zeros((D_D, D_D), np.float32)
    for dst in range(D_D):
        j = dst % HD_D
        if j < half:
            perm[dst + half, dst] = 1.0
        elif j < ROT_DIM:
            perm[dst - half, dst] = 1.0
    return jnp.tile(cos_h, (1, H_D)), jnp.tile(sin_h, (1, H_D)), jnp.asarray(perm, BF16)


POOL_HDR = 16


def _pool_kernel(u_ref, b0_ref, wph_ref, wpl_ref, sc_ref, o_ref, bo_ref, ext_sc, *, tc, valid, start):
    t = pl.program_id(1)
    nt = pl.num_programs(1)

    @pl.when(t == 0)
    def _():
        ext_sc[0:POOL_HDR, :] = b0_ref[0]

    ext_sc[POOL_HDR:POOL_HDR + tc, :] = u_ref[...]
    u = u_ref[...]
    rowi = _iota((tc, POOL_HDR + tc), 0) + POOL_HDR
    colj = _iota((tc, POOL_HDR + tc), 1)
    pos = (start + t * tc + _iota((tc, POOL_GROUP), 0)).astype(F32)
    for gi, w in enumerate(POOL_WINDOWS):
        ls = slice(gi * POOL_GROUP, (gi + 1) * POOL_GROUP)
        band = jnp.where((colj <= rowi) & (colj > rowi - w), 1.0, 0.0).astype(BF16)
        win = _xdot(band, ext_sc[:, ls])
        cnt = jnp.minimum(float(w), pos + 1.0)
        dh, dl = _split2(win / cnt - u[:, ls])
        o_ref[:, ls] = _dot_hl(dh, dl, wph_ref[gi], wpl_ref[gi]) * sc_ref[:, ls]

    @pl.when(t == nt - 1)
    def _():
        bo_ref[0] = ext_sc[valid:valid + POOL_HDR, :]

    ext_sc[0:POOL_HDR, :] = ext_sc[tc:tc + POOL_HDR, :]


def _pool(u, buf, wph, wpl, scale, *, nb, t_len, tc, valid, start):
    nt = t_len // tc
    assert t_len % tc == 0 and (valid == tc or nt == 1)
    b0 = jnp.zeros((nb, POOL_HDR, D_C), F32).at[:, POOL_HDR - POOL_BUF:].set(buf)
    row = pl.BlockSpec((tc, D_C), lambda b, t: (b * nt + t, 0))
    st = pl.BlockSpec((1, POOL_HDR, D_C), lambda b, t: (b, 0, 0))
    pooled, bnew = pl.pallas_call(
        functools.partial(_pool_kernel, tc=tc, valid=valid, start=start),
        grid=(nb, nt),
        in_specs=[row, st, _const_spec(wph.shape), _const_spec(wpl.shape), _const_spec(scale.shape)],
        out_specs=[row, st],
        out_shape=[jax.ShapeDtypeStruct((nb * t_len, D_C), F32), jax.ShapeDtypeStruct((nb, POOL_HDR, D_C), F32)],
        scratch_shapes=[pltpu.VMEM((POOL_HDR + tc, D_C), F32)],
        compiler_params=_params(2),
        name="pool_mix",
    )(u, b0, wph, wpl, scale)
    return pooled, bnew[:, POOL_HDR - POOL_BUF:]


def _top_blocks(gate, lane_f, valid, n_pick):
    sel = jnp.zeros(gate.shape, jnp.bool_)
    g = jnp.where(valid, gate, -jnp.inf)
    picks = []
    for _ in range(n_pick):
        v = jnp.max(g, axis=-1, keepdims=True)
        idx = jnp.min(jnp.where(g == v, lane_f, float(V7X_LANES)), axis=-1, keepdims=True)
        hit = (lane_f == idx) & valid
        sel = sel | hit
        g = jnp.where(hit, -jnp.inf, g)
        picks.append(idx)
    return sel, picks


def _cat_pieces(a, b, c):
    return jnp.concatenate([a, b, c, jnp.zeros_like(a)], axis=1)


def _moba_prompt_kernel(q_ref, k_ref, v_ref, o_ref, kmean_sc, kcat_sc, vh_sc):
    i = pl.program_id(2)
    blk = MOBA_BLOCK
    t_len = k_ref.shape[0]
    n_heads = V7X_LANES // HD_D
    scale = HD_D ** -0.5

    @pl.when(i == 0)
    def _():
        ind = jnp.where((_iota((V7X_LANES, t_len), 1) >> 8) == _iota((V7X_LANES, t_len), 0), 1.0, 0.0)
        kmean_sc[...] = _xdot(ind.astype(BF16), k_ref[...]) * (1.0 / blk)
        for n in range(t_len // blk):
            rows = slice(n * blk, (n + 1) * blk)
            kh, kl = _split2(k_ref[rows, :])
            vh = v_ref[rows, :].astype(BF16)
            for hh in range(n_heads):
                hs = slice(hh * HD_D, (hh + 1) * HD_D)
                kcat_sc[hh, rows, :] = _cat_pieces(kh[:, hs], kh[:, hs], kl[:, hs])
                vh_sc[hh, rows, :] = vh[:, hs]

    q = q_ref[...]
    causal = _iota((blk, blk), 1) <= _iota((blk, blk), 0)
    lane = _iota((blk, V7X_LANES), 1)
    lane_f = lane.astype(F32)
    own0 = pl.multiple_of(i * blk, blk)
    def scores(hh, qcat, r0):
        return _mm(qcat, kcat_sc[hh, pl.ds(r0, blk), :], NT)

    def weighted(hh, p, r0):
        return _mm(p.astype(BF16), vh_sc[hh, pl.ds(r0, blk), :])

    sel_f, qs, state = [], [], []
    for hh in range(n_heads):
        qh = q[:, hh * HD_D:(hh + 1) * HD_D]
        gate = _dot3(qh, kmean_sc[:, hh * HD_D:(hh + 1) * HD_D], NT)
        sel, _ = _top_blocks(gate, lane_f, lane < i, MOBA_TOPK)
        sel_f.append(jnp.where(sel, 1.0, 0.0))
        q_hi, q_lo = _split2(qh * scale)
        qs.append(_cat_pieces(q_hi, q_lo, q_hi))
        s = jnp.where(causal, scores(hh, qs[hh], own0), -jnp.inf)
        m = jnp.max(s, axis=-1, keepdims=True)
        p = jnp.exp(s - m)
        state += [m, jnp.sum(p, axis=-1, keepdims=True), weighted(hh, p, own0)]

    def body(jj, carry):
        out = []
        for hh in range(n_heads):
            m, l, acc = carry[3 * hh:3 * hh + 3]
            ss, r0s = [], []
            for j in (2 * jj, 2 * jj + 1):
                r0 = pl.multiple_of(j * blk, blk)
                on = jnp.sum(jnp.where(lane == j, sel_f[hh], 0.0), axis=-1, keepdims=True) > 0.0
                ss.append(jnp.where(on, scores(hh, qs[hh], r0), -jnp.inf))
                r0s.append(r0)
            m2 = jnp.maximum(m, jnp.maximum(jnp.max(ss[0], axis=-1, keepdims=True),
                                            jnp.max(ss[1], axis=-1, keepdims=True)))
            a = jnp.exp(m - m2)
            ps = [jnp.exp(sj - m2) for sj in ss]
            l = a * l + (jnp.sum(ps[0], axis=-1, keepdims=True) + jnp.sum(ps[1], axis=-1, keepdims=True))
            acc = a * acc + (weighted(hh, ps[0], r0s[0]) + weighted(hh, ps[1], r0s[1]))
            out += [m2, l, acc]
        return tuple(out)

    state = lax.fori_loop(0, (i + 1) >> 1, body, tuple(state))
    for hh in range(n_heads):
        o_ref[:, hh * HD_D:(hh + 1) * HD_D] = state[3 * hh + 2] / state[3 * hh + 1]


def _moba_prompt(q, k, v, *, nb, t_len):
    blk = MOBA_BLOCK
    nq = t_len // blk
    assert t_len % blk == 0 and nq <= V7X_LANES
    qspec = pl.BlockSpec((blk, V7X_LANES), lambda b, hp, i: (b * nq + i, hp))
    kspec = pl.BlockSpec((t_len, V7X_LANES), lambda b, hp, i: (b, hp))
    return pl.pallas_call(
        _moba_prompt_kernel,
        grid=(nb, D_D // V7X_LANES, nq),
        in_specs=[qspec, kspec, kspec],
        out_specs=qspec,
        out_shape=jax.ShapeDtypeStruct((nb * t_len, D_D), F32),
        scratch_shapes=[pltpu.VMEM((V7X_LANES, V7X_LANES), F32),
                        pltpu.VMEM((V7X_LANES // HD_D, t_len, 4 * HD_D), BF16),
                        pltpu.VMEM((V7X_LANES // HD_D, t_len, HD_D), BF16)],
        compiler_params=_params(3),
        name="moba_prompt",
    )(q, k, v)


KM_PAGES = 16
PAGES_PER_BLOCK = MOBA_BLOCK // PAGE_SIZE


def _kmean_kernel(pt_ref, *refs):
    pages, o_ref = refs[:KM_PAGES], refs[KM_PAGES]
    c = pl.program_id(1)
    blocks_per_step = KM_PAGES // PAGES_PER_BLOCK

    @pl.when(c == 0)
    def _():
        o_ref[...] = jnp.zeros(o_ref.shape, F32)

    acc = o_ref[0]
    lane = _iota(acc.shape, 2)
    for blk in range(blocks_per_step):
        tot = pages[blk * PAGES_PER_BLOCK][0]
        for p in range(1, PAGES_PER_BLOCK):
            tot = tot + pages[blk * PAGES_PER_BLOCK + p][0]
        mean = jnp.sum(tot, axis=-1, keepdims=True) * (1.0 / MOBA_BLOCK)
        acc = jnp.where(lane == c * blocks_per_step + blk, mean, acc)
    o_ref[0] = acc


def _block_means(cache_kt, page_table):
    db, n_pages = page_table.shape
    n_blocks = n_pages // PAGES_PER_BLOCK
    assert n_pages % KM_PAGES == 0
    page_spec = lambda j: pl.BlockSpec((1, H_D, HD_D, PAGE_SIZE), lambda b, c, pt: (pt[b, c * KM_PAGES + j], 0, 0, 0))
    grid_spec = pltpu.PrefetchScalarGridSpec(
        num_scalar_prefetch=1,
        grid=(db, n_pages // KM_PAGES),
        in_specs=[page_spec(j) for j in range(KM_PAGES)],
        out_specs=pl.BlockSpec((1, H_D, HD_D, n_blocks), lambda b, c, pt: (b, 0, 0, 0)),
    )
    return pl.pallas_call(
        _kmean_kernel,
        grid_spec=grid_spec,
        out_shape=jax.ShapeDtypeStruct((db, H_D, HD_D, n_blocks), F32),
        compiler_params=_params(2),
        name="moba_block_means",
    )(page_table, *([cache_kt] * KM_PAGES))


def _select_kernel(q_ref, km_ref, o_ref):
    q = q_ref[...]
    n_blk = km_ref.shape[3]
    lane_f = _iota((q.shape[0], n_blk), 1).astype(F32)
    out_lane = _iota((q.shape[0], V7X_LANES), 1)
    for h in range(H_D):
        gate = _dot3(q[:, h * HD_D:(h + 1) * HD_D], km_ref[0, h])
        _, picks = _top_blocks(gate, lane_f, lane_f >= 0.0, MOBA_TOPK)
        res = jnp.zeros((q.shape[0], V7X_LANES), F32)
        for r, idx in enumerate(picks):
            res = jnp.where(out_lane == r, idx, res)
        o_ref[0, h] = res.astype(jnp.int32)


def _select_blocks(q, kmean_t):
    db = kmean_t.shape[0]
    return pl.pallas_call(
        _select_kernel,
        grid=(db,),
        in_specs=[pl.BlockSpec((SAMPLE_PAD, D_D), lambda b: (b, 0)),
                  pl.BlockSpec((1,) + kmean_t.shape[1:], lambda b: (b, 0, 0, 0))],
        out_specs=pl.BlockSpec((1, H_D, SAMPLE_PAD, V7X_LANES), lambda b: (b, 0, 0, 0)),
        out_shape=jax.ShapeDtypeStruct((db, H_D, SAMPLE_PAD, V7X_LANES), jnp.int32),
        compiler_params=_params(1),
        name="moba_select",
    )(q, kmean_t)


HEADS_PER_STEP = V7X_LANES // HD_D


def _attend_kernel(phys_ref, q_ref, kn_ref, vn_ref, ck_hbm, cv_hbm, o_ref, kbuf, vbuf, sem, *, n_q):
    b = pl.program_id(0)
    hp = pl.program_id(1)
    scale = HD_D ** -0.5

    def page_copy(hh, s, r, pg, which):
        h = hp * HEADS_PER_STEP + hh
        flat = (((b * H_D + h) * n_q + s) * MOBA_TOPK + r) * PAGES_PER_BLOCK + pg
        src, dst = (ck_hbm, kbuf) if which == 0 else (cv_hbm, vbuf)
        cols = pl.ds((r * PAGES_PER_BLOCK + pg) * PAGE_SIZE, PAGE_SIZE)
        return pltpu.make_async_copy(src.at[phys_ref[flat], h], dst.at[hh, s, :, cols], sem.at[which])

    combos = [(hh, s, r, pg, which) for hh in range(HEADS_PER_STEP) for s in range(n_q)
              for r in range(MOBA_TOPK) for pg in range(PAGES_PER_BLOCK) for which in range(2)]
    for c in combos:
        page_copy(*c).start()
    for c in combos:
        page_copy(*c).wait()

    q = q_ref[...]
    rows_n = q.shape[0]
    rowi = _iota((rows_n, rows_n), 0)
    coli = _iota((rows_n, rows_n), 1)
    own_ok = (coli <= rowi) & (coli < n_q)
    rsel = _iota((rows_n, HD_D), 0)
    for hh in range(HEADS_PER_STEP):
        hs = slice(hh * HD_D, (hh + 1) * HD_D)
        qh = q[:, hs]
        s_own = jnp.where(own_ok, _dot3(qh, kn_ref[...][:, hs], NT) * scale, -jnp.inf)
        m_own = jnp.max(s_own, axis=-1, keepdims=True)
        out = jnp.zeros((rows_n, HD_D), F32)
        for s in range(n_q):
            s_sel = _dot3(qh, kbuf[hh, s]) * scale
            m = jnp.maximum(m_own, jnp.max(s_sel, axis=-1, keepdims=True))
            p_sel = jnp.exp(s_sel - m)
            p_own = jnp.exp(s_own - m)
            l = jnp.sum(p_sel, axis=-1, keepdims=True) + jnp.sum(p_own, axis=-1, keepdims=True)
            o_s = (_dot3(p_sel, vbuf[hh, s], NT) + _dot3(p_own, vn_ref[...][:, hs])) / l
            out = jnp.where(rsel == s, o_s, out)
        o_ref[:, hs] = out


def _moba_sample_attend(phys, q, k_new, v_new, cache_kt, cache_vt, *, db, n_q):
    row = pl.BlockSpec((SAMPLE_PAD, V7X_LANES), lambda b, hp, ph: (b, hp))
    any_spec = pl.BlockSpec(memory_space=pl.ANY)
    buf = pltpu.VMEM((HEADS_PER_STEP, n_q, HD_D, MOBA_TOPK * MOBA_BLOCK), F32)
    grid_spec = pltpu.PrefetchScalarGridSpec(
        num_scalar_prefetch=1,
        grid=(db, D_D // V7X_LANES),
        in_specs=[row, row, row, any_spec, any_spec],
        out_specs=row,
        scratch_shapes=[buf, buf, pltpu.SemaphoreType.DMA((2,))],
    )
    return pl.pallas_call(
        functools.partial(_attend_kernel, n_q=n_q),
        grid_spec=grid_spec,
        out_shape=jax.ShapeDtypeStruct((db * SAMPLE_PAD, D_D), F32),
        compiler_params=_params(2),
        name="moba_sample_attend",
    )(phys, q, k_new, v_new, cache_kt, cache_vt)


def _outproj2_ln_kernel(a1_ref, a2_ref, x_ref, w1h_ref, w1l_ref, w2h_ref, w2l_ref, g_ref, b_ref, o_ref):
    y = _dot_hl(*_split2(a1_ref[...]), w1h_ref[...], w1l_ref[...])
    y = y + _dot_hl(*_split2(a2_ref[...]), w2h_ref[...], w2l_ref[...])
    o_ref[...] = _layer_norm(DN_ALPHA * x_ref[...] + y, g_ref[...], b_ref[...])


def _outproj2_ln(a1, a2, x, w1, w2, g, b, *, n_rows, tm):
    d = x.shape[1]
    assert n_rows % tm == 0
    row = lambda c: pl.BlockSpec((tm, c), lambda i: (i, 0))
    consts = [w1[0], w1[1], w2[0], w2[1], g, b]
    return pl.pallas_call(
        _outproj2_ln_kernel,
        grid=(n_rows // tm,),
        in_specs=[row(a1.shape[1]), row(a2.shape[1]), row(d)] + [_const_spec(c.shape) for c in consts],
        out_specs=row(d),
        out_shape=jax.ShapeDtypeStruct((n_rows, d), F32),
        compiler_params=_params(1),
        name="outproj2_ln",
    )(a1, a2, x, *consts)


SAMPLE_PAD = 16


def _pad_rows(x, t_pad):
    nb, t, d = x.shape
    return jnp.zeros((nb, t_pad, d), x.dtype).at[:, :t].set(x).reshape(nb * t_pad, d)


def kernel(x_prompt, x_sample, state_gla, state_ssm, state_conv, state_pool, cache_k, cache_v, page_table, ab_w_in, ab_w_gate2, ab_b_gate, ab_gla_norm, ab_conv_w, ab_conv_b, ab_dt_bias, ab_a_log, ab_d_skip, ab_ssm_norm, ab_w_out, cd_w_in, cd_w_pool, cd_pool_scale, cd_w_out, moe_w_coarse, moe_b_coarse, moe_w_fine, moe_b_fine, moe_w_gate_up, moe_w_down, ln_g, ln_b):
    bp, tp, d = x_prompt.shape
    bs, ts, _ = x_sample.shape
    (wih, wil), ab_prm, (woh, wol) = _ab_prepare(ab_w_in, ab_w_gate2, ab_b_gate, ab_gla_norm, ab_conv_w,
                                                 ab_conv_b, ab_dt_bias, ab_a_log, ab_d_skip, ab_ssm_norm, ab_w_out)
    n_p, n_s = bp * tp, bs * ts
    n_pages = page_table.shape[1]
    past = n_pages * PAGE_SIZE
    assert past % MOBA_BLOCK == 0 and past // MOBA_BLOCK >= MOBA_TOPK and ts <= SAMPLE_PAD
    zeros = lambda *s: jnp.zeros(s, F32)
    ln = lambda l, j: (ln_g[l, j].reshape(1, d), ln_b[l, j].reshape(1, d))
    unpad = lambda rows: rows.reshape(bs, SAMPLE_PAD, -1)[:, :ts]
    moe = lambda l, x: _moe_layer(x, moe_w_coarse[l], moe_b_coarse[l], moe_w_fine[l], moe_b_fine[l],
                                  moe_w_gate_up[l], moe_w_down[l], *ln(l, 1), tm=V7X_LANES,
                                  three_pass=l < DEPTH - 1)

    xp = x_prompt.reshape(n_p, d)
    xs = _pad_rows(x_sample, SAMPLE_PAD)
    proj_p = _proj(xp, wih, wil)
    oy_p, gla_p, ssm_p, conv_p = _scan_from_proj(
        proj_p, zeros(bp, H_A, DK_A, DV_A), zeros(bp, H_B, P_B, N_B), zeros(bp, CONV_W - 1, CONV_DIM), ab_prm,
        nb=bp, t_len=tp, tc=256, chunk=CHUNK, valid=256)
    xp = _outproj_ln(oy_p, xp, woh, wol, *ln(0, 0))
    proj_s = _proj(xs, wih, wil)
    oy_s, gla_s, ssm_s, conv_s = _scan_from_proj(
        proj_s, state_gla, state_ssm, state_conv, ab_prm,
        nb=bs, t_len=SAMPLE_PAD, tc=SAMPLE_PAD, chunk=SAMPLE_PAD, valid=ts)
    xs = _outproj_ln(oy_s, xs, woh, wol, *ln(0, 0))
    x_all = moe(0, jnp.concatenate([xp, unpad(xs).reshape(n_s, d)], axis=0))

    wch, wcl = _split_w(cd_w_in)
    wph, wpl = _split_w(cd_w_pool)
    pscale = cd_pool_scale.reshape(1, D_C)
    wo_c, wo_d = _split_w(cd_w_out[:D_C]), _split_w(cd_w_out[D_C:])
    cos_p, sin_p, perm = _rope_tables(jnp.arange(tp))
    u, q, k, v = _cdproj(x_all, wch, wcl, cos_p, sin_p, perm, n_rows=n_p, tm=256)
    pooled, pool_p = _pool(u, zeros(bp, POOL_BUF, D_C), wph, wpl, pscale, nb=bp, t_len=tp, tc=256, valid=256, start=0)
    att = _moba_prompt(q, k, v, nb=bp, t_len=tp)
    xp = _outproj2_ln(pooled, att, x_all, wo_c, wo_d, *ln(1, 0), n_rows=n_p, tm=256)
    k_p, v_p = k.reshape(bp, tp, H_D, HD_D), v.reshape(bp, tp, H_D, HD_D)

    xs = _pad_rows(x_all[n_p:].reshape(bs, ts, d), SAMPLE_PAD)
    cos_s, sin_s, _ = _rope_tables(past + jnp.arange(SAMPLE_PAD))
    n_rows_s = bs * SAMPLE_PAD
    u, q, k, v = _cdproj(xs, wch, wcl, jnp.tile(cos_s, (bs, 1)), jnp.tile(sin_s, (bs, 1)), perm,
                         n_rows=n_rows_s, tm=256)
    pooled, pool_s = _pool(u, state_pool, wph, wpl, pscale, nb=bs, t_len=SAMPLE_PAD, tc=SAMPLE_PAD, valid=ts,
                           start=past)
    cache_kt, cache_vt = jnp.transpose(cache_k, (0, 2, 3, 1)), jnp.transpose(cache_v, (0, 2, 3, 1))
    picks = _select_blocks(q, _block_means(cache_kt, page_table))[:, :, :ts, :MOBA_TOPK]
    logical = picks[..., None] * PAGES_PER_BLOCK + jnp.arange(PAGES_PER_BLOCK)
    phys = page_table[jnp.arange(bs)[:, None, None, None, None], logical].reshape(-1)
    att = _moba_sample_attend(phys, q, k, v, cache_kt, cache_vt, db=bs, n_q=ts)
    xs = _outproj2_ln(pooled, att, xs, wo_c, wo_d, *ln(1, 0), n_rows=n_rows_s, tm=256)
    k_s, v_s = unpad(k).reshape(bs, ts, H_D, HD_D), unpad(v).reshape(bs, ts, H_D, HD_D)
    x_all = moe(1, jnp.concatenate([xp, unpad(xs).reshape(n_s, d)], axis=0))

    return (x_all[:n_p].reshape(bp, tp, d), x_all[n_p:].reshape(bs, ts, d), gla_p, ssm_p, conv_p, pool_p, k_p, v_p,
            gla_s, ssm_s, conv_s, pool_s, k_s, v_s)
```

```python
import functools

import jax
import jax.numpy as jnp
import numpy as np
from jax import lax
from jax.experimental import pallas as pl
from jax.experimental.pallas import tpu as pltpu

F32 = jnp.float32
BF16 = jnp.bfloat16

D_MODEL = 1024
DEPTH = 2
PAGE_SIZE = 128
H_A, DK_A, DV_A = 4, 64, 128
GATE_RANK = 16
GATE_TAU = 16.0
CHUNK = 64
H_B, P_B, N_B, G_B = 8, 64, 64, 2
CONV_W = 4
D_B = H_B * P_B
CONV_DIM = D_B + 2 * G_B * N_B
POOL_WINDOWS = (2, 4, 8, 16)
POOL_GROUP = 128
D_C = len(POOL_WINDOWS) * POOL_GROUP
POOL_BUF = max(POOL_WINDOWS) - 1
H_D, HD_D = 8, 64
D_D = H_D * HD_D
MOBA_BLOCK = 256
MOBA_TOPK = 3
ROT_DIM = HD_D // 4
ROPE_THETA = 500000.0
N_GROUPS, EXP_PER_GROUP, TOP_FINE, D_EXPERT = 4, 8, 2, 256
N_EXPERTS = N_GROUPS * EXP_PER_GROUP
DN_ALPHA = (2 * DEPTH) ** 0.25
LN_EPS = 1e-5
RMS_EPS = 1e-6

V7X_LANES = 128
V7X_SUBLANES = 8
V7X_VMEM_LIMIT_BYTES = 56 * 1024 * 1024

NN = (((1,), (0,)), ((), ()))
NT = (((1,), (1,)), ((), ()))
TN = (((0,), (0,)), ((), ()))


def _params(n_axes):
    return pltpu.CompilerParams(dimension_semantics=("arbitrary",) * n_axes,
                                vmem_limit_bytes=V7X_VMEM_LIMIT_BYTES)


def _const_spec(shape):
    zeros = (0,) * len(shape)
    return pl.BlockSpec(shape, lambda *_: zeros, pipeline_mode=pl.Buffered(1))


def _split2(a):
    hi = a.astype(BF16)
    lo = (a - hi.astype(F32)).astype(BF16)
    return hi, lo


def _split3(a):
    hi = a.astype(BF16)
    r = a - hi.astype(F32)
    mid = r.astype(BF16)
    lo = (r - mid.astype(F32)).astype(BF16)
    return hi, mid, lo


def _mm(a, b, dims=NN):
    return lax.dot_general(a, b, dims, preferred_element_type=F32)


def _dot_hl(ah, al, bh, bl, dims=NN):
    return (_mm(ah, bl, dims) + _mm(al, bh, dims)) + _mm(ah, bh, dims)


def _dot3(a, b, dims=NN):
    ah, al = _split2(a)
    bh, bl = _split2(b)
    return _dot_hl(ah, al, bh, bl, dims)


def _dot3_cat(a, b, dims=NN):
    (ca,), (cb,) = dims[0]
    ah, al = _split2(a)
    bh, bl = _split2(b)
    return _mm(jnp.concatenate([ah, al, ah], axis=ca), jnp.concatenate([bh, bh, bl], axis=cb), dims)


def _dotx(a, e, dims=NN):
    h, m, l = _split3(a)
    return (_mm(l, e, dims) + _mm(m, e, dims)) + _mm(h, e, dims)


def _xdot(e, a, dims=NN):
    h, m, l = _split3(a)
    return (_mm(e, l, dims) + _mm(e, m, dims)) + _mm(e, h, dims)


def _silu(x):
    return x / (1.0 + jnp.exp(-x))


def _softplus(x):
    return jnp.maximum(x, 0.0) + jnp.log1p(jnp.exp(-jnp.abs(x)))


def _log_sigmoid(x):
    return jnp.minimum(x, 0.0) - jnp.log1p(jnp.exp(-jnp.abs(x)))


def _layer_norm(z, g, b):
    mu = jnp.mean(z, axis=-1, keepdims=True)
    zc = z - mu
    var = jnp.mean(zc * zc, axis=-1, keepdims=True)
    return zc * lax.rsqrt(var + LN_EPS) * g + b


def _iota(shape, dim):
    return lax.broadcasted_iota(jnp.int32, shape, dim)


def _proj_kernel(x_ref, wh_ref, wl_ref, o_ref, *, col_chunk):
    xh, xl = _split2(x_ref[...])
    m = o_ref.shape[1]
    for j in range(0, m, col_chunk):
        cs = slice(j, min(j + col_chunk, m))
        o_ref[:, cs] = _dot_hl(xh, xl, wh_ref[:, cs], wl_ref[:, cs])


def _proj(x, wh, wl, *, n_rows=None, tm=256, col_chunk=512):
    n = x.shape[0] if n_rows is None else n_rows
    k = x.shape[1]
    m = wh.shape[1]
    assert n % tm == 0
    return pl.pallas_call(
        functools.partial(_proj_kernel, col_chunk=col_chunk),
        grid=(n // tm,),
        in_specs=[pl.BlockSpec((tm, k), lambda i: (i, 0)), _const_spec((k, m)), _const_spec((k, m))],
        out_specs=pl.BlockSpec((tm, m), lambda i: (i, 0)),
        out_shape=jax.ShapeDtypeStruct((n, m), F32),
        compiler_params=_params(1),
        name="proj",
    )(x, wh, wl)


def _outproj_ln_kernel(a_ref, x_ref, wh_ref, wl_ref, g_ref, b_ref, o_ref):
    ah, al = _split2(a_ref[...])
    y = _dot_hl(ah, al, wh_ref[...], wl_ref[...])
    o_ref[...] = _layer_norm(DN_ALPHA * x_ref[...] + y, g_ref[...], b_ref[...])


def _outproj_ln(a, x, wh, wl, g, b, *, tm=256):
    n, k = a.shape
    d = wh.shape[1]
    assert n % tm == 0
    return pl.pallas_call(
        _outproj_ln_kernel,
        grid=(n // tm,),
        in_specs=[pl.BlockSpec((tm, k), lambda i: (i, 0)), pl.BlockSpec((tm, d), lambda i: (i, 0)),
                  _const_spec((k, d)), _const_spec((k, d)), _const_spec((1, d)), _const_spec((1, d))],
        out_specs=pl.BlockSpec((tm, d), lambda i: (i, 0)),
        out_shape=jax.ShapeDtypeStruct((n, d), F32),
        compiler_params=_params(1),
        name="outproj_ln",
    )(a, x, wh, wl, g, b)


SM_GLR0 = 0
SM_DT0 = GATE_RANK
CONV_HDR = V7X_SUBLANES


def _scan_kernel(qkvr_ref, z_ref, xbc_ref, sm_ref, s0_ref, h0_ref, c0_ref,
                 wg2h_ref, wg2l_ref, bg_ref, gnorm_ref, cw_ref, cb_ref, dtb_ref, alog_ref,
                 dskip_ref, snorm_ref,
                 oy_ref, s_out_ref, h_out_ref, c_out_ref,
                 st_sc, h_sc, ext_sc, o_sc, y_sc, *, tc, chunk, valid):
    t = pl.program_id(1)
    nt = pl.num_programs(1)
    n_chunks = tc // chunk
    shift = chunk.bit_length() - 1
    assert (1 << shift) == chunk and tc % chunk == 0

    @pl.when(t == 0)
    def _():
        st_sc[...] = s0_ref[0]
        h_sc[...] = h0_ref[0]
        ext_sc[0:CONV_HDR, :] = c0_ref[0]

    row = _iota((tc, tc), 0)
    col = _iota((tc, tc), 1)
    causal = ((row >> shift) == (col >> shift)) & (col <= row)
    l_tri = jnp.where(causal, 1.0, 0.0).astype(BF16)
    lane = _iota((tc, V7X_LANES), 1)
    rowv = _iota((tc, V7X_LANES), 0)
    dt_lane = (lane >= SM_DT0) & (lane < SM_DT0 + H_B)
    if valid < tc:
        row_ok = rowv < valid
        dt_lane = dt_lane & row_ok
    e8 = jnp.where((_iota((V7X_LANES, D_B), 1) >> 6) + SM_DT0 == _iota((V7X_LANES, D_B), 0), 1.0, 0.0).astype(BF16)

    sm = sm_ref[...]
    pre = _dot_hl(*_split2(sm), wg2h_ref[...], wg2l_ref[...])
    log_a = _log_sigmoid(pre + bg_ref[...]) / GATE_TAU
    if valid < tc:
        log_a = jnp.where(_iota((tc, H_A * DK_A), 0) < valid, log_a, 0.0)
    b_cum = _xdot(l_tri, log_a)
    dt_pad = jnp.where(dt_lane, _softplus(sm + dtb_ref[...]), 0.0)
    a_pad = -jnp.exp(alog_ref[...])
    la_pad = jnp.where(dt_lane, dt_pad * a_pad, 0.0)
    cum_pad = _xdot(l_tri, la_pad)
    dt_rep = _dotx(dt_pad, e8)
    cum_rep = _dotx(cum_pad, e8)

    ext_sc[CONV_HDR:CONV_HDR + tc, :] = xbc_ref[...]
    cw = cw_ref[...]
    acc = ext_sc[CONV_HDR - 3:CONV_HDR - 3 + tc, :] * cw[0:1, :]
    for i in range(1, CONV_W):
        acc = acc + ext_sc[CONV_HDR - 3 + i:CONV_HDR - 3 + i + tc, :] * cw[i:i + 1, :]
    xc = _silu(acc + cb_ref[...])

    @pl.when(t == nt - 1)
    def _():
        c_out_ref[0] = ext_sc[valid:valid + CONV_HDR, :]

    ext_sc[0:CONV_HDR, :] = ext_sc[tc:tc + CONV_HDR, :]

    xs = xc[:, 0:D_B]
    bm = xc[:, D_B:D_B + G_B * N_B]
    cm = xc[:, D_B + G_B * N_B:CONV_DIM]
    xdt = xs * dt_rep

    qkvr = qkvr_ref[...]
    q = qkvr[:, 0:H_A * DK_A] * (DK_A ** -0.5)
    k = qkvr[:, H_A * DK_A:2 * H_A * DK_A]
    v = qkvr[:, 2 * H_A * DK_A:2 * H_A * DK_A + H_A * DV_A]
    r = qkvr[:, 2 * H_A * DK_A + H_A * DV_A:]
    q_dec = q * jnp.exp(b_cum)
    k_dec = k * jnp.exp(-b_cum)
    for h in range(H_A):
        ks = slice(h * DK_A, (h + 1) * DK_A)
        vs = slice(h * DV_A, (h + 1) * DV_A)
        att = jnp.where(causal, _dot3_cat(q_dec[:, ks], k_dec[:, ks], NT), 0.0)
        o_sc[:, vs] = _dot3(att, v[:, vs])
    for c in range(n_chunks):
        rows = slice(c * chunk, (c + 1) * chunk)
        last = slice((c + 1) * chunk - 1, (c + 1) * chunk)
        b_last = b_cum[last, :]
        k_end = k[rows, :] * jnp.exp(b_last - b_cum[rows, :])
        dec = jnp.exp(b_last)
        for h in range(H_A):
            ks = slice(h * DK_A, (h + 1) * DK_A)
            vs = slice(h * DV_A, (h + 1) * DV_A)
            st = st_sc[h]
            o_sc[rows, vs] = o_sc[rows, vs] + _dot3_cat(q_dec[rows, ks], st, NT)
            st_sc[h] = st * dec[:, ks] + _dot3_cat(v[rows, vs], k_end[:, ks], TN)

    for g in range(G_B):
        gs = slice(g * N_B, (g + 1) * N_B)
        cb = _dot3_cat(cm[:, gs], bm[:, gs], NT)
        for hh in range(g * (H_B // G_B), (g + 1) * (H_B // G_B)):
            ps = slice(hh * P_B, (hh + 1) * P_B)
            ea = jnp.where(_iota((V7X_LANES, tc), 0) == SM_DT0 + hh, 1.0, 0.0).astype(BF16)
            m1 = jnp.where(_iota((tc, V7X_LANES), 1) == SM_DT0 + hh, 1.0, 0.0).astype(BF16)
            seg = _dotx(cum_pad, ea) - _xdot(m1, cum_pad, NT)
            w = cb * jnp.exp(jnp.where(causal, seg, -jnp.inf))
            y_sc[:, ps] = _dot3(w, xdt[:, ps])
    for c in range(n_chunks):
        rows = slice(c * chunk, (c + 1) * chunk)
        last = slice((c + 1) * chunk - 1, (c + 1) * chunk)
        cum_c = cum_rep[rows, :]
        cum_l = cum_rep[last, :]
        e_cum = jnp.exp(cum_c)
        x_dec = xdt[rows, :] * jnp.exp(cum_l - cum_c)
        e_last = jnp.exp(cum_l)
        for hh in range(H_B):
            g = hh // (H_B // G_B)
            gs = slice(g * N_B, (g + 1) * N_B)
            ps = slice(hh * P_B, (hh + 1) * P_B)
            hs = h_sc[hh]
            y_sc[rows, ps] = y_sc[rows, ps] + _dot3_cat(cm[rows, gs], hs, NT) * e_cum[:, ps]
            h_sc[hh] = hs * e_last[:, ps] + _dot3_cat(x_dec[:, ps], bm[rows, gs], TN)

    gn = gnorm_ref[...]
    for h in range(H_A):
        vs = slice(h * DV_A, (h + 1) * DV_A)
        oh = o_sc[:, vs]
        oh = oh * lax.rsqrt(jnp.mean(oh * oh, axis=-1, keepdims=True) + RMS_EPS) * gn
        oy_ref[:, vs] = oh * _silu(r[:, vs])
    y = (y_sc[...] + dskip_ref[...] * xs) * _silu(z_ref[...])
    y = y * lax.rsqrt(jnp.mean(y * y, axis=-1, keepdims=True) + RMS_EPS) * snorm_ref[...]
    oy_ref[:, H_A * DV_A:] = y

    @pl.when(t == nt - 1)
    def _():
        s_out_ref[0] = st_sc[...]
        h_out_ref[0] = h_sc[...]


def _scan_call(proj, s0t, h0, c0, prm, *, specs, nb, t_len, tc, chunk, valid):
    nt = t_len // tc
    assert t_len % tc == 0 and (valid == tc or nt == 1)
    rowspec = lambda c: pl.BlockSpec((tc, c), lambda b, t: (b * nt + t, 0))
    stspec = lambda shp: pl.BlockSpec((1,) + shp, lambda b, t: (b,) + (0,) * len(shp))
    names = ("wg2h", "wg2l", "bg", "gnorm", "cw", "cb", "dtb", "alog", "dskip", "snorm")
    consts = [prm[n] for n in names]
    return pl.pallas_call(
        functools.partial(_scan_kernel, tc=tc, chunk=chunk, valid=valid),
        grid=(nb, nt),
        in_specs=list(specs)
                 + [stspec((H_A, DV_A, DK_A)), stspec((H_B, P_B, N_B)), stspec((CONV_HDR, CONV_DIM))]
                 + [_const_spec(c.shape) for c in consts],
        out_specs=[rowspec(H_A * DV_A + D_B), stspec((H_A, DV_A, DK_A)), stspec((H_B, P_B, N_B)),
                   stspec((CONV_HDR, CONV_DIM))],
        out_shape=[jax.ShapeDtypeStruct((nb * t_len, H_A * DV_A + D_B), F32),
                   jax.ShapeDtypeStruct((nb, H_A, DV_A, DK_A), F32),
                   jax.ShapeDtypeStruct((nb, H_B, P_B, N_B), F32),
                   jax.ShapeDtypeStruct((nb, CONV_HDR, CONV_DIM), F32)],
        scratch_shapes=[pltpu.VMEM((H_A, DV_A, DK_A), F32), pltpu.VMEM((H_B, P_B, N_B), F32),
                        pltpu.VMEM((tc + CONV_HDR, CONV_DIM), F32),
                        pltpu.VMEM((tc, H_A * DV_A), F32), pltpu.VMEM((tc, D_B), F32)],
        compiler_params=_params(2),
        name="gla_ssd_scan",
    )(proj, proj, proj, proj, s0t, h0, c0, *consts)


AB_QKVR = 2 * H_A * DK_A + 2 * H_A * DV_A
AB_XBC0 = AB_QKVR
AB_SM0 = AB_XBC0 + CONV_DIM
AB_Z0 = 2560
AB_COLS = AB_Z0 + D_B


def _split_w(w):
    hi = w.astype(BF16)
    lo = (w - hi.astype(F32)).astype(BF16)
    return hi, lo


def _lane_pad(v, start, width=V7X_LANES):
    out = jnp.zeros((1, width), F32)
    return out.at[0, start:start + v.shape[0]].set(v.astype(F32))


def _ab_prepare(w_in, w_gate2, b_gate, gla_norm, conv_w, conv_b, dt_bias, a_log, d_skip, ssm_norm, w_out):
    o = np.cumsum([0, H_A * DK_A, H_A * DK_A, H_A * DV_A, H_A * DV_A, GATE_RANK, D_B, CONV_DIM, H_B])
    q0, gl0, z0, xbc0, dt0, end = o[0], o[4], o[5], o[6], o[7], o[8]
    k_dim = w_in.shape[0]
    w = jnp.concatenate([
        w_in[:, q0:gl0], w_in[:, xbc0:dt0], w_in[:, gl0:z0], w_in[:, dt0:end],
        jnp.zeros((k_dim, AB_Z0 - AB_SM0 - GATE_RANK - H_B), F32), w_in[:, z0:xbc0]], axis=1)
    assert w.shape[1] == AB_COLS
    wg2 = jnp.zeros((V7X_LANES, H_A * DK_A), F32).at[0:GATE_RANK].set(w_gate2)
    wg2h, wg2l = _split_w(wg2)
    prm = dict(
        wg2h=wg2h, wg2l=wg2l, bg=b_gate.reshape(1, -1), gnorm=gla_norm.reshape(1, -1),
        cw=conv_w, cb=conv_b.reshape(1, -1), dtb=_lane_pad(dt_bias, SM_DT0), alog=_lane_pad(a_log, SM_DT0),
        dskip=jnp.repeat(d_skip, P_B).reshape(1, -1), snorm=ssm_norm.reshape(1, -1))
    return _split_w(w), prm, _split_w(w_out)


def _scan_from_proj(proj, s0, h0, c0, prm, *, nb, t_len, tc, chunk, valid):
    nt = t_len // tc
    s0t = jnp.swapaxes(s0, -1, -2)
    c0p = jnp.zeros((nb, CONV_HDR, CONV_DIM), F32).at[:, CONV_HDR - (CONV_W - 1):].set(c0)
    colspec = lambda width, start: pl.BlockSpec((tc, width), lambda b, t: (b * nt + t, start // width))
    oy, s_new, h_new, c_new = _scan_call(
        proj, s0t, h0, c0p, prm,
        specs=[colspec(AB_QKVR, 0), colspec(D_B, AB_Z0), colspec(CONV_DIM, AB_XBC0), colspec(V7X_LANES, AB_SM0)],
        nb=nb, t_len=t_len, tc=tc, chunk=chunk, valid=valid)
    return oy, jnp.swapaxes(s_new, -1, -2), h_new, c_new[:, CONV_HDR - (CONV_W - 1):]


RT_FINE0 = N_GROUPS
RT_HALF = V7X_LANES // 2
MOE_TILE = 256


def _router_kernel(x_ref, wh_ref, wl_ref, b_ref, id_ref, rank_ref, gate_ref, cnt_ref, base_sc):
    @pl.when(pl.program_id(0) == 0)
    def _():
        base_sc[...] = jnp.zeros(base_sc.shape, F32)

    xh, xl = _split2(x_ref[...])
    logits = _dot_hl(xh, xl, wh_ref[...], wl_ref[...]) + b_ref[...]
    shp = logits.shape
    lane = _iota(shp, 1)
    lane_f = lane.astype(F32)
    big = float(V7X_LANES)
    ninf = -jnp.inf

    def first_max(mask):
        v = jnp.max(jnp.where(mask, logits, ninf), axis=-1, keepdims=True)
        i = jnp.min(jnp.where(mask & (logits == v), lane_f, big), axis=-1, keepdims=True)
        return v, i

    is_c = lane < N_GROUPS
    mc, grp = first_max(is_c)
    p_grp = 1.0 / jnp.sum(jnp.where(is_c, jnp.exp(logits - mc), 0.0), axis=-1, keepdims=True)
    fine = (lane >= RT_FINE0) & (lane < RT_FINE0 + N_EXPERTS)
    cand = fine & (((lane - RT_FINE0) >> 3).astype(F32) == grp)
    v1, i1 = first_max(cand)
    v2, i2 = first_max(cand & (lane_f != i1))
    e = jnp.exp(v2 - v1)
    w1 = p_grp / (1.0 + e)
    w2 = p_grp * (e / (1.0 + e))
    first = lane < RT_HALF
    id_ref[...] = (jnp.where(first, i1, i2) - float(RT_FINE0)).astype(jnp.int32)
    gate_ref[...] = jnp.where(first, w1, w2)

    tm = shp[0]
    oh1 = jnp.where(lane_f == i1, 1.0, 0.0)
    oh2 = jnp.where(lane_f == i2, 1.0, 0.0)
    oh = oh1 + oh2
    earlier = jnp.where(_iota((tm, tm), 1) < _iota((tm, tm), 0), 1.0, 0.0).astype(BF16)
    before = _mm(earlier, oh.astype(BF16)) + base_sc[...]
    r1 = jnp.sum(oh1 * before, axis=-1, keepdims=True)
    r2 = jnp.sum(oh2 * before, axis=-1, keepdims=True)
    rank_ref[...] = jnp.where(first, r1, r2).astype(jnp.int32)
    base_sc[...] = base_sc[...] + jnp.sum(oh, axis=0, keepdims=True)
    cnt_ref[...] = base_sc[...]


def _router(x, wh, wl, b, *, tm):
    n, d = x.shape
    assert n % tm == 0
    row = pl.BlockSpec((tm, V7X_LANES), lambda i: (i, 0))
    return pl.pallas_call(
        _router_kernel,
        grid=(n // tm,),
        in_specs=[pl.BlockSpec((tm, d), lambda i: (i, 0)), _const_spec(wh.shape), _const_spec(wl.shape),
                  _const_spec(b.shape)],
        out_specs=[row, row, row, pl.BlockSpec((1, V7X_LANES), lambda i: (0, 0))],
        out_shape=[jax.ShapeDtypeStruct((n, V7X_LANES), jnp.int32), jax.ShapeDtypeStruct((n, V7X_LANES), jnp.int32),
                   jax.ShapeDtypeStruct((n, V7X_LANES), F32), jax.ShapeDtypeStruct((1, V7X_LANES), F32)],
        scratch_shapes=[pltpu.VMEM((1, V7X_LANES), F32)],
        compiler_params=_params(1),
        name="moe_router",
    )(x, wh, wl, b)


DMA_UNROLL = 8


def _pos_copy(pos_hbm, pos_sm, sem_p, step, slot):
    return pltpu.make_async_copy(pos_hbm.at[pl.ds(step, 1)], pos_sm.at[slot], sem_p.at[slot])


def _dispatch_kernel(zf_ref, x_ref, pos_hbm, xs_hbm, pos_sm, zbuf, sem_p, sem_z, sem_s):
    i = pl.program_id(0)
    n = pl.num_programs(0)
    tm = x_ref.shape[0]
    tile = zbuf.shape[0]
    n_tiles = xs_hbm.shape[0] // tile
    slot = lax.rem(i, 2)

    def zero_copy(j):
        return pltpu.make_async_copy(zbuf, xs_hbm.at[pl.ds(pl.multiple_of(j * tile, tile), tile)], sem_z)

    @pl.when(i == 0)
    def _():
        _pos_copy(pos_hbm, pos_sm, sem_p, 0, 0).start()
        zbuf[...] = jnp.zeros(zbuf.shape, F32)

        def z_start(j, c):
            @pl.when(zf_ref[j] > 0)
            def _():
                zero_copy(j).start()
            return c

        def z_wait(j, c):
            @pl.when(zf_ref[j] > 0)
            def _():
                zero_copy(j).wait()
            return c

        lax.fori_loop(0, n_tiles, z_start, 0)
        lax.fori_loop(0, n_tiles, z_wait, 0)

    @pl.when(i + 1 < n)
    def _():
        _pos_copy(pos_hbm, pos_sm, sem_p, i + 1, 1 - slot).start()

    _pos_copy(pos_hbm, pos_sm, sem_p, i, slot).wait()

    def issue(r, c):
        for k in range(TOP_FINE):
            dst = pos_sm[slot, 0, k * tm + r]
            pltpu.make_async_copy(x_ref.at[pl.ds(r, 1)], xs_hbm.at[pl.ds(dst, 1)], sem_s).start()
        return c

    lax.fori_loop(0, tm, issue, 0, unroll=DMA_UNROLL)
    for _ in range(TOP_FINE):
        pltpu.make_async_copy(x_ref, xs_hbm.at[pl.ds(0, tm)], sem_s).wait()


def _dispatch(x, pos_tab, zero_flag, *, n_tiles, tm):
    n, d = x.shape
    any_spec = pl.BlockSpec(memory_space=pl.ANY)
    grid_spec = pltpu.PrefetchScalarGridSpec(
        num_scalar_prefetch=1,
        grid=(n // tm,),
        in_specs=[pl.BlockSpec((tm, d), lambda i, zf: (i, 0)), any_spec],
        out_specs=any_spec,
        scratch_shapes=[pltpu.SMEM((2, 1, TOP_FINE * tm), jnp.int32), pltpu.VMEM((MOE_TILE, d), F32),
                        pltpu.SemaphoreType.DMA((2,)), pltpu.SemaphoreType.DMA(()), pltpu.SemaphoreType.DMA(())],
    )
    return pl.pallas_call(
        _dispatch_kernel,
        grid_spec=grid_spec,
        out_shape=jax.ShapeDtypeStruct((n_tiles * MOE_TILE, d), F32),
        compiler_params=_params(1),
        name="moe_dispatch",
    )(zero_flag, x, pos_tab)


def _expert_kernel(te_ref, xs_ref, wgu_ref, wd_ref, ys_ref, *, three_pass):
    if three_pass:
        xh, xl = _split2(xs_ref[...])
        wh, wl = _split2(wgu_ref[0])
        hgu = _dot_hl(xh, xl, wh, wl)
    else:
        hgu = _mm(xs_ref[...].astype(BF16), wgu_ref[0].astype(BF16))
    act = _silu(hgu[:, :D_EXPERT]) * hgu[:, D_EXPERT:]
    if three_pass:
        ys_ref[...] = _dot3(act, wd_ref[0])
    else:
        ys_ref[...] = _mm(act.astype(BF16), wd_ref[0].astype(BF16))


def _experts(xs, w_gate_up, w_down, tile_expert, *, three_pass):
    p, d = xs.shape
    f2 = w_gate_up.shape[-1]
    row = pl.BlockSpec((MOE_TILE, d), lambda i, te: (i, 0))
    grid_spec = pltpu.PrefetchScalarGridSpec(
        num_scalar_prefetch=1,
        grid=(p // MOE_TILE,),
        in_specs=[row, pl.BlockSpec((1, d, f2), lambda i, te: (te[i], 0, 0)),
                  pl.BlockSpec((1, f2 // 2, d), lambda i, te: (te[i], 0, 0))],
        out_specs=row,
    )
    return pl.pallas_call(
        functools.partial(_expert_kernel, three_pass=three_pass),
        grid_spec=grid_spec,
        out_shape=jax.ShapeDtypeStruct((p, d), F32),
        compiler_params=_params(1),
        name="moe_experts",
    )(tile_expert, xs, w_gate_up, w_down)


def _combine_ln_kernel(x_ref, gate_ref, g_ref, b_ref, pos_hbm, ys_hbm, o_ref, pos_sm, ybuf, sem_p, sem_y):
    i = pl.program_id(0)
    n = pl.num_programs(0)
    tm = x_ref.shape[0]
    slot = lax.rem(i, 2)

    def issue_gathers(s):
        def issue(r, c):
            for k in range(TOP_FINE):
                src = pos_sm[s, 0, k * tm + r]
                pltpu.make_async_copy(ys_hbm.at[pl.ds(src, 1)], ybuf.at[s, pl.ds(k * tm + r, 1)],
                                      sem_y.at[s]).start()
            return c

        lax.fori_loop(0, tm, issue, 0, unroll=DMA_UNROLL)

    @pl.when(i == 0)
    def _():
        _pos_copy(pos_hbm, pos_sm, sem_p, 0, 0).start()
        _pos_copy(pos_hbm, pos_sm, sem_p, 0, 0).wait()
        issue_gathers(0)

        @pl.when(n > 1)
        def _():
            _pos_copy(pos_hbm, pos_sm, sem_p, 1, 1).start()

    @pl.when(i + 1 < n)
    def _():
        _pos_copy(pos_hbm, pos_sm, sem_p, i + 1, 1 - slot).wait()
        issue_gathers(1 - slot)

    @pl.when(i + 2 < n)
    def _():
        _pos_copy(pos_hbm, pos_sm, sem_p, i + 2, slot).start()

    pltpu.make_async_copy(ys_hbm.at[pl.ds(0, TOP_FINE * tm)], ybuf.at[slot], sem_y.at[slot]).wait()
    gate = gate_ref[...]
    y = ybuf[slot]
    moe = gate[:, 0:1] * y[0:tm] + gate[:, RT_HALF:RT_HALF + 1] * y[tm:2 * tm]
    o_ref[...] = _layer_norm(DN_ALPHA * x_ref[...] + moe, g_ref[...], b_ref[...])


def _combine_ln(x, ys, pos_tab, gate, g, b, *, tm):
    n, d = x.shape
    assert n % tm == 0 and TOP_FINE == 2
    row = pl.BlockSpec((tm, d), lambda i: (i, 0))
    any_spec = pl.BlockSpec(memory_space=pl.ANY)
    return pl.pallas_call(
        _combine_ln_kernel,
        grid=(n // tm,),
        in_specs=[row, pl.BlockSpec((tm, V7X_LANES), lambda i: (i, 0)), _const_spec((1, d)), _const_spec((1, d)),
                  any_spec, any_spec],
        out_specs=row,
        out_shape=jax.ShapeDtypeStruct((n, d), F32),
        scratch_shapes=[pltpu.SMEM((2, 1, TOP_FINE * tm), jnp.int32), pltpu.VMEM((2, TOP_FINE * tm, d), F32),
                        pltpu.SemaphoreType.DMA((2,)), pltpu.SemaphoreType.DMA((2,))],
        compiler_params=_params(1),
        name="moe_combine_ln",
    )(x, gate, g, b, pos_tab, ys)


def _moe_plan(ids, rank, cnt, n, tm):
    counts = cnt[0, RT_FINE0:RT_FINE0 + N_EXPERTS].astype(jnp.int32)
    tiles_per = (counts + MOE_TILE - 1) // MOE_TILE
    e_idx = jnp.arange(N_EXPERTS)
    tile_end = tiles_per @ (e_idx[:, None] <= e_idx[None, :]).astype(jnp.int32)
    tile_beg = tile_end - tiles_per
    n_tiles = -(-TOP_FINE * n // MOE_TILE) + N_EXPERTS
    ti = jnp.arange(n_tiles)
    te = jnp.minimum(jnp.sum((tile_end[None, :] <= ti[:, None]).astype(jnp.int32), axis=1), N_EXPERTS - 1)
    is_last = jnp.any((ti[:, None] == tile_end[None, :] - 1) & (tiles_per[None, :] > 0), axis=1)
    zero_flag = (is_last | (ti >= tile_end[-1])).astype(jnp.int32)
    pos = [tile_beg[ids[:, c]] * MOE_TILE + rank[:, c] for c in (0, RT_HALF)]
    pos_tab = jnp.concatenate([p.reshape(n // tm, tm) for p in pos], axis=1).astype(jnp.int32)
    return te.astype(jnp.int32), zero_flag, pos_tab, n_tiles


def _moe_prepare(w_coarse, b_coarse, w_fine, b_fine):
    d = w_coarse.shape[0]
    wf = jnp.transpose(w_fine, (1, 0, 2)).reshape(d, N_EXPERTS)
    w = jnp.zeros((d, V7X_LANES), F32).at[:, :N_GROUPS].set(w_coarse).at[:, RT_FINE0:RT_FINE0 + N_EXPERTS].set(wf)
    b = jnp.zeros((1, V7X_LANES), F32).at[0, :N_GROUPS].set(b_coarse)
    b = b.at[0, RT_FINE0:RT_FINE0 + N_EXPERTS].set(b_fine.reshape(-1))
    return _split_w(w), b


def _moe_layer(x, w_coarse, b_coarse, w_fine, b_fine, w_gate_up, w_down, g, b, *, tm, three_pass):
    n, d = x.shape
    (wrh, wrl), br = _moe_prepare(w_coarse, b_coarse, w_fine, b_fine)
    ids, rank, gate, cnt = _router(x, wrh, wrl, br, tm=tm)
    te, zero_flag, pos_tab, n_tiles = _moe_plan(ids, rank, cnt, n, tm)
    xs = _dispatch(x, pos_tab, zero_flag, n_tiles=n_tiles, tm=tm)
    ys = _experts(xs, w_gate_up.reshape(N_EXPERTS, d, 2 * D_EXPERT), w_down.reshape(N_EXPERTS, D_EXPERT, d), te,
                  three_pass=three_pass)
    return _combine_ln(x, ys, pos_tab, gate, g, b, tm=tm)


def _cdproj_kernel(x_ref, wh_ref, wl_ref, cos_ref, sin_ref, perm_ref, u_ref, q_ref, k_ref, v_ref):
    xh, xl = _split2(x_ref[...])

    def col(j):
        cs = slice(j * D_C, (j + 1) * D_C)
        return _dot_hl(xh, xl, wh_ref[:, cs], wl_ref[:, cs])

    u_ref[...] = col(0)
    for j, ref in ((1, q_ref), (2, k_ref)):
        t = col(j)
        ref[...] = t * cos_ref[...] + _dotx(t, perm_ref[...]) * sin_ref[...]
    v_ref[...] = col(3)


def _cdproj(x, wh, wl, cos_t, sin_t, perm, *, n_rows, tm):
    k = x.shape[1]
    assert D_C == D_D and n_rows % tm == 0 and cos_t.shape[0] % tm == 0
    nt = cos_t.shape[0] // tm
    row = pl.BlockSpec((tm, D_D), lambda i: (i, 0))
    tab = pl.BlockSpec((tm, D_D), lambda i: (i % nt, 0))
    return pl.pallas_call(
        _cdproj_kernel,
        grid=(n_rows // tm,),
        in_specs=[pl.BlockSpec((tm, k), lambda i: (i, 0)), _const_spec(wh.shape), _const_spec(wl.shape),
                  tab, tab, _const_spec(perm.shape)],
        out_specs=[row] * 4,
        out_shape=[jax.ShapeDtypeStruct((n_rows, D_D), F32)] * 4,
        compiler_params=_params(1),
        name="cd_proj_rope",
    )(x, wh, wl, cos_t, sin_t, perm)


def _rope_tables(pos):
    half = ROT_DIM // 2
    inv = ROPE_THETA ** (-jnp.arange(half, dtype=F32) / half)
    ang = pos.astype(F32)[:, None] * inv
    cos, sin = jnp.cos(ang), jnp.sin(ang)
    n = pos.shape[0]
    rest = HD_D - ROT_DIM
    cos_h = jnp.concatenate([cos, cos, jnp.ones((n, rest), F32)], axis=-1)
    sin_h = jnp.concatenate([-sin, sin, jnp.zeros((n, rest), F32)], axis=-1)
    perm = np.zeros((D_D, D_D), np.float32)
    for dst in range(D_D):
        j = dst % HD_D
        if j < half:
            perm[dst + half, dst] = 1.0
        elif j < ROT_DIM:
            perm[dst - half, dst] = 1.0
    return jnp.tile(cos_h, (1, H_D)), jnp.tile(sin_h, (1, H_D)), jnp.asarray(perm, BF16)


POOL_HDR = 16


def _pool_kernel(u_ref, b0_ref, wph_ref, wpl_ref, sc_ref, o_ref, bo_ref, ext_sc, *, tc, valid, start):
    t = pl.program_id(1)
    nt = pl.num_programs(1)

    @pl.when(t == 0)
    def _():
        ext_sc[0:POOL_HDR, :] = b0_ref[0]

    ext_sc[POOL_HDR:POOL_HDR + tc, :] = u_ref[...]
    u = u_ref[...]
    rowi = _iota((tc, POOL_HDR + tc), 0) + POOL_HDR
    colj = _iota((tc, POOL_HDR + tc), 1)
    pos = (start + t * tc + _iota((tc, POOL_GROUP), 0)).astype(F32)
    for gi, w in enumerate(POOL_WINDOWS):
        ls = slice(gi * POOL_GROUP, (gi + 1) * POOL_GROUP)
        band = jnp.where((colj <= rowi) & (colj > rowi - w), 1.0, 0.0).astype(BF16)
        win = _xdot(band, ext_sc[:, ls])
        cnt = jnp.minimum(float(w), pos + 1.0)
        dh, dl = _split2(win / cnt - u[:, ls])
        o_ref[:, ls] = _dot_hl(dh, dl, wph_ref[gi], wpl_ref[gi]) * sc_ref[:, ls]

    @pl.when(t == nt - 1)
    def _():
        bo_ref[0] = ext_sc[valid:valid + POOL_HDR, :]

    ext_sc[0:POOL_HDR, :] = ext_sc[tc:tc + POOL_HDR, :]


def _pool(u, buf, wph, wpl, scale, *, nb, t_len, tc, valid, start):
    nt = t_len // tc
    assert t_len % tc == 0 and (valid == tc or nt == 1)
    b0 = jnp.zeros((nb, POOL_HDR, D_C), F32).at[:, POOL_HDR - POOL_BUF:].set(buf)
    row = pl.BlockSpec((tc, D_C), lambda b, t: (b * nt + t, 0))
    st = pl.BlockSpec((1, POOL_HDR, D_C), lambda b, t: (b, 0, 0))
    pooled, bnew = pl.pallas_call(
        functools.partial(_pool_kernel, tc=tc, valid=valid, start=start),
        grid=(nb, nt),
        in_specs=[row, st, _const_spec(wph.shape), _const_spec(wpl.shape), _const_spec(scale.shape)],
        out_specs=[row, st],
        out_shape=[jax.ShapeDtypeStruct((nb * t_len, D_C), F32), jax.ShapeDtypeStruct((nb, POOL_HDR, D_C), F32)],
        scratch_shapes=[pltpu.VMEM((POOL_HDR + tc, D_C), F32)],
        compiler_params=_params(2),
        name="pool_mix",
    )(u, b0, wph, wpl, scale)
    return pooled, bnew[:, POOL_HDR - POOL_BUF:]


def _top_blocks(gate, lane_f, valid, n_pick):
    sel = jnp.zeros(gate.shape, jnp.bool_)
    g = jnp.where(valid, gate, -jnp.inf)
    picks = []
    for _ in range(n_pick):
        v = jnp.max(g, axis=-1, keepdims=True)
        idx = jnp.min(jnp.where(g == v, lane_f, float(V7X_LANES)), axis=-1, keepdims=True)
        hit = (lane_f == idx) & valid
        sel = sel | hit
        g = jnp.where(hit, -jnp.inf, g)
        picks.append(idx)
    return sel, picks


def _cat_pieces(a, b, c):
    return jnp.concatenate([a, b, c, jnp.zeros_like(a)], axis=1)


def _moba_prompt_kernel(q_ref, k_ref, v_ref, o_ref, kmean_sc, kcat_sc, vh_sc):
    i = pl.program_id(2)
    blk = MOBA_BLOCK
    t_len = k_ref.shape[0]
    n_heads = V7X_LANES // HD_D
    scale = HD_D ** -0.5

    @pl.when(i == 0)
    def _():
        ind = jnp.where((_iota((V7X_LANES, t_len), 1) >> 8) == _iota((V7X_LANES, t_len), 0), 1.0, 0.0)
        kmean_sc[...] = _xdot(ind.astype(BF16), k_ref[...]) * (1.0 / blk)
        for n in range(t_len // blk):
            rows = slice(n * blk, (n + 1) * blk)
            kh, kl = _split2(k_ref[rows, :])
            vh = v_ref[rows, :].astype(BF16)
            for hh in range(n_heads):
                hs = slice(hh * HD_D, (hh + 1) * HD_D)
                kcat_sc[hh, rows, :] = _cat_pieces(kh[:, hs], kh[:, hs], kl[:, hs])
                vh_sc[hh, rows, :] = vh[:, hs]

    q = q_ref[...]
    causal = _iota((blk, blk), 1) <= _iota((blk, blk), 0)
    lane = _iota((blk, V7X_LANES), 1)
    lane_f = lane.astype(F32)
    own0 = pl.multiple_of(i * blk, blk)
    def scores(hh, qcat, r0):
        return _mm(qcat, kcat_sc[hh, pl.ds(r0, blk), :], NT)

    def weighted(hh, p, r0):
        return _mm(p.astype(BF16), vh_sc[hh, pl.ds(r0, blk), :])

    sel_f, qs, state = [], [], []
    for hh in range(n_heads):
        qh = q[:, hh * HD_D:(hh + 1) * HD_D]
        gate = _dot3(qh, kmean_sc[:, hh * HD_D:(hh + 1) * HD_D], NT)
        sel, _ = _top_blocks(gate, lane_f, lane < i, MOBA_TOPK)
        sel_f.append(jnp.where(sel, 1.0, 0.0))
        q_hi, q_lo = _split2(qh * scale)
        qs.append(_cat_pieces(q_hi, q_lo, q_hi))
        s = jnp.where(causal, scores(hh, qs[hh], own0), -jnp.inf)
        m = jnp.max(s, axis=-1, keepdims=True)
        p = jnp.exp(s - m)
        state += [m, jnp.sum(p, axis=-1, keepdims=True), weighted(hh, p, own0)]

    def body(jj, carry):
        out = []
        for hh in range(n_heads):
            m, l, acc = carry[3 * hh:3 * hh + 3]
            ss, r0s = [], []
            for j in (2 * jj, 2 * jj + 1):
                r0 = pl.multiple_of(j * blk, blk)
                on = jnp.sum(jnp.where(lane == j, sel_f[hh], 0.0), axis=-1, keepdims=True) > 0.0
                ss.append(jnp.where(on, scores(hh, qs[hh], r0), -jnp.inf))
                r0s.append(r0)
            m2 = jnp.maximum(m, jnp.maximum(jnp.max(ss[0], axis=-1, keepdims=True),
                                            jnp.max(ss[1], axis=-1, keepdims=True)))
            a = jnp.exp(m - m2)
            ps = [jnp.exp(sj - m2) for sj in ss]
            l = a * l + (jnp.sum(ps[0], axis=-1, keepdims=True) + jnp.sum(ps[1], axis=-1, keepdims=True))
            acc = a * acc + (weighted(hh, ps[0], r0s[0]) + weighted(hh, ps[1], r0s[1]))
            out += [m2, l, acc]
        return tuple(out)

    state = lax.fori_loop(0, (i + 1) >> 1, body, tuple(state))
    for hh in range(n_heads):
        o_ref[:, hh * HD_D:(hh + 1) * HD_D] = state[3 * hh + 2] / state[3 * hh + 1]


def _moba_prompt(q, k, v, *, nb, t_len):
    blk = MOBA_BLOCK
    nq = t_len // blk
    assert t_len % blk == 0 and nq <= V7X_LANES
    qspec = pl.BlockSpec((blk, V7X_LANES), lambda b, hp, i: (b * nq + i, hp))
    kspec = pl.BlockSpec((t_len, V7X_LANES), lambda b, hp, i: (b, hp))
    return pl.pallas_call(
        _moba_prompt_kernel,
        grid=(nb, D_D // V7X_LANES, nq),
        in_specs=[qspec, kspec, kspec],
        out_specs=qspec,
        out_shape=jax.ShapeDtypeStruct((nb * t_len, D_D), F32),
        scratch_shapes=[pltpu.VMEM((V7X_LANES, V7X_LANES), F32),
                        pltpu.VMEM((V7X_LANES // HD_D, t_len, 4 * HD_D), BF16),
                        pltpu.VMEM((V7X_LANES // HD_D, t_len, HD_D), BF16)],
        compiler_params=_params(3),
        name="moba_prompt",
    )(q, k, v)


KM_PAGES = 16
PAGES_PER_BLOCK = MOBA_BLOCK // PAGE_SIZE


def _kmean_kernel(pt_ref, *refs):
    pages, o_ref = refs[:KM_PAGES], refs[KM_PAGES]
    c = pl.program_id(1)
    blocks_per_step = KM_PAGES // PAGES_PER_BLOCK

    @pl.when(c == 0)
    def _():
        o_ref[...] = jnp.zeros(o_ref.shape, F32)

    acc = o_ref[0]
    lane = _iota(acc.shape, 2)
    for blk in range(blocks_per_step):
        tot = pages[blk * PAGES_PER_BLOCK][0]
        for p in range(1, PAGES_PER_BLOCK):
            tot = tot + pages[blk * PAGES_PER_BLOCK + p][0]
        mean = jnp.sum(tot, axis=-1, keepdims=True) * (1.0 / MOBA_BLOCK)
        acc = jnp.where(lane == c * blocks_per_step + blk, mean, acc)
    o_ref[0] = acc


def _block_means(cache_kt, page_table):
    db, n_pages = page_table.shape
    n_blocks = n_pages // PAGES_PER_BLOCK
    assert n_pages % KM_PAGES == 0
    page_spec = lambda j: pl.BlockSpec((1, H_D, HD_D, PAGE_SIZE), lambda b, c, pt: (pt[b, c * KM_PAGES + j], 0, 0, 0))
    grid_spec = pltpu.PrefetchScalarGridSpec(
        num_scalar_prefetch=1,
        grid=(db, n_pages // KM_PAGES),
        in_specs=[page_spec(j) for j in range(KM_PAGES)],
        out_specs=pl.BlockSpec((1, H_D, HD_D, n_blocks), lambda b, c, pt: (b, 0, 0, 0)),
    )
    return pl.pallas_call(
        _kmean_kernel,
        grid_spec=grid_spec,
        out_shape=jax.ShapeDtypeStruct((db, H_D, HD_D, n_blocks), F32),
        compiler_params=_params(2),
        name="moba_block_means",
    )(page_table, *([cache_kt] * KM_PAGES))


def _select_kernel(q_ref, km_ref, o_ref):
    q = q_ref[...]
    n_blk = km_ref.shape[3]
    lane_f = _iota((q.shape[0], n_blk), 1).astype(F32)
    out_lane = _iota((q.shape[0], V7X_LANES), 1)
    for h in range(H_D):
        gate = _dot3(q[:, h * HD_D:(h + 1) * HD_D], km_ref[0, h])
        _, picks = _top_blocks(gate, lane_f, lane_f >= 0.0, MOBA_TOPK)
        res = jnp.zeros((q.shape[0], V7X_LANES), F32)
        for r, idx in enumerate(picks):
            res = jnp.where(out_lane == r, idx, res)
        o_ref[0, h] = res.astype(jnp.int32)


def _select_blocks(q, kmean_t):
    db = kmean_t.shape[0]
    return pl.pallas_call(
        _select_kernel,
        grid=(db,),
        in_specs=[pl.BlockSpec((SAMPLE_PAD, D_D), lambda b: (b, 0)),
                  pl.BlockSpec((1,) + kmean_t.shape[1:], lambda b: (b, 0, 0, 0))],
        out_specs=pl.BlockSpec((1, H_D, SAMPLE_PAD, V7X_LANES), lambda b: (b, 0, 0, 0)),
        out_shape=jax.ShapeDtypeStruct((db, H_D, SAMPLE_PAD, V7X_LANES), jnp.int32),
        compiler_params=_params(1),
        name="moba_select",
    )(q, kmean_t)


HEADS_PER_STEP = V7X_LANES // HD_D


def _attend_kernel(phys_ref, q_ref, kn_ref, vn_ref, ck_hbm, cv_hbm, o_ref, kbuf, vbuf, sem, *, n_q):
    b = pl.program_id(0)
    hp = pl.program_id(1)
    n_hp = pl.num_programs(1)
    step = b * n_hp + hp
    slot = lax.rem(step, 2)
    scale = HD_D ** -0.5

    def fetch(fb, fhp, fslot):
        for hh in range(HEADS_PER_STEP):
            h = fhp * HEADS_PER_STEP + hh
            for s in range(n_q):
                for r in range(MOBA_TOPK):
                    for pg in range(PAGES_PER_BLOCK):
                        flat = (((fb * H_D + h) * n_q + s) * MOBA_TOPK + r) * PAGES_PER_BLOCK + pg
                        cols = pl.ds((r * PAGES_PER_BLOCK + pg) * PAGE_SIZE, PAGE_SIZE)
                        page = phys_ref[flat]
                        for which, (src, dst) in enumerate(((ck_hbm, kbuf), (cv_hbm, vbuf))):
                            pltpu.make_async_copy(src.at[page, h], dst.at[fslot, hh, s, :, cols],
                                                  sem.at[fslot, which]).start()

    @pl.when(step == 0)
    def _():
        fetch(b, hp, slot)

    @pl.when(step + 1 < pl.num_programs(0) * n_hp)
    def _():
        wrap = hp + 1 == n_hp
        fetch(jnp.where(wrap, b + 1, b), jnp.where(wrap, 0, hp + 1), 1 - slot)

    pltpu.make_async_copy(vbuf.at[1 - slot], kbuf.at[slot], sem.at[slot, 0]).wait()
    pltpu.make_async_copy(kbuf.at[1 - slot], vbuf.at[slot], sem.at[slot, 1]).wait()

    q = q_ref[...]
    rows_n = q.shape[0]
    rowi = _iota((rows_n, rows_n), 0)
    coli = _iota((rows_n, rows_n), 1)
    own_ok = (coli <= rowi) & (coli < n_q)
    rsel = _iota((rows_n, HD_D), 0)
    for hh in range(HEADS_PER_STEP):
        hs = slice(hh * HD_D, (hh + 1) * HD_D)
        qh = q[:, hs]
        s_own = jnp.where(own_ok, _dot3(qh, kn_ref[...][:, hs], NT) * scale, -jnp.inf)
        m_own = jnp.max(s_own, axis=-1, keepdims=True)
        out = jnp.zeros((rows_n, HD_D), F32)
        for s in range(n_q):
            s_sel = _dot3_cat(qh, kbuf[slot, hh, s]) * scale
            m = jnp.maximum(m_own, jnp.max(s_sel, axis=-1, keepdims=True))
            p_sel = jnp.exp(s_sel - m)
            p_own = jnp.exp(s_own - m)
            l = jnp.sum(p_sel, axis=-1, keepdims=True) + jnp.sum(p_own, axis=-1, keepdims=True)
            o_s = (_dot3(p_sel, vbuf[slot, hh, s], NT) + _dot3(p_own, vn_ref[...][:, hs])) / l
            out = jnp.where(rsel == s, o_s, out)
        o_ref[:, hs] = out


def _moba_sample_attend(phys, q, k_new, v_new, cache_kt, cache_vt, *, db, n_q):
    row = pl.BlockSpec((SAMPLE_PAD, V7X_LANES), lambda b, hp, ph: (b, hp))
    any_spec = pl.BlockSpec(memory_space=pl.ANY)
    buf = pltpu.VMEM((2, HEADS_PER_STEP, n_q, HD_D, MOBA_TOPK * MOBA_BLOCK), F32)
    grid_spec = pltpu.PrefetchScalarGridSpec(
        num_scalar_prefetch=1,
        grid=(db, D_D // V7X_LANES),
        in_specs=[row, row, row, any_spec, any_spec],
        out_specs=row,
        scratch_shapes=[buf, buf, pltpu.SemaphoreType.DMA((2, 2))],
    )
    return pl.pallas_call(
        functools.partial(_attend_kernel, n_q=n_q),
        grid_spec=grid_spec,
        out_shape=jax.ShapeDtypeStruct((db * SAMPLE_PAD, D_D), F32),
        compiler_params=_params(2),
        name="moba_sample_attend",
    )(phys, q, k_new, v_new, cache_kt, cache_vt)


def _outproj2_ln_kernel(a1_ref, a2_ref, x_ref, w1h_ref, w1l_ref, w2h_ref, w2l_ref, g_ref, b_ref, o_ref):
    y = _dot_hl(*_split2(a1_ref[...]), w1h_ref[...], w1l_ref[...])
    y = y + _dot_hl(*_split2(a2_ref[...]), w2h_ref[...], w2l_ref[...])
    o_ref[...] = _layer_norm(DN_ALPHA * x_ref[...] + y, g_ref[...], b_ref[...])


def _outproj2_ln(a1, a2, x, w1, w2, g, b, *, n_rows, tm):
    d = x.shape[1]
    assert n_rows % tm == 0
    row = lambda c: pl.BlockSpec((tm, c), lambda i: (i, 0))
    consts = [w1[0], w1[1], w2[0], w2[1], g, b]
    return pl.pallas_call(
        _outproj2_ln_kernel,
        grid=(n_rows // tm,),
        in_specs=[row(a1.shape[1]), row(a2.shape[1]), row(d)] + [_const_spec(c.shape) for c in consts],
        out_specs=row(d),
        out_shape=jax.ShapeDtypeStruct((n_rows, d), F32),
        compiler_params=_params(1),
        name="outproj2_ln",
    )(a1, a2, x, *consts)


SAMPLE_PAD = 16


def _pad_rows(x, t_pad):
    nb, t, d = x.shape
    return jnp.zeros((nb, t_pad, d), x.dtype).at[:, :t].set(x).reshape(nb * t_pad, d)


def kernel(x_prompt, x_sample, state_gla, state_ssm, state_conv, state_pool, cache_k, cache_v, page_table, ab_w_in, ab_w_gate2, ab_b_gate, ab_gla_norm, ab_conv_w, ab_conv_b, ab_dt_bias, ab_a_log, ab_d_skip, ab_ssm_norm, ab_w_out, cd_w_in, cd_w_pool, cd_pool_scale, cd_w_out, moe_w_coarse, moe_b_coarse, moe_w_fine, moe_b_fine, moe_w_gate_up, moe_w_down, ln_g, ln_b):
    bp, tp, d = x_prompt.shape
    bs, ts, _ = x_sample.shape
    (wih, wil), ab_prm, (woh, wol) = _ab_prepare(ab_w_in, ab_w_gate2, ab_b_gate, ab_gla_norm, ab_conv_w,
                                                 ab_conv_b, ab_dt_bias, ab_a_log, ab_d_skip, ab_ssm_norm, ab_w_out)
    n_p, n_s = bp * tp, bs * ts
    n_pages = page_table.shape[1]
    past = n_pages * PAGE_SIZE
    assert past % MOBA_BLOCK == 0 and past // MOBA_BLOCK >= MOBA_TOPK and ts <= SAMPLE_PAD
    zeros = lambda *s: jnp.zeros(s, F32)
    ln = lambda l, j: (ln_g[l, j].reshape(1, d), ln_b[l, j].reshape(1, d))
    unpad = lambda rows: rows.reshape(bs, SAMPLE_PAD, -1)[:, :ts]
    moe = lambda l, x: _moe_layer(x, moe_w_coarse[l], moe_b_coarse[l], moe_w_fine[l], moe_b_fine[l],
                                  moe_w_gate_up[l], moe_w_down[l], *ln(l, 1), tm=V7X_LANES,
                                  three_pass=l < DEPTH - 1)

    xp = x_prompt.reshape(n_p, d)
    xs = _pad_rows(x_sample, SAMPLE_PAD)
    proj_p = _proj(xp, wih, wil)
    oy_p, gla_p, ssm_p, conv_p = _scan_from_proj(
        proj_p, zeros(bp, H_A, DK_A, DV_A), zeros(bp, H_B, P_B, N_B), zeros(bp, CONV_W - 1, CONV_DIM), ab_prm,
        nb=bp, t_len=tp, tc=256, chunk=CHUNK, valid=256)
    xp = _outproj_ln(oy_p, xp, woh, wol, *ln(0, 0))
    proj_s = _proj(xs, wih, wil)
    oy_s, gla_s, ssm_s, conv_s = _scan_from_proj(
        proj_s, state_gla, state_ssm, state_conv, ab_prm,
        nb=bs, t_len=SAMPLE_PAD, tc=SAMPLE_PAD, chunk=SAMPLE_PAD, valid=ts)
    xs = _outproj_ln(oy_s, xs, woh, wol, *ln(0, 0))
    x_all = moe(0, jnp.concatenate([xp, unpad(xs).reshape(n_s, d)], axis=0))

    wch, wcl = _split_w(cd_w_in)
    wph, wpl = _split_w(cd_w_pool)
    pscale = cd_pool_scale.reshape(1, D_C)
    wo_c, wo_d = _split_w(cd_w_out[:D_C]), _split_w(cd_w_out[D_C:])
    cos_p, sin_p, perm = _rope_tables(jnp.arange(tp))
    u, q, k, v = _cdproj(x_all, wch, wcl, cos_p, sin_p, perm, n_rows=n_p, tm=256)
    pooled, pool_p = _pool(u, zeros(bp, POOL_BUF, D_C), wph, wpl, pscale, nb=bp, t_len=tp, tc=256, valid=256, start=0)
    att = _moba_prompt(q, k, v, nb=bp, t_len=tp)
    xp = _outproj2_ln(pooled, att, x_all, wo_c, wo_d, *ln(1, 0), n_rows=n_p, tm=256)
    k_p, v_p = k.reshape(bp, tp, H_D, HD_D), v.reshape(bp, tp, H_D, HD_D)

    xs = _pad_rows(x_all[n_p:].reshape(bs, ts, d), SAMPLE_PAD)
    cos_s, sin_s, _ = _rope_tables(past + jnp.arange(SAMPLE_PAD))
    n_rows_s = bs * SAMPLE_PAD
    u, q, k, v = _cdproj(xs, wch, wcl, jnp.tile(cos_s, (bs, 1)), jnp.tile(sin_s, (bs, 1)), perm,
                         n_rows=n_rows_s, tm=256)
    pooled, pool_s = _pool(u, state_pool, wph, wpl, pscale, nb=bs, t_len=SAMPLE_PAD, tc=SAMPLE_PAD, valid=ts,
                           start=past)
    cache_kt, cache_vt = jnp.transpose(cache_k, (0, 2, 3, 1)), jnp.transpose(cache_v, (0, 2, 3, 1))
    picks = _select_blocks(q, _block_means(cache_kt, page_table))[:, :, :ts, :MOBA_TOPK]
    logical = picks[..., None] * PAGES_PER_BLOCK + jnp.arange(PAGES_PER_BLOCK)
    phys = page_table[jnp.arange(bs)[:, None, None, None, None], logical].reshape(-1)
    att = _moba_sample_attend(phys, q, k, v, cache_kt, cache_vt, db=bs, n_q=ts)
    xs = _outproj2_ln(pooled, att, xs, wo_c, wo_d, *ln(1, 0), n_rows=n_rows_s, tm=256)
    k_s, v_s = unpad(k).reshape(bs, ts, H_D, HD_D), unpad(v).reshape(bs, ts, H_D, HD_D)
    x_all = moe(1, jnp.concatenate([xp, unpad(xs).reshape(n_s, d)], axis=0))

    return (x_all[:n_p].reshape(bp, tp, d), x_all[n_p:].reshape(bs, ts, d), gla_p, ssm_p, conv_p, pool_p, k_p, v_p,
            gla_s, ssm_s, conv_s, pool_s, k_s, v_s)
```

```python
import functools

import jax
import jax.numpy as jnp
import numpy as np
from jax import lax
from jax.experimental import pallas as pl
from jax.experimental.pallas import tpu as pltpu

F32 = jnp.float32
BF16 = jnp.bfloat16

D_MODEL = 1024
DEPTH = 2
PAGE_SIZE = 128
H_A, DK_A, DV_A = 4, 64, 128
GATE_RANK = 16
GATE_TAU = 16.0
CHUNK = 64
H_B, P_B, N_B, G_B = 8, 64, 64, 2
CONV_W = 4
D_B = H_B * P_B
CONV_DIM = D_B + 2 * G_B * N_B
POOL_WINDOWS = (2, 4, 8, 16)
POOL_GROUP = 128
D_C = len(POOL_WINDOWS) * POOL_GROUP
POOL_BUF = max(POOL_WINDOWS) - 1
H_D, HD_D = 8, 64
D_D = H_D * HD_D
MOBA_BLOCK = 256
MOBA_TOPK = 3
ROT_DIM = HD_D // 4
ROPE_THETA = 500000.0
N_GROUPS, EXP_PER_GROUP, TOP_FINE, D_EXPERT = 4, 8, 2, 256
N_EXPERTS = N_GROUPS * EXP_PER_GROUP
DN_ALPHA = (2 * DEPTH) ** 0.25
LN_EPS = 1e-5
RMS_EPS = 1e-6

V7X_LANES = 128
V7X_SUBLANES = 8
V7X_VMEM_LIMIT_BYTES = 56 * 1024 * 1024

NN = (((1,), (0,)), ((), ()))
NT = (((1,), (1,)), ((), ()))
TN = (((0,), (0,)), ((), ()))


def _params(n_axes):
    return pltpu.CompilerParams(dimension_semantics=("arbitrary",) * n_axes,
                                vmem_limit_bytes=V7X_VMEM_LIMIT_BYTES)


def _const_spec(shape):
    zeros = (0,) * len(shape)
    return pl.BlockSpec(shape, lambda *_: zeros, pipeline_mode=pl.Buffered(1))


def _split2(a):
    hi = a.astype(BF16)
    lo = (a - hi.astype(F32)).astype(BF16)
    return hi, lo


def _split3(a):
    hi = a.astype(BF16)
    r = a - hi.astype(F32)
    mid = r.astype(BF16)
    lo = (r - mid.astype(F32)).astype(BF16)
    return hi, mid, lo


def _mm(a, b, dims=NN):
    return lax.dot_general(a, b, dims, preferred_element_type=F32)


def _dot_hl(ah, al, bh, bl, dims=NN):
    return (_mm(ah, bl, dims) + _mm(al, bh, dims)) + _mm(ah, bh, dims)


def _dot3(a, b, dims=NN):
    ah, al = _split2(a)
    bh, bl = _split2(b)
    return _dot_hl(ah, al, bh, bl, dims)


def _dot3_cat(a, b, dims=NN):
    (ca,), (cb,) = dims[0]
    ah, al = _split2(a)
    bh, bl = _split2(b)
    return _mm(jnp.concatenate([ah, al, ah], axis=ca), jnp.concatenate([bh, bh, bl], axis=cb), dims)


def _dotx(a, e, dims=NN):
    h, m, l = _split3(a)
    return (_mm(l, e, dims) + _mm(m, e, dims)) + _mm(h, e, dims)


def _xdot(e, a, dims=NN):
    h, m, l = _split3(a)
    return (_mm(e, l, dims) + _mm(e, m, dims)) + _mm(e, h, dims)


def _silu(x):
    return x / (1.0 + jnp.exp(-x))


def _softplus(x):
    return jnp.maximum(x, 0.0) + jnp.log1p(jnp.exp(-jnp.abs(x)))


def _log_sigmoid(x):
    return jnp.minimum(x, 0.0) - jnp.log1p(jnp.exp(-jnp.abs(x)))


def _layer_norm(z, g, b):
    mu = jnp.mean(z, axis=-1, keepdims=True)
    zc = z - mu
    var = jnp.mean(zc * zc, axis=-1, keepdims=True)
    return zc * lax.rsqrt(var + LN_EPS) * g + b


def _iota(shape, dim):
    return lax.broadcasted_iota(jnp.int32, shape, dim)


def _proj_kernel(x_ref, wh_ref, wl_ref, o_ref, *, col_chunk):
    xh, xl = _split2(x_ref[...])
    m = o_ref.shape[1]
    for j in range(0, m, col_chunk):
        cs = slice(j, min(j + col_chunk, m))
        o_ref[:, cs] = _dot_hl(xh, xl, wh_ref[:, cs], wl_ref[:, cs])


def _proj(x, wh, wl, *, n_rows=None, tm=256, col_chunk=512):
    n = x.shape[0] if n_rows is None else n_rows
    k = x.shape[1]
    m = wh.shape[1]
    assert n % tm == 0
    return pl.pallas_call(
        functools.partial(_proj_kernel, col_chunk=col_chunk),
        grid=(n // tm,),
        in_specs=[pl.BlockSpec((tm, k), lambda i: (i, 0)), _const_spec((k, m)), _const_spec((k, m))],
        out_specs=pl.BlockSpec((tm, m), lambda i: (i, 0)),
        out_shape=jax.ShapeDtypeStruct((n, m), F32),
        compiler_params=_params(1),
        name="proj",
    )(x, wh, wl)


def _outproj_ln_kernel(a_ref, x_ref, wh_ref, wl_ref, g_ref, b_ref, o_ref):
    ah, al = _split2(a_ref[...])
    y = _dot_hl(ah, al, wh_ref[...], wl_ref[...])
    o_ref[...] = _layer_norm(DN_ALPHA * x_ref[...] + y, g_ref[...], b_ref[...])


def _outproj_ln(a, x, wh, wl, g, b, *, tm=256):
    n, k = a.shape
    d = wh.shape[1]
    assert n % tm == 0
    return pl.pallas_call(
        _outproj_ln_kernel,
        grid=(n // tm,),
        in_specs=[pl.BlockSpec((tm, k), lambda i: (i, 0)), pl.BlockSpec((tm, d), lambda i: (i, 0)),
                  _const_spec((k, d)), _const_spec((k, d)), _const_spec((1, d)), _const_spec((1, d))],
        out_specs=pl.BlockSpec((tm, d), lambda i: (i, 0)),
        out_shape=jax.ShapeDtypeStruct((n, d), F32),
        compiler_params=_params(1),
        name="outproj_ln",
    )(a, x, wh, wl, g, b)


SM_GLR0 = 0
SM_DT0 = GATE_RANK
CONV_HDR = V7X_SUBLANES


def _scan_kernel(qkvr_ref, z_ref, xbc_ref, sm_ref, s0_ref, h0_ref, c0_ref,
                 wg2h_ref, wg2l_ref, bg_ref, gnorm_ref, cw_ref, cb_ref, dtb_ref, alog_ref,
                 dskip_ref, snorm_ref,
                 oy_ref, s_out_ref, h_out_ref, c_out_ref,
                 st_sc, h_sc, ext_sc, o_sc, y_sc, *, tc, chunk, valid):
    t = pl.program_id(1)
    nt = pl.num_programs(1)
    n_chunks = tc // chunk
    shift = chunk.bit_length() - 1
    assert (1 << shift) == chunk and tc % chunk == 0

    @pl.when(t == 0)
    def _():
        st_sc[...] = s0_ref[0]
        h_sc[...] = h0_ref[0]
        ext_sc[0:CONV_HDR, :] = c0_ref[0]

    row = _iota((tc, tc), 0)
    col = _iota((tc, tc), 1)
    causal = ((row >> shift) == (col >> shift)) & (col <= row)
    l_tri = jnp.where(causal, 1.0, 0.0).astype(BF16)
    lane = _iota((tc, V7X_LANES), 1)
    rowv = _iota((tc, V7X_LANES), 0)
    dt_lane = (lane >= SM_DT0) & (lane < SM_DT0 + H_B)
    if valid < tc:
        row_ok = rowv < valid
        dt_lane = dt_lane & row_ok
    e8 = jnp.where((_iota((V7X_LANES, D_B), 1) >> 6) + SM_DT0 == _iota((V7X_LANES, D_B), 0), 1.0, 0.0).astype(BF16)

    sm = sm_ref[...]
    pre = _dot_hl(*_split2(sm), wg2h_ref[...], wg2l_ref[...])
    log_a = _log_sigmoid(pre + bg_ref[...]) / GATE_TAU
    if valid < tc:
        log_a = jnp.where(_iota((tc, H_A * DK_A), 0) < valid, log_a, 0.0)
    b_cum = _xdot(l_tri, log_a)
    dt_pad = jnp.where(dt_lane, _softplus(sm + dtb_ref[...]), 0.0)
    a_pad = -jnp.exp(alog_ref[...])
    la_pad = jnp.where(dt_lane, dt_pad * a_pad, 0.0)
    cum_pad = _xdot(l_tri, la_pad)
    dt_rep = _dotx(dt_pad, e8)
    cum_rep = _dotx(cum_pad, e8)

    ext_sc[CONV_HDR:CONV_HDR + tc, :] = xbc_ref[...]
    cw = cw_ref[...]
    acc = ext_sc[CONV_HDR - 3:CONV_HDR - 3 + tc, :] * cw[0:1, :]
    for i in range(1, CONV_W):
        acc = acc + ext_sc[CONV_HDR - 3 + i:CONV_HDR - 3 + i + tc, :] * cw[i:i + 1, :]
    xc = _silu(acc + cb_ref[...])

    @pl.when(t == nt - 1)
    def _():
        c_out_ref[0] = ext_sc[valid:valid + CONV_HDR, :]

    ext_sc[0:CONV_HDR, :] = ext_sc[tc:tc + CONV_HDR, :]

    xs = xc[:, 0:D_B]
    bm = xc[:, D_B:D_B + G_B * N_B]
    cm = xc[:, D_B + G_B * N_B:CONV_DIM]
    xdt = xs * dt_rep

    qkvr = qkvr_ref[...]
    q = qkvr[:, 0:H_A * DK_A] * (DK_A ** -0.5)
    k = qkvr[:, H_A * DK_A:2 * H_A * DK_A]
    v = qkvr[:, 2 * H_A * DK_A:2 * H_A * DK_A + H_A * DV_A]
    r = qkvr[:, 2 * H_A * DK_A + H_A * DV_A:]
    q_dec = q * jnp.exp(b_cum)
    k_dec = k * jnp.exp(-b_cum)
    for h in range(H_A):
        ks = slice(h * DK_A, (h + 1) * DK_A)
        vs = slice(h * DV_A, (h + 1) * DV_A)
        att = jnp.where(causal, _dot3_cat(q_dec[:, ks], k_dec[:, ks], NT), 0.0)
        o_sc[:, vs] = _dot3(att, v[:, vs])
    for c in range(n_chunks):
        rows = slice(c * chunk, (c + 1) * chunk)
        last = slice((c + 1) * chunk - 1, (c + 1) * chunk)
        b_last = b_cum[last, :]
        k_end = k[rows, :] * jnp.exp(b_last - b_cum[rows, :])
        dec = jnp.exp(b_last)
        for h in range(H_A):
            ks = slice(h * DK_A, (h + 1) * DK_A)
            vs = slice(h * DV_A, (h + 1) * DV_A)
            st = st_sc[h]
            o_sc[rows, vs] = o_sc[rows, vs] + _dot3_cat(q_dec[rows, ks], st, NT)
            st_sc[h] = st * dec[:, ks] + _dot3_cat(v[rows, vs], k_end[:, ks], TN)

    for g in range(G_B):
        gs = slice(g * N_B, (g + 1) * N_B)
        cb = _dot3_cat(cm[:, gs], bm[:, gs], NT)
        for hh in range(g * (H_B // G_B), (g + 1) * (H_B // G_B)):
            ps = slice(hh * P_B, (hh + 1) * P_B)
            ea = jnp.where(_iota((V7X_LANES, tc), 0) == SM_DT0 + hh, 1.0, 0.0).astype(BF16)
            m1 = jnp.where(_iota((tc, V7X_LANES), 1) == SM_DT0 + hh, 1.0, 0.0).astype(BF16)
            seg = _dotx(cum_pad, ea) - _xdot(m1, cum_pad, NT)
            w = cb * jnp.exp(jnp.where(causal, seg, -jnp.inf))
            y_sc[:, ps] = _dot3(w, xdt[:, ps])
    for c in range(n_chunks):
        rows = slice(c * chunk, (c + 1) * chunk)
        last = slice((c + 1) * chunk - 1, (c + 1) * chunk)
        cum_c = cum_rep[rows, :]
        cum_l = cum_rep[last, :]
        e_cum = jnp.exp(cum_c)
        x_dec = xdt[rows, :] * jnp.exp(cum_l - cum_c)
        e_last = jnp.exp(cum_l)
        for hh in range(H_B):
            g = hh // (H_B // G_B)
            gs = slice(g * N_B, (g + 1) * N_B)
            ps = slice(hh * P_B, (hh + 1) * P_B)
            hs = h_sc[hh]
            y_sc[rows, ps] = y_sc[rows, ps] + _dot3_cat(cm[rows, gs], hs, NT) * e_cum[:, ps]
            h_sc[hh] = hs * e_last[:, ps] + _dot3_cat(x_dec[:, ps], bm[rows, gs], TN)

    gn = gnorm_ref[...]
    for h in range(H_A):
        vs = slice(h * DV_A, (h + 1) * DV_A)
        oh = o_sc[:, vs]
        oh = oh * lax.rsqrt(jnp.mean(oh * oh, axis=-1, keepdims=True) + RMS_EPS) * gn
        oy_ref[:, vs] = oh * _silu(r[:, vs])
    y = (y_sc[...] + dskip_ref[...] * xs) * _silu(z_ref[...])
    y = y * lax.rsqrt(jnp.mean(y * y, axis=-1, keepdims=True) + RMS_EPS) * snorm_ref[...]
    oy_ref[:, H_A * DV_A:] = y

    @pl.when(t == nt - 1)
    def _():
        s_out_ref[0] = st_sc[...]
        h_out_ref[0] = h_sc[...]


def _scan_call(proj, s0t, h0, c0, prm, *, specs, nb, t_len, tc, chunk, valid):
    nt = t_len // tc
    assert t_len % tc == 0 and (valid == tc or nt == 1)
    rowspec = lambda c: pl.BlockSpec((tc, c), lambda b, t: (b * nt + t, 0))
    stspec = lambda shp: pl.BlockSpec((1,) + shp, lambda b, t: (b,) + (0,) * len(shp))
    names = ("wg2h", "wg2l", "bg", "gnorm", "cw", "cb", "dtb", "alog", "dskip", "snorm")
    consts = [prm[n] for n in names]
    return pl.pallas_call(
        functools.partial(_scan_kernel, tc=tc, chunk=chunk, valid=valid),
        grid=(nb, nt),
        in_specs=list(specs)
                 + [stspec((H_A, DV_A, DK_A)), stspec((H_B, P_B, N_B)), stspec((CONV_HDR, CONV_DIM))]
                 + [_const_spec(c.shape) for c in consts],
        out_specs=[rowspec(H_A * DV_A + D_B), stspec((H_A, DV_A, DK_A)), stspec((H_B, P_B, N_B)),
                   stspec((CONV_HDR, CONV_DIM))],
        out_shape=[jax.ShapeDtypeStruct((nb * t_len, H_A * DV_A + D_B), F32),
                   jax.ShapeDtypeStruct((nb, H_A, DV_A, DK_A), F32),
                   jax.ShapeDtypeStruct((nb, H_B, P_B, N_B), F32),
                   jax.ShapeDtypeStruct((nb, CONV_HDR, CONV_DIM), F32)],
        scratch_shapes=[pltpu.VMEM((H_A, DV_A, DK_A), F32), pltpu.VMEM((H_B, P_B, N_B), F32),
                        pltpu.VMEM((tc + CONV_HDR, CONV_DIM), F32),
                        pltpu.VMEM((tc, H_A * DV_A), F32), pltpu.VMEM((tc, D_B), F32)],
        compiler_params=_params(2),
        name="gla_ssd_scan",
    )(proj, proj, proj, proj, s0t, h0, c0, *consts)


AB_QKVR = 2 * H_A * DK_A + 2 * H_A * DV_A
AB_XBC0 = AB_QKVR
AB_SM0 = AB_XBC0 + CONV_DIM
AB_Z0 = 2560
AB_COLS = AB_Z0 + D_B


def _split_w(w):
    hi = w.astype(BF16)
    lo = (w - hi.astype(F32)).astype(BF16)
    return hi, lo


def _lane_pad(v, start, width=V7X_LANES):
    out = jnp.zeros((1, width), F32)
    return out.at[0, start:start + v.shape[0]].set(v.astype(F32))


def _ab_prepare(w_in, w_gate2, b_gate, gla_norm, conv_w, conv_b, dt_bias, a_log, d_skip, ssm_norm, w_out):
    o = np.cumsum([0, H_A * DK_A, H_A * DK_A, H_A * DV_A, H_A * DV_A, GATE_RANK, D_B, CONV_DIM, H_B])
    q0, gl0, z0, xbc0, dt0, end = o[0], o[4], o[5], o[6], o[7], o[8]
    k_dim = w_in.shape[0]
    w = jnp.concatenate([
        w_in[:, q0:gl0], w_in[:, xbc0:dt0], w_in[:, gl0:z0], w_in[:, dt0:end],
        jnp.zeros((k_dim, AB_Z0 - AB_SM0 - GATE_RANK - H_B), F32), w_in[:, z0:xbc0]], axis=1)
    assert w.shape[1] == AB_COLS
    wg2 = jnp.zeros((V7X_LANES, H_A * DK_A), F32).at[0:GATE_RANK].set(w_gate2)
    wg2h, wg2l = _split_w(wg2)
    prm = dict(
        wg2h=wg2h, wg2l=wg2l, bg=b_gate.reshape(1, -1), gnorm=gla_norm.reshape(1, -1),
        cw=conv_w, cb=conv_b.reshape(1, -1), dtb=_lane_pad(dt_bias, SM_DT0), alog=_lane_pad(a_log, SM_DT0),
        dskip=jnp.repeat(d_skip, P_B).reshape(1, -1), snorm=ssm_norm.reshape(1, -1))
    return _split_w(w), prm, _split_w(w_out)


def _scan_from_proj(proj, s0, h0, c0, prm, *, nb, t_len, tc, chunk, valid):
    nt = t_len // tc
    s0t = jnp.swapaxes(s0, -1, -2)
    c0p = jnp.zeros((nb, CONV_HDR, CONV_DIM), F32).at[:, CONV_HDR - (CONV_W - 1):].set(c0)
    colspec = lambda width, start: pl.BlockSpec((tc, width), lambda b, t: (b * nt + t, start // width))
    oy, s_new, h_new, c_new = _scan_call(
        proj, s0t, h0, c0p, prm,
        specs=[colspec(AB_QKVR, 0), colspec(D_B, AB_Z0), colspec(CONV_DIM, AB_XBC0), colspec(V7X_LANES, AB_SM0)],
        nb=nb, t_len=t_len, tc=tc, chunk=chunk, valid=valid)
    return oy, jnp.swapaxes(s_new, -1, -2), h_new, c_new[:, CONV_HDR - (CONV_W - 1):]


RT_FINE0 = N_GROUPS
RT_HALF = V7X_LANES // 2
MOE_TILE = 256


def _two_source_specs(na, nb, tm, width):
    assert na % tm == 0 and nb % tm == 0
    na_t = na // tm
    return (pl.BlockSpec((tm, width), lambda i, *_: (jnp.minimum(i, na_t - 1), 0)),
            pl.BlockSpec((tm, width), lambda i, *_: (jnp.maximum(i - na_t, 0), 0)), na_t)


def _router_kernel(xa_ref, xb_ref, wh_ref, wl_ref, b_ref, id_ref, rank_ref, gate_ref, cnt_ref, base_sc, *, na_t):
    @pl.when(pl.program_id(0) == 0)
    def _():
        base_sc[...] = jnp.zeros(base_sc.shape, F32)

    xh, xl = _split2(jnp.where(pl.program_id(0) < na_t, xa_ref[...], xb_ref[...]))
    logits = _dot_hl(xh, xl, wh_ref[...], wl_ref[...]) + b_ref[...]
    shp = logits.shape
    lane = _iota(shp, 1)
    lane_f = lane.astype(F32)
    big = float(V7X_LANES)
    ninf = -jnp.inf

    def first_max(mask):
        v = jnp.max(jnp.where(mask, logits, ninf), axis=-1, keepdims=True)
        i = jnp.min(jnp.where(mask & (logits == v), lane_f, big), axis=-1, keepdims=True)
        return v, i

    is_c = lane < N_GROUPS
    mc, grp = first_max(is_c)
    p_grp = 1.0 / jnp.sum(jnp.where(is_c, jnp.exp(logits - mc), 0.0), axis=-1, keepdims=True)
    fine = (lane >= RT_FINE0) & (lane < RT_FINE0 + N_EXPERTS)
    cand = fine & (((lane - RT_FINE0) >> 3).astype(F32) == grp)
    v1, i1 = first_max(cand)
    v2, i2 = first_max(cand & (lane_f != i1))
    e = jnp.exp(v2 - v1)
    w1 = p_grp / (1.0 + e)
    w2 = p_grp * (e / (1.0 + e))
    first = lane < RT_HALF
    id_ref[...] = (jnp.where(first, i1, i2) - float(RT_FINE0)).astype(jnp.int32)
    gate_ref[...] = jnp.where(first, w1, w2)

    tm = shp[0]
    oh1 = jnp.where(lane_f == i1, 1.0, 0.0)
    oh2 = jnp.where(lane_f == i2, 1.0, 0.0)
    oh = oh1 + oh2
    earlier = jnp.where(_iota((tm, tm), 1) < _iota((tm, tm), 0), 1.0, 0.0).astype(BF16)
    before = _mm(earlier, oh.astype(BF16)) + base_sc[...]
    r1 = jnp.sum(oh1 * before, axis=-1, keepdims=True)
    r2 = jnp.sum(oh2 * before, axis=-1, keepdims=True)
    rank_ref[...] = jnp.where(first, r1, r2).astype(jnp.int32)
    base_sc[...] = base_sc[...] + jnp.sum(oh, axis=0, keepdims=True)
    cnt_ref[...] = base_sc[...]


def _router(xa, xb, wh, wl, b, *, tm):
    d = xa.shape[1]
    n = xa.shape[0] + xb.shape[0]
    spec_a, spec_b, na_t = _two_source_specs(xa.shape[0], xb.shape[0], tm, d)
    row = pl.BlockSpec((tm, V7X_LANES), lambda i: (i, 0))
    return pl.pallas_call(
        functools.partial(_router_kernel, na_t=na_t),
        grid=(n // tm,),
        in_specs=[spec_a, spec_b, _const_spec(wh.shape), _const_spec(wl.shape), _const_spec(b.shape)],
        out_specs=[row, row, row, pl.BlockSpec((1, V7X_LANES), lambda i: (0, 0))],
        out_shape=[jax.ShapeDtypeStruct((n, V7X_LANES), jnp.int32), jax.ShapeDtypeStruct((n, V7X_LANES), jnp.int32),
                   jax.ShapeDtypeStruct((n, V7X_LANES), F32), jax.ShapeDtypeStruct((1, V7X_LANES), F32)],
        scratch_shapes=[pltpu.VMEM((1, V7X_LANES), F32)],
        compiler_params=_params(1),
        name="moe_router",
    )(xa, xb, wh, wl, b)


DMA_UNROLL = 8


def _pos_copy(pos_hbm, pos_sm, sem_p, step, slot):
    return pltpu.make_async_copy(pos_hbm.at[pl.ds(step, 1)], pos_sm.at[slot], sem_p.at[slot])


def _dispatch_kernel(zf_ref, xa_ref, xb_ref, pos_hbm, xs_hbm, pos_sm, zbuf, sem_p, sem_z, sem_s, *, na_t):
    i = pl.program_id(0)
    n = pl.num_programs(0)
    tm = xa_ref.shape[0]
    tile = zbuf.shape[0]
    n_tiles = xs_hbm.shape[0] // tile
    slot = lax.rem(i, 2)

    def zero_copy(j):
        return pltpu.make_async_copy(zbuf, xs_hbm.at[pl.ds(pl.multiple_of(j * tile, tile), tile)], sem_z)

    @pl.when(i == 0)
    def _():
        _pos_copy(pos_hbm, pos_sm, sem_p, 0, 0).start()
        zbuf[...] = jnp.zeros(zbuf.shape, F32)

        def z_start(j, c):
            @pl.when(zf_ref[j] > 0)
            def _():
                zero_copy(j).start()
            return c

        def z_wait(j, c):
            @pl.when(zf_ref[j] > 0)
            def _():
                zero_copy(j).wait()
            return c

        lax.fori_loop(0, n_tiles, z_start, 0)
        lax.fori_loop(0, n_tiles, z_wait, 0)

    @pl.when(i + 1 < n)
    def _():
        _pos_copy(pos_hbm, pos_sm, sem_p, i + 1, 1 - slot).start()

    _pos_copy(pos_hbm, pos_sm, sem_p, i, slot).wait()

    def scatter_rows(x_ref):
        def issue(r, c):
            for k in range(TOP_FINE):
                dst = pos_sm[slot, 0, k * tm + r]
                pltpu.make_async_copy(x_ref.at[pl.ds(r, 1)], xs_hbm.at[pl.ds(dst, 1)], sem_s).start()
            return c

        lax.fori_loop(0, tm, issue, 0, unroll=DMA_UNROLL)
        for _ in range(TOP_FINE):
            pltpu.make_async_copy(x_ref, xs_hbm.at[pl.ds(0, tm)], sem_s).wait()

    @pl.when(i < na_t)
    def _():
        scatter_rows(xa_ref)

    @pl.when(i >= na_t)
    def _():
        scatter_rows(xb_ref)


def _dispatch(xa, xb, pos_tab, zero_flag, *, n_tiles, tm):
    d = xa.shape[1]
    n = xa.shape[0] + xb.shape[0]
    spec_a, spec_b, na_t = _two_source_specs(xa.shape[0], xb.shape[0], tm, d)
    any_spec = pl.BlockSpec(memory_space=pl.ANY)
    grid_spec = pltpu.PrefetchScalarGridSpec(
        num_scalar_prefetch=1,
        grid=(n // tm,),
        in_specs=[spec_a, spec_b, any_spec],
        out_specs=any_spec,
        scratch_shapes=[pltpu.SMEM((2, 1, TOP_FINE * tm), jnp.int32), pltpu.VMEM((MOE_TILE, d), F32),
                        pltpu.SemaphoreType.DMA((2,)), pltpu.SemaphoreType.DMA(()), pltpu.SemaphoreType.DMA(())],
    )
    return pl.pallas_call(
        functools.partial(_dispatch_kernel, na_t=na_t),
        grid_spec=grid_spec,
        out_shape=jax.ShapeDtypeStruct((n_tiles * MOE_TILE, d), F32),
        compiler_params=_params(1),
        name="moe_dispatch",
    )(zero_flag, xa, xb, pos_tab)


def _expert_kernel(te_ref, xs_ref, wgu_ref, wd_ref, ys_ref, *, three_pass):
    if three_pass:
        xh, xl = _split2(xs_ref[...])
        wh, wl = _split2(wgu_ref[0])
        hgu = _dot_hl(xh, xl, wh, wl)
    else:
        hgu = _mm(xs_ref[...].astype(BF16), wgu_ref[0].astype(BF16))
    act = _silu(hgu[:, :D_EXPERT]) * hgu[:, D_EXPERT:]
    if three_pass:
        ys_ref[...] = _dot3(act, wd_ref[0])
    else:
        ys_ref[...] = _mm(act.astype(BF16), wd_ref[0].astype(BF16))


def _experts(xs, w_gate_up, w_down, tile_expert, *, three_pass):
    p, d = xs.shape
    f2 = w_gate_up.shape[-1]
    row = pl.BlockSpec((MOE_TILE, d), lambda i, te: (i, 0))
    grid_spec = pltpu.PrefetchScalarGridSpec(
        num_scalar_prefetch=1,
        grid=(p // MOE_TILE,),
        in_specs=[row, pl.BlockSpec((1, d, f2), lambda i, te: (te[i], 0, 0)),
                  pl.BlockSpec((1, f2 // 2, d), lambda i, te: (te[i], 0, 0))],
        out_specs=row,
    )
    return pl.pallas_call(
        functools.partial(_expert_kernel, three_pass=three_pass),
        grid_spec=grid_spec,
        out_shape=jax.ShapeDtypeStruct((p, d), F32),
        compiler_params=_params(1),
        name="moe_experts",
    )(tile_expert, xs, w_gate_up, w_down)


def _combine_ln_kernel(xa_ref, xb_ref, gate_ref, g_ref, b_ref, pos_hbm, ys_hbm, oa_ref, ob_ref,
                       pos_sm, ybuf, sem_p, sem_y, *, na_t):
    i = pl.program_id(0)
    n = pl.num_programs(0)
    tm = xa_ref.shape[0]
    slot = lax.rem(i, 2)

    def issue_gathers(s):
        def issue(r, c):
            for k in range(TOP_FINE):
                src = pos_sm[s, 0, k * tm + r]
                pltpu.make_async_copy(ys_hbm.at[pl.ds(src, 1)], ybuf.at[s, pl.ds(k * tm + r, 1)],
                                      sem_y.at[s]).start()
            return c

        lax.fori_loop(0, tm, issue, 0, unroll=DMA_UNROLL)

    @pl.when(i == 0)
    def _():
        _pos_copy(pos_hbm, pos_sm, sem_p, 0, 0).start()
        _pos_copy(pos_hbm, pos_sm, sem_p, 0, 0).wait()
        issue_gathers(0)

        @pl.when(n > 1)
        def _():
            _pos_copy(pos_hbm, pos_sm, sem_p, 1, 1).start()

    @pl.when(i + 1 < n)
    def _():
        _pos_copy(pos_hbm, pos_sm, sem_p, i + 1, 1 - slot).wait()
        issue_gathers(1 - slot)

    @pl.when(i + 2 < n)
    def _():
        _pos_copy(pos_hbm, pos_sm, sem_p, i + 2, slot).start()

    pltpu.make_async_copy(ys_hbm.at[pl.ds(0, TOP_FINE * tm)], ybuf.at[slot], sem_y.at[slot]).wait()
    gate = gate_ref[...]
    y = ybuf[slot]
    moe = gate[:, 0:1] * y[0:tm] + gate[:, RT_HALF:RT_HALF + 1] * y[tm:2 * tm]

    @pl.when(i < na_t)
    def _():
        oa_ref[...] = _layer_norm(DN_ALPHA * xa_ref[...] + moe, g_ref[...], b_ref[...])

    @pl.when(i >= na_t)
    def _():
        ob_ref[...] = _layer_norm(DN_ALPHA * xb_ref[...] + moe, g_ref[...], b_ref[...])


def _combine_ln(xa, xb, ys, pos_tab, gate, g, b, *, tm):
    d = xa.shape[1]
    na, nb = xa.shape[0], xb.shape[0]
    assert TOP_FINE == 2 and nb == tm
    spec_a, spec_b, na_t = _two_source_specs(na, nb, tm, d)
    any_spec = pl.BlockSpec(memory_space=pl.ANY)
    return pl.pallas_call(
        functools.partial(_combine_ln_kernel, na_t=na_t),
        grid=((na + nb) // tm,),
        in_specs=[spec_a, spec_b, pl.BlockSpec((tm, V7X_LANES), lambda i: (i, 0)), _const_spec((1, d)),
                  _const_spec((1, d)), any_spec, any_spec],
        out_specs=[spec_a, spec_b],
        out_shape=[jax.ShapeDtypeStruct((na, d), F32), jax.ShapeDtypeStruct((nb, d), F32)],
        scratch_shapes=[pltpu.SMEM((2, 1, TOP_FINE * tm), jnp.int32), pltpu.VMEM((2, TOP_FINE * tm, d), F32),
                        pltpu.SemaphoreType.DMA((2,)), pltpu.SemaphoreType.DMA((2,))],
        compiler_params=_params(1),
        name="moe_combine_ln",
    )(xa, xb, gate, g, b, pos_tab, ys)


def _moe_plan(ids, rank, cnt, n, tm):
    counts = cnt[0, RT_FINE0:RT_FINE0 + N_EXPERTS].astype(jnp.int32)
    tiles_per = (counts + MOE_TILE - 1) // MOE_TILE
    e_idx = jnp.arange(N_EXPERTS)
    tile_end = tiles_per @ (e_idx[:, None] <= e_idx[None, :]).astype(jnp.int32)
    tile_beg = tile_end - tiles_per
    n_tiles = -(-TOP_FINE * n // MOE_TILE) + N_EXPERTS
    ti = jnp.arange(n_tiles)
    te = jnp.minimum(jnp.sum((tile_end[None, :] <= ti[:, None]).astype(jnp.int32), axis=1), N_EXPERTS - 1)
    is_last = jnp.any((ti[:, None] == tile_end[None, :] - 1) & (tiles_per[None, :] > 0), axis=1)
    zero_flag = (is_last | (ti >= tile_end[-1])).astype(jnp.int32)
    pos = [tile_beg[ids[:, c].reshape(n // tm, tm)] * MOE_TILE + rank[:, c].reshape(n // tm, tm)
           for c in (0, RT_HALF)]
    pos_tab = jnp.concatenate(pos, axis=1).astype(jnp.int32)
    return te.astype(jnp.int32), zero_flag, pos_tab, n_tiles


def _moe_prepare(w_coarse, b_coarse, w_fine, b_fine):
    d = w_coarse.shape[0]
    wf = jnp.transpose(w_fine, (1, 0, 2)).reshape(d, N_EXPERTS)
    w = jnp.zeros((d, V7X_LANES), F32).at[:, :N_GROUPS].set(w_coarse).at[:, RT_FINE0:RT_FINE0 + N_EXPERTS].set(wf)
    b = jnp.zeros((1, V7X_LANES), F32).at[0, :N_GROUPS].set(b_coarse)
    b = b.at[0, RT_FINE0:RT_FINE0 + N_EXPERTS].set(b_fine.reshape(-1))
    return _split_w(w), b


def _moe_layer(xa, xb, w_coarse, b_coarse, w_fine, b_fine, w_gate_up, w_down, g, b, *, tm, three_pass):
    d = xa.shape[1]
    n = xa.shape[0] + xb.shape[0]
    (wrh, wrl), br = _moe_prepare(w_coarse, b_coarse, w_fine, b_fine)
    ids, rank, gate, cnt = _router(xa, xb, wrh, wrl, br, tm=tm)
    te, zero_flag, pos_tab, n_tiles = _moe_plan(ids, rank, cnt, n, tm)
    xs = _dispatch(xa, xb, pos_tab, zero_flag, n_tiles=n_tiles, tm=tm)
    ys = _experts(xs, w_gate_up.reshape(N_EXPERTS, d, 2 * D_EXPERT), w_down.reshape(N_EXPERTS, D_EXPERT, d), te,
                  three_pass=three_pass)
    return _combine_ln(xa, xb, ys, pos_tab, gate, g, b, tm=tm)


def _cdproj_kernel(x_ref, wh_ref, wl_ref, cos_ref, sin_ref, perm_ref, u_ref, q_ref, k_ref, v_ref):
    xh, xl = _split2(x_ref[...])

    def col(j):
        cs = slice(j * D_C, (j + 1) * D_C)
        return _dot_hl(xh, xl, wh_ref[:, cs], wl_ref[:, cs])

    u_ref[...] = col(0)
    for j, ref in ((1, q_ref), (2, k_ref)):
        t = col(j)
        ref[...] = t * cos_ref[...] + _dotx(t, perm_ref[...]) * sin_ref[...]
    v_ref[...] = col(3)


def _cdproj(x, wh, wl, cos_t, sin_t, perm, *, n_rows, tm):
    k = x.shape[1]
    assert D_C == D_D and n_rows % tm == 0 and cos_t.shape[0] % tm == 0
    nt = cos_t.shape[0] // tm
    row = pl.BlockSpec((tm, D_D), lambda i: (i, 0))
    tab = pl.BlockSpec((tm, D_D), lambda i: (i % nt, 0))
    return pl.pallas_call(
        _cdproj_kernel,
        grid=(n_rows // tm,),
        in_specs=[pl.BlockSpec((tm, k), lambda i: (i, 0)), _const_spec(wh.shape), _const_spec(wl.shape),
                  tab, tab, _const_spec(perm.shape)],
        out_specs=[row] * 4,
        out_shape=[jax.ShapeDtypeStruct((n_rows, D_D), F32)] * 4,
        compiler_params=_params(1),
        name="cd_proj_rope",
    )(x, wh, wl, cos_t, sin_t, perm)


def _rope_tables(pos):
    half = ROT_DIM // 2
    inv = ROPE_THETA ** (-jnp.arange(half, dtype=F32) / half)
    ang = pos.astype(F32)[:, None] * inv
    cos, sin = jnp.cos(ang), jnp.sin(ang)
    n = pos.shape[0]
    rest = HD_D - ROT_DIM
    cos_h = jnp.concatenate([cos, cos, jnp.ones((n, rest), F32)], axis=-1)
    sin_h = jnp.concatenate([-sin, sin, jnp.zeros((n, rest), F32)], axis=-1)
    perm = np.zeros((D_D, D_D), np.float32)
    for dst in range(D_D):
        j = dst % HD_D
        if j < half:
            perm[dst + half, dst] = 1.0
        elif j < ROT_DIM:
            perm[dst - half, dst] = 1.0
    return jnp.tile(cos_h, (1, H_D)), jnp.tile(sin_h, (1, H_D)), jnp.asarray(perm, BF16)


POOL_HDR = 16


def _pool_kernel(u_ref, b0_ref, wph_ref, wpl_ref, sc_ref, o_ref, bo_ref, ext_sc, *, tc, valid, start):
    t = pl.program_id(1)
    nt = pl.num_programs(1)

    @pl.when(t == 0)
    def _():
        ext_sc[0:POOL_HDR, :] = b0_ref[0]

    ext_sc[POOL_HDR:POOL_HDR + tc, :] = u_ref[...]
    u = u_ref[...]
    rowi = _iota((tc, POOL_HDR + tc), 0) + POOL_HDR
    colj = _iota((tc, POOL_HDR + tc), 1)
    pos = (start + t * tc + _iota((tc, POOL_GROUP), 0)).astype(F32)
    for gi, w in enumerate(POOL_WINDOWS):
        ls = slice(gi * POOL_GROUP, (gi + 1) * POOL_GROUP)
        band = jnp.where((colj <= rowi) & (colj > rowi - w), 1.0, 0.0).astype(BF16)
        win = _xdot(band, ext_sc[:, ls])
        cnt = jnp.minimum(float(w), pos + 1.0)
        dh, dl = _split2(win / cnt - u[:, ls])
        o_ref[:, ls] = _dot_hl(dh, dl, wph_ref[gi], wpl_ref[gi]) * sc_ref[:, ls]

    @pl.when(t == nt - 1)
    def _():
        bo_ref[0] = ext_sc[valid:valid + POOL_HDR, :]

    ext_sc[0:POOL_HDR, :] = ext_sc[tc:tc + POOL_HDR, :]


def _pool(u, buf, wph, wpl, scale, *, nb, t_len, tc, valid, start):
    nt = t_len // tc
    assert t_len % tc == 0 and (valid == tc or nt == 1)
    b0 = jnp.zeros((nb, POOL_HDR, D_C), F32).at[:, POOL_HDR - POOL_BUF:].set(buf)
    row = pl.BlockSpec((tc, D_C), lambda b, t: (b * nt + t, 0))
    st = pl.BlockSpec((1, POOL_HDR, D_C), lambda b, t: (b, 0, 0))
    pooled, bnew = pl.pallas_call(
        functools.partial(_pool_kernel, tc=tc, valid=valid, start=start),
        grid=(nb, nt),
        in_specs=[row, st, _const_spec(wph.shape), _const_spec(wpl.shape), _const_spec(scale.shape)],
        out_specs=[row, st],
        out_shape=[jax.ShapeDtypeStruct((nb * t_len, D_C), F32), jax.ShapeDtypeStruct((nb, POOL_HDR, D_C), F32)],
        scratch_shapes=[pltpu.VMEM((POOL_HDR + tc, D_C), F32)],
        compiler_params=_params(2),
        name="pool_mix",
    )(u, b0, wph, wpl, scale)
    return pooled, bnew[:, POOL_HDR - POOL_BUF:]


def _top_blocks(gate, lane_f, valid, n_pick):
    sel = jnp.zeros(gate.shape, jnp.bool_)
    g = jnp.where(valid, gate, -jnp.inf)
    picks = []
    for _ in range(n_pick):
        v = jnp.max(g, axis=-1, keepdims=True)
        idx = jnp.min(jnp.where(g == v, lane_f, float(V7X_LANES)), axis=-1, keepdims=True)
        hit = (lane_f == idx) & valid
        sel = sel | hit
        g = jnp.where(hit, -jnp.inf, g)
        picks.append(idx)
    return sel, picks


def _cat_pieces(a, b, c):
    return jnp.concatenate([a, b, c, jnp.zeros_like(a)], axis=1)


MOBA_TRIP = 4


def _top_rows(gate, row_f, valid, n_pick):
    sel = jnp.zeros(gate.shape, F32)
    g = jnp.where(valid, gate, -jnp.inf)
    for _ in range(n_pick):
        v = jnp.max(g, axis=0, keepdims=True)
        idx = jnp.min(jnp.where(g == v, row_f, float(V7X_LANES)), axis=0, keepdims=True)
        hit = (row_f == idx) & valid
        sel = jnp.where(hit, 1.0, sel)
        g = jnp.where(hit, -jnp.inf, g)
    return sel


def _moba_prompt_kernel(q_ref, k_ref, v_ref, o_ref, kmean_sc, kcat_sc, vt_sc, sel_sc):
    i = pl.program_id(2)
    blk = MOBA_BLOCK
    t_len = k_ref.shape[0]
    n_heads = V7X_LANES // HD_D
    g_rows = sel_sc.shape[1]
    scale = HD_D ** -0.5

    @pl.when(i == 0)
    def _():
        ind = jnp.where((_iota((V7X_LANES, t_len), 1) >> 8) == _iota((V7X_LANES, t_len), 0), 1.0, 0.0)
        kmean_sc[...] = _xdot(ind.astype(BF16), k_ref[...]) * (1.0 / blk)
        ones_row = jnp.where(_iota((HD_D, blk), 0) == 0, 1.0, 0.0).astype(BF16)
        for n in range(t_len // blk):
            rows = slice(n * blk, (n + 1) * blk)
            kh, kl = _split2(k_ref[rows, :])
            vt = jnp.transpose(v_ref[rows, :]).astype(BF16)
            for hh in range(n_heads):
                hs = slice(hh * HD_D, (hh + 1) * HD_D)
                kcat_sc[hh, rows, :] = _cat_pieces(kh[:, hs], kh[:, hs], kl[:, hs])
                vt_sc[hh, n, 0:HD_D, :] = vt[hs, :]
                vt_sc[hh, n, HD_D:2 * HD_D, :] = ones_row

    def scores_t(hh, qcat, j):
        return _mm(kcat_sc[hh, pl.ds(pl.multiple_of(j * blk, blk), blk), :], qcat, NT)

    def weighted_t(hh, p, j):
        return _mm(vt_sc[hh, j], p.astype(BF16))

    q = q_ref[...]
    key_le_query = _iota((blk, blk), 0) <= _iota((blk, blk), 1)
    brow = _iota((g_rows, blk), 0)
    qcats, state = [], []
    for hh in range(n_heads):
        hs = slice(hh * HD_D, (hh + 1) * HD_D)
        qh = q[:, hs]
        gate_t = _dot3_cat(kmean_sc[0:g_rows, hs], qh, NT)
        sel_sc[hh] = _top_rows(gate_t, brow.astype(F32), brow < i, MOBA_TOPK)
        q_hi, q_lo = _split2(qh * scale)
        qcats.append(_cat_pieces(q_hi, q_lo, q_hi))
        s = jnp.where(key_le_query, scores_t(hh, qcats[hh], i), -jnp.inf)
        m = jnp.max(s, axis=0, keepdims=True)
        state += [m, weighted_t(hh, jnp.exp(s - m), i)]

    def body(jj, carry):
        out = []
        for hh in range(n_heads):
            m, acc = carry[2 * hh:2 * hh + 2]
            js = [jnp.minimum(MOBA_TRIP * jj + t, i) for t in range(MOBA_TRIP)]
            ss = []
            for j in js:
                on = sel_sc[hh, pl.ds(j, 1), :] > 0.0
                ss.append(jnp.where(on, scores_t(hh, qcats[hh], j), -jnp.inf))
            top = functools.reduce(jnp.maximum, ss)
            m2 = jnp.maximum(m, jnp.max(top, axis=0, keepdims=True))
            new = functools.reduce(lambda x, y: x + y,
                                   [weighted_t(hh, jnp.exp(sj - m2), j) for sj, j in zip(ss, js)])
            out += [m2, jnp.exp(m - m2) * acc + new]
        return tuple(out)

    state = lax.fori_loop(0, (i + MOBA_TRIP - 1) >> (MOBA_TRIP.bit_length() - 1), body, tuple(state))
    for hh in range(n_heads):
        acc = jnp.transpose(state[2 * hh + 1])
        o_ref[:, hh * HD_D:(hh + 1) * HD_D] = acc[:, 0:HD_D] / acc[:, HD_D:HD_D + 1]


def _moba_prompt(q, k, v, *, nb, t_len):
    blk = MOBA_BLOCK
    nq = t_len // blk
    assert t_len % blk == 0 and nq <= V7X_LANES
    n_heads = V7X_LANES // HD_D
    g_rows = -(-nq // V7X_SUBLANES) * V7X_SUBLANES
    qspec = pl.BlockSpec((blk, V7X_LANES), lambda b, hp, i: (b * nq + i, hp))
    kspec = pl.BlockSpec((t_len, V7X_LANES), lambda b, hp, i: (b, hp))
    return pl.pallas_call(
        _moba_prompt_kernel,
        grid=(nb, D_D // V7X_LANES, nq),
        in_specs=[qspec, kspec, kspec],
        out_specs=qspec,
        out_shape=jax.ShapeDtypeStruct((nb * t_len, D_D), F32),
        scratch_shapes=[pltpu.VMEM((V7X_LANES, V7X_LANES), F32),
                        pltpu.VMEM((n_heads, t_len, 4 * HD_D), BF16),
                        pltpu.VMEM((n_heads, nq, 2 * HD_D, blk), BF16),
                        pltpu.VMEM((n_heads, g_rows, blk), F32)],
        compiler_params=_params(3),
        name="moba_prompt",
    )(q, k, v)


KM_PAGES = 16
PAGES_PER_BLOCK = MOBA_BLOCK // PAGE_SIZE


def _kmean_kernel(pt_ref, *refs):
    pages, o_ref = refs[:KM_PAGES], refs[KM_PAGES]
    c = pl.program_id(1)
    blocks_per_step = KM_PAGES // PAGES_PER_BLOCK

    @pl.when(c == 0)
    def _():
        o_ref[...] = jnp.zeros(o_ref.shape, F32)

    acc = o_ref[0]
    lane = _iota(acc.shape, 2)
    for blk in range(blocks_per_step):
        tot = pages[blk * PAGES_PER_BLOCK][0]
        for p in range(1, PAGES_PER_BLOCK):
            tot = tot + pages[blk * PAGES_PER_BLOCK + p][0]
        mean = jnp.sum(tot, axis=-1, keepdims=True) * (1.0 / MOBA_BLOCK)
        acc = jnp.where(lane == c * blocks_per_step + blk, mean, acc)
    o_ref[0] = acc


def _block_means(cache_kt, page_table):
    db, n_pages = page_table.shape
    n_blocks = n_pages // PAGES_PER_BLOCK
    assert n_pages % KM_PAGES == 0
    page_spec = lambda j: pl.BlockSpec((1, H_D, HD_D, PAGE_SIZE), lambda b, c, pt: (pt[b, c * KM_PAGES + j], 0, 0, 0))
    grid_spec = pltpu.PrefetchScalarGridSpec(
        num_scalar_prefetch=1,
        grid=(db, n_pages // KM_PAGES),
        in_specs=[page_spec(j) for j in range(KM_PAGES)],
        out_specs=pl.BlockSpec((1, H_D, HD_D, n_blocks), lambda b, c, pt: (b, 0, 0, 0)),
    )
    return pl.pallas_call(
        _kmean_kernel,
        grid_spec=grid_spec,
        out_shape=jax.ShapeDtypeStruct((db, H_D, HD_D, n_blocks), F32),
        compiler_params=_params(2),
        name="moba_block_means",
    )(page_table, *([cache_kt] * KM_PAGES))


def _select_kernel(q_ref, km_ref, o_ref):
    q = q_ref[...]
    n_blk = km_ref.shape[3]
    lane_f = _iota((q.shape[0], n_blk), 1).astype(F32)
    out_lane = _iota((q.shape[0], V7X_LANES), 1)
    for h in range(H_D):
        gate = _dot3(q[:, h * HD_D:(h + 1) * HD_D], km_ref[0, h])
        _, picks = _top_blocks(gate, lane_f, lane_f >= 0.0, MOBA_TOPK)
        res = jnp.zeros((q.shape[0], V7X_LANES), F32)
        for r, idx in enumerate(picks):
            res = jnp.where(out_lane == r, idx, res)
        o_ref[0, h] = res.astype(jnp.int32)


def _select_blocks(q, kmean_t):
    db = kmean_t.shape[0]
    return pl.pallas_call(
        _select_kernel,
        grid=(db,),
        in_specs=[pl.BlockSpec((SAMPLE_PAD, D_D), lambda b: (b, 0)),
                  pl.BlockSpec((1,) + kmean_t.shape[1:], lambda b: (b, 0, 0, 0))],
        out_specs=pl.BlockSpec((1, H_D, SAMPLE_PAD, V7X_LANES), lambda b: (b, 0, 0, 0)),
        out_shape=jax.ShapeDtypeStruct((db, H_D, SAMPLE_PAD, V7X_LANES), jnp.int32),
        compiler_params=_params(1),
        name="moba_select",
    )(q, kmean_t)


HEADS_PER_STEP = V7X_LANES // HD_D


def _attend_kernel(phys_ref, q_ref, kn_ref, vn_ref, ck_hbm, cv_hbm, o_ref, kbuf, vbuf, sem, *, n_q):
    b = pl.program_id(0)
    hp = pl.program_id(1)
    n_hp = pl.num_programs(1)
    step = b * n_hp + hp
    slot = lax.rem(step, 2)
    scale = HD_D ** -0.5

    def fetch(fb, fhp, fslot):
        for hh in range(HEADS_PER_STEP):
            h = fhp * HEADS_PER_STEP + hh
            for s in range(n_q):
                for r in range(MOBA_TOPK):
                    for pg in range(PAGES_PER_BLOCK):
                        flat = (((fb * H_D + h) * n_q + s) * MOBA_TOPK + r) * PAGES_PER_BLOCK + pg
                        cols = pl.ds((r * PAGES_PER_BLOCK + pg) * PAGE_SIZE, PAGE_SIZE)
                        page = phys_ref[flat]
                        for which, (src, dst) in enumerate(((ck_hbm, kbuf), (cv_hbm, vbuf))):
                            pltpu.make_async_copy(src.at[page, h], dst.at[fslot, hh, s, :, cols],
                                                  sem.at[fslot, which]).start()

    @pl.when(step == 0)
    def _():
        fetch(b, hp, slot)

    @pl.when(step + 1 < pl.num_programs(0) * n_hp)
    def _():
        wrap = hp + 1 == n_hp
        fetch(jnp.where(wrap, b + 1, b), jnp.where(wrap, 0, hp + 1), 1 - slot)

    pltpu.make_async_copy(vbuf.at[1 - slot], kbuf.at[slot], sem.at[slot, 0]).wait()
    pltpu.make_async_copy(kbuf.at[1 - slot], vbuf.at[slot], sem.at[slot, 1]).wait()

    q = q_ref[...]
    rows_n = q.shape[0]
    rowi = _iota((rows_n, rows_n), 0)
    coli = _iota((rows_n, rows_n), 1)
    own_ok = (coli <= rowi) & (coli < n_q)
    rsel = _iota((rows_n, HD_D), 0)
    for hh in range(HEADS_PER_STEP):
        hs = slice(hh * HD_D, (hh + 1) * HD_D)
        qh = q[:, hs]
        s_own = jnp.where(own_ok, _dot3(qh, kn_ref[...][:, hs], NT) * scale, -jnp.inf)
        m_own = jnp.max(s_own, axis=-1, keepdims=True)
        out = jnp.zeros((rows_n, HD_D), F32)
        for s in range(n_q):
            s_sel = _dot3_cat(qh, kbuf[slot, hh, s]) * scale
            m = jnp.maximum(m_own, jnp.max(s_sel, axis=-1, keepdims=True))
            p_sel = jnp.exp(s_sel - m)
            p_own = jnp.exp(s_own - m)
            l = jnp.sum(p_sel, axis=-1, keepdims=True) + jnp.sum(p_own, axis=-1, keepdims=True)
            o_s = (_dot3(p_sel, vbuf[slot, hh, s], NT) + _dot3(p_own, vn_ref[...][:, hs])) / l
            out = jnp.where(rsel == s, o_s, out)
        o_ref[:, hs] = out


def _moba_sample_attend(phys, q, k_new, v_new, cache_kt, cache_vt, *, db, n_q):
    row = pl.BlockSpec((SAMPLE_PAD, V7X_LANES), lambda b, hp, ph: (b, hp))
    any_spec = pl.BlockSpec(memory_space=pl.ANY)
    buf = pltpu.VMEM((2, HEADS_PER_STEP, n_q, HD_D, MOBA_TOPK * MOBA_BLOCK), F32)
    grid_spec = pltpu.PrefetchScalarGridSpec(
        num_scalar_prefetch=1,
        grid=(db, D_D // V7X_LANES),
        in_specs=[row, row, row, any_spec, any_spec],
        out_specs=row,
        scratch_shapes=[buf, buf, pltpu.SemaphoreType.DMA((2, 2))],
    )
    return pl.pallas_call(
        functools.partial(_attend_kernel, n_q=n_q),
        grid_spec=grid_spec,
        out_shape=jax.ShapeDtypeStruct((db * SAMPLE_PAD, D_D), F32),
        compiler_params=_params(2),
        name="moba_sample_attend",
    )(phys, q, k_new, v_new, cache_kt, cache_vt)


def _outproj2_ln_kernel(a1_ref, a2_ref, x_ref, w1h_ref, w1l_ref, w2h_ref, w2l_ref, g_ref, b_ref, o_ref):
    y = _dot_hl(*_split2(a1_ref[...]), w1h_ref[...], w1l_ref[...])
    y = y + _dot_hl(*_split2(a2_ref[...]), w2h_ref[...], w2l_ref[...])
    o_ref[...] = _layer_norm(DN_ALPHA * x_ref[...] + y, g_ref[...], b_ref[...])


def _outproj2_ln(a1, a2, x, w1, w2, g, b, *, n_rows, tm):
    d = x.shape[1]
    assert n_rows % tm == 0
    row = lambda c: pl.BlockSpec((tm, c), lambda i: (i, 0))
    consts = [w1[0], w1[1], w2[0], w2[1], g, b]
    return pl.pallas_call(
        _outproj2_ln_kernel,
        grid=(n_rows // tm,),
        in_specs=[row(a1.shape[1]), row(a2.shape[1]), row(d)] + [_const_spec(c.shape) for c in consts],
        out_specs=row(d),
        out_shape=jax.ShapeDtypeStruct((n_rows, d), F32),
        compiler_params=_params(1),
        name="outproj2_ln",
    )(a1, a2, x, *consts)


SAMPLE_PAD = 16


def _pad_rows(x, t_pad):
    nb, t, d = x.shape
    return jnp.zeros((nb, t_pad, d), x.dtype).at[:, :t].set(x).reshape(nb * t_pad, d)


def kernel(x_prompt, x_sample, state_gla, state_ssm, state_conv, state_pool, cache_k, cache_v, page_table, ab_w_in, ab_w_gate2, ab_b_gate, ab_gla_norm, ab_conv_w, ab_conv_b, ab_dt_bias, ab_a_log, ab_d_skip, ab_ssm_norm, ab_w_out, cd_w_in, cd_w_pool, cd_pool_scale, cd_w_out, moe_w_coarse, moe_b_coarse, moe_w_fine, moe_b_fine, moe_w_gate_up, moe_w_down, ln_g, ln_b):
    bp, tp, d = x_prompt.shape
    bs, ts, _ = x_sample.shape
    (wih, wil), ab_prm, (woh, wol) = _ab_prepare(ab_w_in, ab_w_gate2, ab_b_gate, ab_gla_norm, ab_conv_w,
                                                 ab_conv_b, ab_dt_bias, ab_a_log, ab_d_skip, ab_ssm_norm, ab_w_out)
    n_p, n_s = bp * tp, bs * ts
    n_pages = page_table.shape[1]
    past = n_pages * PAGE_SIZE
    assert past % MOBA_BLOCK == 0 and past // MOBA_BLOCK >= MOBA_TOPK and ts <= SAMPLE_PAD
    zeros = lambda *s: jnp.zeros(s, F32)
    ln = lambda l, j: (ln_g[l, j].reshape(1, d), ln_b[l, j].reshape(1, d))
    unpad = lambda rows: rows.reshape(bs, SAMPLE_PAD, -1)[:, :ts]
    assert n_s == V7X_LANES
    moe = lambda l, xa, xb: _moe_layer(xa, unpad(xb).reshape(n_s, d), moe_w_coarse[l], moe_b_coarse[l],
                                       moe_w_fine[l], moe_b_fine[l], moe_w_gate_up[l], moe_w_down[l], *ln(l, 1),
                                       tm=V7X_LANES,
                                       three_pass=l < DEPTH - 1)

    xp = x_prompt.reshape(n_p, d)
    xs = _pad_rows(x_sample, SAMPLE_PAD)
    proj_p = _proj(xp, wih, wil)
    oy_p, gla_p, ssm_p, conv_p = _scan_from_proj(
        proj_p, zeros(bp, H_A, DK_A, DV_A), zeros(bp, H_B, P_B, N_B), zeros(bp, CONV_W - 1, CONV_DIM), ab_prm,
        nb=bp, t_len=tp, tc=256, chunk=CHUNK, valid=256)
    xp = _outproj_ln(oy_p, xp, woh, wol, *ln(0, 0))
    proj_s = _proj(xs, wih, wil)
    oy_s, gla_s, ssm_s, conv_s = _scan_from_proj(
        proj_s, state_gla, state_ssm, state_conv, ab_prm,
        nb=bs, t_len=SAMPLE_PAD, tc=SAMPLE_PAD, chunk=SAMPLE_PAD, valid=ts)
    xs = _outproj_ln(oy_s, xs, woh, wol, *ln(0, 0))
    xp, xs = moe(0, xp, xs)

    wch, wcl = _split_w(cd_w_in)
    wph, wpl = _split_w(cd_w_pool)
    pscale = cd_pool_scale.reshape(1, D_C)
    wo_c, wo_d = _split_w(cd_w_out[:D_C]), _split_w(cd_w_out[D_C:])
    cos_p, sin_p, perm = _rope_tables(jnp.arange(tp))
    u, q, k, v = _cdproj(xp, wch, wcl, cos_p, sin_p, perm, n_rows=n_p, tm=256)
    pooled, pool_p = _pool(u, zeros(bp, POOL_BUF, D_C), wph, wpl, pscale, nb=bp, t_len=tp, tc=256, valid=256, start=0)
    att = _moba_prompt(q, k, v, nb=bp, t_len=tp)
    xp = _outproj2_ln(pooled, att, xp, wo_c, wo_d, *ln(1, 0), n_rows=n_p, tm=256)
    k_p, v_p = k.reshape(bp, tp, H_D, HD_D), v.reshape(bp, tp, H_D, HD_D)

    xs = _pad_rows(xs.reshape(bs, ts, d), SAMPLE_PAD)
    cos_s, sin_s, _ = _rope_tables(past + jnp.arange(SAMPLE_PAD))
    n_rows_s = bs * SAMPLE_PAD
    u, q, k, v = _cdproj(xs, wch, wcl, jnp.tile(cos_s, (bs, 1)), jnp.tile(sin_s, (bs, 1)), perm,
                         n_rows=n_rows_s, tm=256)
    pooled, pool_s = _pool(u, state_pool, wph, wpl, pscale, nb=bs, t_len=SAMPLE_PAD, tc=SAMPLE_PAD, valid=ts,
                           start=past)
    cache_kt, cache_vt = jnp.transpose(cache_k, (0, 2, 3, 1)), jnp.transpose(cache_v, (0, 2, 3, 1))
    picks = _select_blocks(q, _block_means(cache_kt, page_table))[:, :, :ts, :MOBA_TOPK]
    logical = picks[..., None] * PAGES_PER_BLOCK + jnp.arange(PAGES_PER_BLOCK)
    phys = page_table[jnp.arange(bs)[:, None, None, None, None], logical].reshape(-1)
    att = _moba_sample_attend(phys, q, k, v, cache_kt, cache_vt, db=bs, n_q=ts)
    xs = _outproj2_ln(pooled, att, xs, wo_c, wo_d, *ln(1, 0), n_rows=n_rows_s, tm=256)
    k_s, v_s = unpad(k).reshape(bs, ts, H_D, HD_D), unpad(v).reshape(bs, ts, H_D, HD_D)
    xp, xs = moe(1, xp, xs)

    return (xp.reshape(bp, tp, d), xs.reshape(bs, ts, d), gla_p, ssm_p, conv_p, pool_p, k_p, v_p,
            gla_s, ssm_s, conv_s, pool_s, k_s, v_s)
```

```python
import functools

import jax
import jax.numpy as jnp
import numpy as np
from jax import lax
from jax.experimental import pallas as pl
from jax.experimental.pallas import tpu as pltpu

F32 = jnp.float32
BF16 = jnp.bfloat16

D_MODEL = 1024
DEPTH = 2
PAGE_SIZE = 128
H_A, DK_A, DV_A = 4, 64, 128
GATE_RANK = 16
GATE_TAU = 16.0
CHUNK = 64
H_B, P_B, N_B, G_B = 8, 64, 64, 2
CONV_W = 4
D_B = H_B * P_B
CONV_DIM = D_B + 2 * G_B * N_B
POOL_WINDOWS = (2, 4, 8, 16)
POOL_GROUP = 128
D_C = len(POOL_WINDOWS) * POOL_GROUP
POOL_BUF = max(POOL_WINDOWS) - 1
H_D, HD_D = 8, 64
D_D = H_D * HD_D
MOBA_BLOCK = 256
MOBA_TOPK = 3
ROT_DIM = HD_D // 4
ROPE_THETA = 500000.0
N_GROUPS, EXP_PER_GROUP, TOP_FINE, D_EXPERT = 4, 8, 2, 256
N_EXPERTS = N_GROUPS * EXP_PER_GROUP
DN_ALPHA = (2 * DEPTH) ** 0.25
LN_EPS = 1e-5
RMS_EPS = 1e-6

V7X_LANES = 128
V7X_SUBLANES = 8
V7X_VMEM_LIMIT_BYTES = 56 * 1024 * 1024

NN = (((1,), (0,)), ((), ()))
NT = (((1,), (1,)), ((), ()))
TN = (((0,), (0,)), ((), ()))


def _params(n_axes):
    return pltpu.CompilerParams(dimension_semantics=("arbitrary",) * n_axes,
                                vmem_limit_bytes=V7X_VMEM_LIMIT_BYTES)


def _const_spec(shape):
    zeros = (0,) * len(shape)
    return pl.BlockSpec(shape, lambda *_: zeros, pipeline_mode=pl.Buffered(1))


def _split2(a):
    hi = a.astype(BF16)
    lo = (a - hi.astype(F32)).astype(BF16)
    return hi, lo


def _split3(a):
    hi = a.astype(BF16)
    r = a - hi.astype(F32)
    mid = r.astype(BF16)
    lo = (r - mid.astype(F32)).astype(BF16)
    return hi, mid, lo


def _mm(a, b, dims=NN):
    return lax.dot_general(a, b, dims, preferred_element_type=F32)


def _dot_hl(ah, al, bh, bl, dims=NN):
    return (_mm(ah, bl, dims) + _mm(al, bh, dims)) + _mm(ah, bh, dims)


def _dot3(a, b, dims=NN):
    ah, al = _split2(a)
    bh, bl = _split2(b)
    return _dot_hl(ah, al, bh, bl, dims)


def _dot3_cat(a, b, dims=NN):
    (ca,), (cb,) = dims[0]
    ah, al = _split2(a)
    bh, bl = _split2(b)
    return _mm(jnp.concatenate([ah, al, ah], axis=ca), jnp.concatenate([bh, bh, bl], axis=cb), dims)


def _dotx(a, e, dims=NN):
    h, m, l = _split3(a)
    return (_mm(l, e, dims) + _mm(m, e, dims)) + _mm(h, e, dims)


def _xdot(e, a, dims=NN):
    h, m, l = _split3(a)
    return (_mm(e, l, dims) + _mm(e, m, dims)) + _mm(e, h, dims)


def _silu(x):
    return x / (1.0 + jnp.exp(-x))


def _softplus(x):
    return jnp.maximum(x, 0.0) + jnp.log1p(jnp.exp(-jnp.abs(x)))


def _log_sigmoid(x):
    return jnp.minimum(x, 0.0) - jnp.log1p(jnp.exp(-jnp.abs(x)))


def _layer_norm(z, g, b):
    mu = jnp.mean(z, axis=-1, keepdims=True)
    zc = z - mu
    var = jnp.mean(zc * zc, axis=-1, keepdims=True)
    return zc * lax.rsqrt(var + LN_EPS) * g + b


def _iota(shape, dim):
    return lax.broadcasted_iota(jnp.int32, shape, dim)


def _proj_kernel(x_ref, wh_ref, wl_ref, o_ref, *, col_chunk):
    xh, xl = _split2(x_ref[...])
    m = o_ref.shape[1]
    for j in range(0, m, col_chunk):
        cs = slice(j, min(j + col_chunk, m))
        o_ref[:, cs] = _dot_hl(xh, xl, wh_ref[:, cs], wl_ref[:, cs])


def _proj(x, wh, wl, *, n_rows=None, tm=256, col_chunk=512):
    n = x.shape[0] if n_rows is None else n_rows
    k = x.shape[1]
    m = wh.shape[1]
    assert n % tm == 0
    return pl.pallas_call(
        functools.partial(_proj_kernel, col_chunk=col_chunk),
        grid=(n // tm,),
        in_specs=[pl.BlockSpec((tm, k), lambda i: (i, 0)), _const_spec((k, m)), _const_spec((k, m))],
        out_specs=pl.BlockSpec((tm, m), lambda i: (i, 0)),
        out_shape=jax.ShapeDtypeStruct((n, m), F32),
        compiler_params=_params(1),
        name="proj",
    )(x, wh, wl)


def _outproj_ln_kernel(a_ref, x_ref, wh_ref, wl_ref, g_ref, b_ref, o_ref):
    ah, al = _split2(a_ref[...])
    y = _dot_hl(ah, al, wh_ref[...], wl_ref[...])
    o_ref[...] = _layer_norm(DN_ALPHA * x_ref[...] + y, g_ref[...], b_ref[...])


def _outproj_ln(a, x, wh, wl, g, b, *, tm=256):
    n, k = a.shape
    d = wh.shape[1]
    assert n % tm == 0
    return pl.pallas_call(
        _outproj_ln_kernel,
        grid=(n // tm,),
        in_specs=[pl.BlockSpec((tm, k), lambda i: (i, 0)), pl.BlockSpec((tm, d), lambda i: (i, 0)),
                  _const_spec((k, d)), _const_spec((k, d)), _const_spec((1, d)), _const_spec((1, d))],
        out_specs=pl.BlockSpec((tm, d), lambda i: (i, 0)),
        out_shape=jax.ShapeDtypeStruct((n, d), F32),
        compiler_params=_params(1),
        name="outproj_ln",
    )(a, x, wh, wl, g, b)


SM_GLR0 = 0
SM_DT0 = GATE_RANK
CONV_HDR = V7X_SUBLANES


def _scan_kernel(qkvr_ref, z_ref, xbc_ref, sm_ref, s0_ref, h0_ref, c0_ref,
                 wg2h_ref, wg2l_ref, bg_ref, gnorm_ref, cw_ref, cb_ref, dtb_ref, alog_ref,
                 dskip_ref, snorm_ref,
                 oy_ref, s_out_ref, h_out_ref, c_out_ref,
                 st_sc, h_sc, ext_sc, o_sc, y_sc, *, tc, chunk, valid):
    t = pl.program_id(1)
    nt = pl.num_programs(1)
    n_chunks = tc // chunk
    shift = chunk.bit_length() - 1
    assert (1 << shift) == chunk and tc % chunk == 0

    @pl.when(t == 0)
    def _():
        st_sc[...] = s0_ref[0]
        h_sc[...] = h0_ref[0]
        ext_sc[0:CONV_HDR, :] = c0_ref[0]

    row = _iota((tc, tc), 0)
    col = _iota((tc, tc), 1)
    causal = ((row >> shift) == (col >> shift)) & (col <= row)
    l_tri = jnp.where(causal, 1.0, 0.0).astype(BF16)
    lane = _iota((tc, V7X_LANES), 1)
    rowv = _iota((tc, V7X_LANES), 0)
    dt_lane = (lane >= SM_DT0) & (lane < SM_DT0 + H_B)
    if valid < tc:
        row_ok = rowv < valid
        dt_lane = dt_lane & row_ok
    e8 = jnp.where((_iota((V7X_LANES, D_B), 1) >> 6) + SM_DT0 == _iota((V7X_LANES, D_B), 0), 1.0, 0.0).astype(BF16)

    sm = sm_ref[...]
    pre = _dot_hl(*_split2(sm), wg2h_ref[...], wg2l_ref[...])
    log_a = _log_sigmoid(pre + bg_ref[...]) / GATE_TAU
    if valid < tc:
        log_a = jnp.where(_iota((tc, H_A * DK_A), 0) < valid, log_a, 0.0)
    b_cum = _xdot(l_tri, log_a)
    dt_pad = jnp.where(dt_lane, _softplus(sm + dtb_ref[...]), 0.0)
    a_pad = -jnp.exp(alog_ref[...])
    la_pad = jnp.where(dt_lane, dt_pad * a_pad, 0.0)
    cum_pad = _xdot(l_tri, la_pad)
    dt_rep = _dotx(dt_pad, e8)
    cum_rep = _dotx(cum_pad, e8)

    ext_sc[CONV_HDR:CONV_HDR + tc, :] = xbc_ref[...]
    cw = cw_ref[...]
    acc = ext_sc[CONV_HDR - 3:CONV_HDR - 3 + tc, :] * cw[0:1, :]
    for i in range(1, CONV_W):
        acc = acc + ext_sc[CONV_HDR - 3 + i:CONV_HDR - 3 + i + tc, :] * cw[i:i + 1, :]
    xc = _silu(acc + cb_ref[...])

    @pl.when(t == nt - 1)
    def _():
        c_out_ref[0] = ext_sc[valid:valid + CONV_HDR, :]

    ext_sc[0:CONV_HDR, :] = ext_sc[tc:tc + CONV_HDR, :]

    xs = xc[:, 0:D_B]
    bm = xc[:, D_B:D_B + G_B * N_B]
    cm = xc[:, D_B + G_B * N_B:CONV_DIM]
    xdt = xs * dt_rep

    qkvr = qkvr_ref[...]
    q = qkvr[:, 0:H_A * DK_A] * (DK_A ** -0.5)
    k = qkvr[:, H_A * DK_A:2 * H_A * DK_A]
    v = qkvr[:, 2 * H_A * DK_A:2 * H_A * DK_A + H_A * DV_A]
    r = qkvr[:, 2 * H_A * DK_A + H_A * DV_A:]
    q_dec = q * jnp.exp(b_cum)
    k_dec = k * jnp.exp(-b_cum)
    for h in range(H_A):
        ks = slice(h * DK_A, (h + 1) * DK_A)
        vs = slice(h * DV_A, (h + 1) * DV_A)
        att = jnp.where(causal, _dot3_cat(q_dec[:, ks], k_dec[:, ks], NT), 0.0)
        o_sc[:, vs] = _dot3(att, v[:, vs])
    for c in range(n_chunks):
        rows = slice(c * chunk, (c + 1) * chunk)
        last = slice((c + 1) * chunk - 1, (c + 1) * chunk)
        b_last = b_cum[last, :]
        k_end = k[rows, :] * jnp.exp(b_last - b_cum[rows, :])
        dec = jnp.exp(b_last)
        for h in range(H_A):
            ks = slice(h * DK_A, (h + 1) * DK_A)
            vs = slice(h * DV_A, (h + 1) * DV_A)
            st = st_sc[h]
            o_sc[rows, vs] = o_sc[rows, vs] + _dot3_cat(q_dec[rows, ks], st, NT)
            st_sc[h] = st * dec[:, ks] + _dot3_cat(v[rows, vs], k_end[:, ks], TN)

    c_pieces = [p.astype(F32) for p in _split3(cum_pad)]
    in_group = lambda first: (lane >= first * SM_DT0) & (lane < first * SM_DT0 + 3 * SM_DT0) & ((lane & 15) < H_B)
    seg_x = jnp.where(in_group(4), 1.0, 0.0)
    seg_y = jnp.where(in_group(1), 1.0, 0.0)
    for n, piece in enumerate(c_pieces):
        seg_x = seg_x + (piece if n == 0 else pltpu.roll(piece, n * SM_DT0, 1))
        seg_y = seg_y - pltpu.roll(piece, (3 + n) * SM_DT0, 1)
    seg_y = seg_y.astype(BF16)
    for g in range(G_B):
        gs = slice(g * N_B, (g + 1) * N_B)
        cb = _dot3_cat(cm[:, gs], bm[:, gs], NT)
        for hh in range(g * (H_B // G_B), (g + 1) * (H_B // G_B)):
            ps = slice(hh * P_B, (hh + 1) * P_B)
            seg = _mm(jnp.where((lane & 15) == hh, seg_x, 0.0).astype(BF16), seg_y, NT)
            w = cb * jnp.exp(jnp.where(causal, seg, -jnp.inf))
            y_sc[:, ps] = _dot3(w, xdt[:, ps])
    for c in range(n_chunks):
        rows = slice(c * chunk, (c + 1) * chunk)
        last = slice((c + 1) * chunk - 1, (c + 1) * chunk)
        cum_c = cum_rep[rows, :]
        cum_l = cum_rep[last, :]
        e_cum = jnp.exp(cum_c)
        x_dec = xdt[rows, :] * jnp.exp(cum_l - cum_c)
        e_last = jnp.exp(cum_l)
        for hh in range(H_B):
            g = hh // (H_B // G_B)
            gs = slice(g * N_B, (g + 1) * N_B)
            ps = slice(hh * P_B, (hh + 1) * P_B)
            hs = h_sc[hh]
            y_sc[rows, ps] = y_sc[rows, ps] + _dot3_cat(cm[rows, gs], hs, NT) * e_cum[:, ps]
            h_sc[hh] = hs * e_last[:, ps] + _dot3_cat(x_dec[:, ps], bm[rows, gs], TN)

    gn = gnorm_ref[...]
    for h in range(H_A):
        vs = slice(h * DV_A, (h + 1) * DV_A)
        oh = o_sc[:, vs]
        oh = oh * lax.rsqrt(jnp.mean(oh * oh, axis=-1, keepdims=True) + RMS_EPS) * gn
        oy_ref[:, vs] = oh * _silu(r[:, vs])
    y = (y_sc[...] + dskip_ref[...] * xs) * _silu(z_ref[...])
    y = y * lax.rsqrt(jnp.mean(y * y, axis=-1, keepdims=True) + RMS_EPS) * snorm_ref[...]
    oy_ref[:, H_A * DV_A:] = y

    @pl.when(t == nt - 1)
    def _():
        s_out_ref[0] = st_sc[...]
        h_out_ref[0] = h_sc[...]


def _scan_call(proj, s0t, h0, c0, prm, *, specs, nb, t_len, tc, chunk, valid):
    nt = t_len // tc
    assert t_len % tc == 0 and (valid == tc or nt == 1)
    rowspec = lambda c: pl.BlockSpec((tc, c), lambda b, t: (b * nt + t, 0))
    stspec = lambda shp: pl.BlockSpec((1,) + shp, lambda b, t: (b,) + (0,) * len(shp))
    names = ("wg2h", "wg2l", "bg", "gnorm", "cw", "cb", "dtb", "alog", "dskip", "snorm")
    consts = [prm[n] for n in names]
    return pl.pallas_call(
        functools.partial(_scan_kernel, tc=tc, chunk=chunk, valid=valid),
        grid=(nb, nt),
        in_specs=list(specs)
                 + [stspec((H_A, DV_A, DK_A)), stspec((H_B, P_B, N_B)), stspec((CONV_HDR, CONV_DIM))]
                 + [_const_spec(c.shape) for c in consts],
        out_specs=[rowspec(H_A * DV_A + D_B), stspec((H_A, DV_A, DK_A)), stspec((H_B, P_B, N_B)),
                   stspec((CONV_HDR, CONV_DIM))],
        out_shape=[jax.ShapeDtypeStruct((nb * t_len, H_A * DV_A + D_B), F32),
                   jax.ShapeDtypeStruct((nb, H_A, DV_A, DK_A), F32),
                   jax.ShapeDtypeStruct((nb, H_B, P_B, N_B), F32),
                   jax.ShapeDtypeStruct((nb, CONV_HDR, CONV_DIM), F32)],
        scratch_shapes=[pltpu.VMEM((H_A, DV_A, DK_A), F32), pltpu.VMEM((H_B, P_B, N_B), F32),
                        pltpu.VMEM((tc + CONV_HDR, CONV_DIM), F32),
                        pltpu.VMEM((tc, H_A * DV_A), F32), pltpu.VMEM((tc, D_B), F32)],
        compiler_params=_params(2),
        name="gla_ssd_scan",
    )(proj, proj, proj, proj, s0t, h0, c0, *consts)


AB_QKVR = 2 * H_A * DK_A + 2 * H_A * DV_A
AB_XBC0 = AB_QKVR
AB_SM0 = AB_XBC0 + CONV_DIM
AB_Z0 = 2560
AB_COLS = AB_Z0 + D_B


def _split_w(w):
    hi = w.astype(BF16)
    lo = (w - hi.astype(F32)).astype(BF16)
    return hi, lo


def _lane_pad(v, start, width=V7X_LANES):
    out = jnp.zeros((1, width), F32)
    return out.at[0, start:start + v.shape[0]].set(v.astype(F32))


def _ab_prepare(w_in, w_gate2, b_gate, gla_norm, conv_w, conv_b, dt_bias, a_log, d_skip, ssm_norm, w_out):
    o = np.cumsum([0, H_A * DK_A, H_A * DK_A, H_A * DV_A, H_A * DV_A, GATE_RANK, D_B, CONV_DIM, H_B])
    q0, gl0, z0, xbc0, dt0, end = o[0], o[4], o[5], o[6], o[7], o[8]
    k_dim = w_in.shape[0]
    w = jnp.concatenate([
        w_in[:, q0:gl0], w_in[:, xbc0:dt0], w_in[:, gl0:z0], w_in[:, dt0:end],
        jnp.zeros((k_dim, AB_Z0 - AB_SM0 - GATE_RANK - H_B), F32), w_in[:, z0:xbc0]], axis=1)
    assert w.shape[1] == AB_COLS
    wg2 = jnp.zeros((V7X_LANES, H_A * DK_A), F32).at[0:GATE_RANK].set(w_gate2)
    wg2h, wg2l = _split_w(wg2)
    prm = dict(
        wg2h=wg2h, wg2l=wg2l, bg=b_gate.reshape(1, -1), gnorm=gla_norm.reshape(1, -1),
        cw=conv_w, cb=conv_b.reshape(1, -1), dtb=_lane_pad(dt_bias, SM_DT0), alog=_lane_pad(a_log, SM_DT0),
        dskip=jnp.repeat(d_skip, P_B).reshape(1, -1), snorm=ssm_norm.reshape(1, -1))
    return _split_w(w), prm, _split_w(w_out)


def _scan_from_proj(proj, s0, h0, c0, prm, *, nb, t_len, tc, chunk, valid):
    nt = t_len // tc
    s0t = jnp.swapaxes(s0, -1, -2)
    c0p = jnp.zeros((nb, CONV_HDR, CONV_DIM), F32).at[:, CONV_HDR - (CONV_W - 1):].set(c0)
    colspec = lambda width, start: pl.BlockSpec((tc, width), lambda b, t: (b * nt + t, start // width))
    oy, s_new, h_new, c_new = _scan_call(
        proj, s0t, h0, c0p, prm,
        specs=[colspec(AB_QKVR, 0), colspec(D_B, AB_Z0), colspec(CONV_DIM, AB_XBC0), colspec(V7X_LANES, AB_SM0)],
        nb=nb, t_len=t_len, tc=tc, chunk=chunk, valid=valid)
    return oy, jnp.swapaxes(s_new, -1, -2), h_new, c_new[:, CONV_HDR - (CONV_W - 1):]


RT_FINE0 = N_GROUPS
RT_HALF = V7X_LANES // 2
MOE_TILE = 256


def _two_source_specs(na, nb, tm, width):
    assert na % tm == 0 and nb % tm == 0
    na_t = na // tm
    return (pl.BlockSpec((tm, width), lambda i, *_: (jnp.minimum(i, na_t - 1), 0)),
            pl.BlockSpec((tm, width), lambda i, *_: (jnp.maximum(i - na_t, 0), 0)), na_t)


def _router_kernel(xa_ref, xb_ref, wh_ref, wl_ref, b_ref, id_ref, rank_ref, gate_ref, cnt_ref, base_sc, *, na_t):
    @pl.when(pl.program_id(0) == 0)
    def _():
        base_sc[...] = jnp.zeros(base_sc.shape, F32)

    xh, xl = _split2(jnp.where(pl.program_id(0) < na_t, xa_ref[...], xb_ref[...]))
    logits = _dot_hl(xh, xl, wh_ref[...], wl_ref[...]) + b_ref[...]
    shp = logits.shape
    lane = _iota(shp, 1)
    lane_f = lane.astype(F32)
    big = float(V7X_LANES)
    ninf = -jnp.inf

    def first_max(mask):
        v = jnp.max(jnp.where(mask, logits, ninf), axis=-1, keepdims=True)
        i = jnp.min(jnp.where(mask & (logits == v), lane_f, big), axis=-1, keepdims=True)
        return v, i

    is_c = lane < N_GROUPS
    mc, grp = first_max(is_c)
    p_grp = 1.0 / jnp.sum(jnp.where(is_c, jnp.exp(logits - mc), 0.0), axis=-1, keepdims=True)
    fine = (lane >= RT_FINE0) & (lane < RT_FINE0 + N_EXPERTS)
    cand = fine & (((lane - RT_FINE0) >> 3).astype(F32) == grp)
    v1, i1 = first_max(cand)
    v2, i2 = first_max(cand & (lane_f != i1))
    e = jnp.exp(v2 - v1)
    w1 = p_grp / (1.0 + e)
    w2 = p_grp * (e / (1.0 + e))
    first = lane < RT_HALF
    id_ref[...] = (jnp.where(first, i1, i2) - float(RT_FINE0)).astype(jnp.int32)
    gate_ref[...] = jnp.where(first, w1, w2)

    tm = shp[0]
    oh1 = jnp.where(lane_f == i1, 1.0, 0.0)
    oh2 = jnp.where(lane_f == i2, 1.0, 0.0)
    oh = oh1 + oh2
    earlier = jnp.where(_iota((tm, tm), 1) < _iota((tm, tm), 0), 1.0, 0.0).astype(BF16)
    before = _mm(earlier, oh.astype(BF16)) + base_sc[...]
    r1 = jnp.sum(oh1 * before, axis=-1, keepdims=True)
    r2 = jnp.sum(oh2 * before, axis=-1, keepdims=True)
    rank_ref[...] = jnp.where(first, r1, r2).astype(jnp.int32)
    base_sc[...] = base_sc[...] + jnp.sum(oh, axis=0, keepdims=True)
    cnt_ref[...] = base_sc[...]


def _router(xa, xb, wh, wl, b, *, tm):
    d = xa.shape[1]
    n = xa.shape[0] + xb.shape[0]
    spec_a, spec_b, na_t = _two_source_specs(xa.shape[0], xb.shape[0], tm, d)
    row = pl.BlockSpec((tm, V7X_LANES), lambda i: (i, 0))
    return pl.pallas_call(
        functools.partial(_router_kernel, na_t=na_t),
        grid=(n // tm,),
        in_specs=[spec_a, spec_b, _const_spec(wh.shape), _const_spec(wl.shape), _const_spec(b.shape)],
        out_specs=[row, row, row, pl.BlockSpec((1, V7X_LANES), lambda i: (0, 0))],
        out_shape=[jax.ShapeDtypeStruct((n, V7X_LANES), jnp.int32), jax.ShapeDtypeStruct((n, V7X_LANES), jnp.int32),
                   jax.ShapeDtypeStruct((n, V7X_LANES), F32), jax.ShapeDtypeStruct((1, V7X_LANES), F32)],
        scratch_shapes=[pltpu.VMEM((1, V7X_LANES), F32)],
        compiler_params=_params(1),
        name="moe_router",
    )(xa, xb, wh, wl, b)


DMA_UNROLL = 8


def _pos_copy(pos_hbm, pos_sm, sem_p, step, slot):
    return pltpu.make_async_copy(pos_hbm.at[pl.ds(step, 1)], pos_sm.at[slot], sem_p.at[slot])


def _dispatch_kernel(zf_ref, xa_ref, xb_ref, pos_hbm, xs_hbm, pos_sm, zbuf, sem_p, sem_z, sem_s, *, na_t):
    i = pl.program_id(0)
    n = pl.num_programs(0)
    tm = xa_ref.shape[0]
    tile = zbuf.shape[0]
    n_tiles = xs_hbm.shape[0] // tile
    slot = lax.rem(i, 2)

    def zero_copy(j):
        return pltpu.make_async_copy(zbuf, xs_hbm.at[pl.ds(pl.multiple_of(j * tile, tile), tile)], sem_z)

    @pl.when(i == 0)
    def _():
        _pos_copy(pos_hbm, pos_sm, sem_p, 0, 0).start()
        zbuf[...] = jnp.zeros(zbuf.shape, F32)

        def z_start(j, c):
            @pl.when(zf_ref[j] > 0)
            def _():
                zero_copy(j).start()
            return c

        def z_wait(j, c):
            @pl.when(zf_ref[j] > 0)
            def _():
                zero_copy(j).wait()
            return c

        lax.fori_loop(0, n_tiles, z_start, 0)
        lax.fori_loop(0, n_tiles, z_wait, 0)

    @pl.when(i + 1 < n)
    def _():
        _pos_copy(pos_hbm, pos_sm, sem_p, i + 1, 1 - slot).start()

    _pos_copy(pos_hbm, pos_sm, sem_p, i, slot).wait()

    def scatter_rows(x_ref):
        def issue(r, c):
            for k in range(TOP_FINE):
                dst = pos_sm[slot, 0, k * tm + r]
                pltpu.make_async_copy(x_ref.at[pl.ds(r, 1)], xs_hbm.at[pl.ds(dst, 1)], sem_s).start()
            return c

        lax.fori_loop(0, tm, issue, 0, unroll=DMA_UNROLL)
        for _ in range(TOP_FINE):
            pltpu.make_async_copy(x_ref, xs_hbm.at[pl.ds(0, tm)], sem_s).wait()

    @pl.when(i < na_t)
    def _():
        scatter_rows(xa_ref)

    @pl.when(i >= na_t)
    def _():
        scatter_rows(xb_ref)


def _dispatch(xa, xb, pos_tab, zero_flag, *, n_tiles, tm):
    d = xa.shape[1]
    n = xa.shape[0] + xb.shape[0]
    spec_a, spec_b, na_t = _two_source_specs(xa.shape[0], xb.shape[0], tm, d)
    any_spec = pl.BlockSpec(memory_space=pl.ANY)
    grid_spec = pltpu.PrefetchScalarGridSpec(
        num_scalar_prefetch=1,
        grid=(n // tm,),
        in_specs=[spec_a, spec_b, any_spec],
        out_specs=any_spec,
        scratch_shapes=[pltpu.SMEM((2, 1, TOP_FINE * tm), jnp.int32), pltpu.VMEM((MOE_TILE, d), F32),
                        pltpu.SemaphoreType.DMA((2,)), pltpu.SemaphoreType.DMA(()), pltpu.SemaphoreType.DMA(())],
    )
    return pl.pallas_call(
        functools.partial(_dispatch_kernel, na_t=na_t),
        grid_spec=grid_spec,
        out_shape=jax.ShapeDtypeStruct((n_tiles * MOE_TILE, d), F32),
        compiler_params=_params(1),
        name="moe_dispatch",
    )(zero_flag, xa, xb, pos_tab)


def _expert_kernel(te_ref, xs_ref, wgu_ref, wd_ref, ys_ref, *, three_pass):
    if three_pass:
        xh, xl = _split2(xs_ref[...])
        wh, wl = _split2(wgu_ref[0])
        hgu = _dot_hl(xh, xl, wh, wl)
    else:
        hgu = _mm(xs_ref[...].astype(BF16), wgu_ref[0].astype(BF16))
    act = _silu(hgu[:, :D_EXPERT]) * hgu[:, D_EXPERT:]
    if three_pass:
        ys_ref[...] = _dot3(act, wd_ref[0])
    else:
        ys_ref[...] = _mm(act.astype(BF16), wd_ref[0].astype(BF16))


def _experts(xs, w_gate_up, w_down, tile_expert, *, three_pass):
    p, d = xs.shape
    f2 = w_gate_up.shape[-1]
    row = pl.BlockSpec((MOE_TILE, d), lambda i, te: (i, 0))
    grid_spec = pltpu.PrefetchScalarGridSpec(
        num_scalar_prefetch=1,
        grid=(p // MOE_TILE,),
        in_specs=[row, pl.BlockSpec((1, d, f2), lambda i, te: (te[i], 0, 0)),
                  pl.BlockSpec((1, f2 // 2, d), lambda i, te: (te[i], 0, 0))],
        out_specs=row,
    )
    return pl.pallas_call(
        functools.partial(_expert_kernel, three_pass=three_pass),
        grid_spec=grid_spec,
        out_shape=jax.ShapeDtypeStruct((p, d), F32),
        compiler_params=_params(1),
        name="moe_experts",
    )(tile_expert, xs, w_gate_up, w_down)


def _combine_ln_kernel(xa_ref, xb_ref, gate_ref, g_ref, b_ref, pos_hbm, ys_hbm, oa_ref, ob_ref,
                       pos_sm, ybuf, sem_p, sem_y, *, na_t):
    i = pl.program_id(0)
    n = pl.num_programs(0)
    tm = xa_ref.shape[0]
    slot = lax.rem(i, 2)

    def issue_gathers(s):
        def issue(r, c):
            for k in range(TOP_FINE):
                src = pos_sm[s, 0, k * tm + r]
                pltpu.make_async_copy(ys_hbm.at[pl.ds(src, 1)], ybuf.at[s, pl.ds(k * tm + r, 1)],
                                      sem_y.at[s]).start()
            return c

        lax.fori_loop(0, tm, issue, 0, unroll=DMA_UNROLL)

    @pl.when(i == 0)
    def _():
        _pos_copy(pos_hbm, pos_sm, sem_p, 0, 0).start()
        _pos_copy(pos_hbm, pos_sm, sem_p, 0, 0).wait()
        issue_gathers(0)

        @pl.when(n > 1)
        def _():
            _pos_copy(pos_hbm, pos_sm, sem_p, 1, 1).start()

    @pl.when(i + 1 < n)
    def _():
        _pos_copy(pos_hbm, pos_sm, sem_p, i + 1, 1 - slot).wait()
        issue_gathers(1 - slot)

    @pl.when(i + 2 < n)
    def _():
        _pos_copy(pos_hbm, pos_sm, sem_p, i + 2, slot).start()

    pltpu.make_async_copy(ys_hbm.at[pl.ds(0, TOP_FINE * tm)], ybuf.at[slot], sem_y.at[slot]).wait()
    gate = gate_ref[...]
    y = ybuf[slot]
    moe = gate[:, 0:1] * y[0:tm] + gate[:, RT_HALF:RT_HALF + 1] * y[tm:2 * tm]

    @pl.when(i < na_t)
    def _():
        oa_ref[...] = _layer_norm(DN_ALPHA * xa_ref[...] + moe, g_ref[...], b_ref[...])

    @pl.when(i >= na_t)
    def _():
        ob_ref[...] = _layer_norm(DN_ALPHA * xb_ref[...] + moe, g_ref[...], b_ref[...])


def _combine_ln(xa, xb, ys, pos_tab, gate, g, b, *, tm):
    d = xa.shape[1]
    na, nb = xa.shape[0], xb.shape[0]
    assert TOP_FINE == 2 and nb == tm
    spec_a, spec_b, na_t = _two_source_specs(na, nb, tm, d)
    any_spec = pl.BlockSpec(memory_space=pl.ANY)
    return pl.pallas_call(
        functools.partial(_combine_ln_kernel, na_t=na_t),
        grid=((na + nb) // tm,),
        in_specs=[spec_a, spec_b, pl.BlockSpec((tm, V7X_LANES), lambda i: (i, 0)), _const_spec((1, d)),
                  _const_spec((1, d)), any_spec, any_spec],
        out_specs=[spec_a, spec_b],
        out_shape=[jax.ShapeDtypeStruct((na, d), F32), jax.ShapeDtypeStruct((nb, d), F32)],
        scratch_shapes=[pltpu.SMEM((2, 1, TOP_FINE * tm), jnp.int32), pltpu.VMEM((2, TOP_FINE * tm, d), F32),
                        pltpu.SemaphoreType.DMA((2,)), pltpu.SemaphoreType.DMA((2,))],
        compiler_params=_params(1),
        name="moe_combine_ln",
    )(xa, xb, gate, g, b, pos_tab, ys)


def _slots_kernel(id_ref, rank_ref, base_ref, pos_ref):
    ids = id_ref[...].astype(F32)
    tm = ids.shape[0]
    lane = _iota(ids.shape, 1)
    lane_f = lane.astype(F32)
    base = base_ref[...]

    def first_row(choice):
        hit = lane_f == ids[:, choice:choice + 1] + float(RT_FINE0)
        return jnp.sum(jnp.where(hit, base, 0.0), axis=-1, keepdims=True)

    pos = rank_ref[...].astype(F32) + jnp.where(lane < RT_HALF, first_row(0), first_row(RT_HALF))
    pos_t = jnp.transpose(pos)
    pos_ref[0] = jnp.concatenate([pos_t[0:1, :], pos_t[RT_HALF:RT_HALF + 1, :]], axis=1).astype(jnp.int32)


def _slots(ids, rank, base, *, tm):
    n = ids.shape[0]
    assert tm == V7X_LANES and n % tm == 0
    row = pl.BlockSpec((tm, V7X_LANES), lambda i: (i, 0))
    out = pl.pallas_call(
        _slots_kernel,
        grid=(n // tm,),
        in_specs=[row, row, _const_spec((1, V7X_LANES))],
        out_specs=pl.BlockSpec((1, 1, TOP_FINE * tm), lambda i: (i, 0, 0)),
        out_shape=jax.ShapeDtypeStruct((n // tm, 1, TOP_FINE * tm), jnp.int32),
        compiler_params=_params(1),
        name="moe_slots",
    )(ids, rank, base)
    return out.reshape(n // tm, TOP_FINE * tm)


def _moe_plan(ids, rank, cnt, n, tm):
    counts = cnt[0, RT_FINE0:RT_FINE0 + N_EXPERTS].astype(jnp.int32)
    tiles_per = (counts + MOE_TILE - 1) // MOE_TILE
    e_idx = jnp.arange(N_EXPERTS)
    tile_end = tiles_per @ (e_idx[:, None] <= e_idx[None, :]).astype(jnp.int32)
    tile_beg = tile_end - tiles_per
    n_tiles = -(-TOP_FINE * n // MOE_TILE) + N_EXPERTS
    ti = jnp.arange(n_tiles)
    te = jnp.minimum(jnp.sum((tile_end[None, :] <= ti[:, None]).astype(jnp.int32), axis=1), N_EXPERTS - 1)
    is_last = jnp.any((ti[:, None] == tile_end[None, :] - 1) & (tiles_per[None, :] > 0), axis=1)
    zero_flag = (is_last | (ti >= tile_end[-1])).astype(jnp.int32)
    base = jnp.zeros((1, V7X_LANES), F32).at[0, RT_FINE0:RT_FINE0 + N_EXPERTS].set(
        (tile_beg * MOE_TILE).astype(F32))
    pos_tab = _slots(ids, rank, base, tm=tm)
    return te.astype(jnp.int32), zero_flag, pos_tab, n_tiles


def _moe_prepare(w_coarse, b_coarse, w_fine, b_fine):
    d = w_coarse.shape[0]
    wf = jnp.transpose(w_fine, (1, 0, 2)).reshape(d, N_EXPERTS)
    w = jnp.zeros((d, V7X_LANES), F32).at[:, :N_GROUPS].set(w_coarse).at[:, RT_FINE0:RT_FINE0 + N_EXPERTS].set(wf)
    b = jnp.zeros((1, V7X_LANES), F32).at[0, :N_GROUPS].set(b_coarse)
    b = b.at[0, RT_FINE0:RT_FINE0 + N_EXPERTS].set(b_fine.reshape(-1))
    return _split_w(w), b


def _moe_layer(xa, xb, w_coarse, b_coarse, w_fine, b_fine, w_gate_up, w_down, layer, g, b, *, tm, three_pass):
    d = xa.shape[1]
    n = xa.shape[0] + xb.shape[0]
    (wrh, wrl), br = _moe_prepare(w_coarse, b_coarse, w_fine, b_fine)
    ids, rank, gate, cnt = _router(xa, xb, wrh, wrl, br, tm=tm)
    te, zero_flag, pos_tab, n_tiles = _moe_plan(ids, rank, cnt, n, tm)
    xs = _dispatch(xa, xb, pos_tab, zero_flag, n_tiles=n_tiles, tm=tm)
    ys = _experts(xs, w_gate_up.reshape(-1, d, 2 * D_EXPERT), w_down.reshape(-1, D_EXPERT, d),
                  te + layer * N_EXPERTS, three_pass=three_pass)
    return _combine_ln(xa, xb, ys, pos_tab, gate, g, b, tm=tm)


def _cdproj_kernel(x_ref, wh_ref, wl_ref, cos_ref, sin_ref, perm_ref, u_ref, q_ref, k_ref, v_ref):
    xh, xl = _split2(x_ref[...])

    def col(j):
        cs = slice(j * D_C, (j + 1) * D_C)
        return _dot_hl(xh, xl, wh_ref[:, cs], wl_ref[:, cs])

    u_ref[...] = col(0)
    for j, ref in ((1, q_ref), (2, k_ref)):
        t = col(j)
        ref[...] = t * cos_ref[...] + _dotx(t, perm_ref[...]) * sin_ref[...]
    v_ref[...] = col(3)


def _cdproj(x, wh, wl, cos_t, sin_t, perm, *, n_rows, tm):
    k = x.shape[1]
    assert D_C == D_D and n_rows % tm == 0 and cos_t.shape[0] % tm == 0
    nt = cos_t.shape[0] // tm
    row = pl.BlockSpec((tm, D_D), lambda i: (i, 0))
    tab = pl.BlockSpec((tm, D_D), lambda i: (i % nt, 0))
    return pl.pallas_call(
        _cdproj_kernel,
        grid=(n_rows // tm,),
        in_specs=[pl.BlockSpec((tm, k), lambda i: (i, 0)), _const_spec(wh.shape), _const_spec(wl.shape),
                  tab, tab, _const_spec(perm.shape)],
        out_specs=[row] * 4,
        out_shape=[jax.ShapeDtypeStruct((n_rows, D_D), F32)] * 4,
        compiler_params=_params(1),
        name="cd_proj_rope",
    )(x, wh, wl, cos_t, sin_t, perm)


def _rope_tables(pos):
    half = ROT_DIM // 2
    inv = ROPE_THETA ** (-jnp.arange(half, dtype=F32) / half)
    ang = pos.astype(F32)[:, None] * inv
    cos, sin = jnp.cos(ang), jnp.sin(ang)
    n = pos.shape[0]
    rest = HD_D - ROT_DIM
    cos_h = jnp.concatenate([cos, cos, jnp.ones((n, rest), F32)], axis=-1)
    sin_h = jnp.concatenate([-sin, sin, jnp.zeros((n, rest), F32)], axis=-1)
    perm = np.zeros((D_D, D_D), np.float32)
    for dst in range(D_D):
        j = dst % HD_D
        if j < half:
            perm[dst + half, dst] = 1.0
        elif j < ROT_DIM:
            perm[dst - half, dst] = 1.0
    return jnp.tile(cos_h, (1, H_D)), jnp.tile(sin_h, (1, H_D)), jnp.asarray(perm, BF16)


POOL_HDR = 16


def _pool_kernel(u_ref, b0_ref, wph_ref, wpl_ref, sc_ref, o_ref, bo_ref, ext_sc, *, tc, valid, start):
    t = pl.program_id(1)
    nt = pl.num_programs(1)

    @pl.when(t == 0)
    def _():
        ext_sc[0:POOL_HDR, :] = b0_ref[0]

    ext_sc[POOL_HDR:POOL_HDR + tc, :] = u_ref[...]
    u = u_ref[...]
    rowi = _iota((tc, POOL_HDR + tc), 0) + POOL_HDR
    colj = _iota((tc, POOL_HDR + tc), 1)
    pos = (start + t * tc + _iota((tc, POOL_GROUP), 0)).astype(F32)
    for gi, w in enumerate(POOL_WINDOWS):
        ls = slice(gi * POOL_GROUP, (gi + 1) * POOL_GROUP)
        band = jnp.where((colj <= rowi) & (colj > rowi - w), 1.0, 0.0).astype(BF16)
        win = _xdot(band, ext_sc[:, ls])
        cnt = jnp.minimum(float(w), pos + 1.0)
        dh, dl = _split2(win / cnt - u[:, ls])
        o_ref[:, ls] = _dot_hl(dh, dl, wph_ref[gi], wpl_ref[gi]) * sc_ref[:, ls]

    @pl.when(t == nt - 1)
    def _():
        bo_ref[0] = ext_sc[valid:valid + POOL_HDR, :]

    ext_sc[0:POOL_HDR, :] = ext_sc[tc:tc + POOL_HDR, :]


def _pool(u, buf, wph, wpl, scale, *, nb, t_len, tc, valid, start):
    nt = t_len // tc
    assert t_len % tc == 0 and (valid == tc or nt == 1)
    b0 = jnp.zeros((nb, POOL_HDR, D_C), F32).at[:, POOL_HDR - POOL_BUF:].set(buf)
    row = pl.BlockSpec((tc, D_C), lambda b, t: (b * nt + t, 0))
    st = pl.BlockSpec((1, POOL_HDR, D_C), lambda b, t: (b, 0, 0))
    pooled, bnew = pl.pallas_call(
        functools.partial(_pool_kernel, tc=tc, valid=valid, start=start),
        grid=(nb, nt),
        in_specs=[row, st, _const_spec(wph.shape), _const_spec(wpl.shape), _const_spec(scale.shape)],
        out_specs=[row, st],
        out_shape=[jax.ShapeDtypeStruct((nb * t_len, D_C), F32), jax.ShapeDtypeStruct((nb, POOL_HDR, D_C), F32)],
        scratch_shapes=[pltpu.VMEM((POOL_HDR + tc, D_C), F32)],
        compiler_params=_params(2),
        name="pool_mix",
    )(u, b0, wph, wpl, scale)
    return pooled, bnew[:, POOL_HDR - POOL_BUF:]


def _top_blocks(gate, lane_f, valid, n_pick):
    sel = jnp.zeros(gate.shape, jnp.bool_)
    g = jnp.where(valid, gate, -jnp.inf)
    picks = []
    for _ in range(n_pick):
        v = jnp.max(g, axis=-1, keepdims=True)
        idx = jnp.min(jnp.where(g == v, lane_f, float(V7X_LANES)), axis=-1, keepdims=True)
        hit = (lane_f == idx) & valid
        sel = sel | hit
        g = jnp.where(hit, -jnp.inf, g)
        picks.append(idx)
    return sel, picks


def _cat_pieces(a, b, c):
    return jnp.concatenate([a, b, c, jnp.zeros_like(a)], axis=1)


MOBA_TRIP = 4


def _top_rows(gate, row_f, valid, n_pick):
    sel = jnp.zeros(gate.shape, F32)
    g = jnp.where(valid, gate, -jnp.inf)
    for _ in range(n_pick):
        v = jnp.max(g, axis=0, keepdims=True)
        idx = jnp.min(jnp.where(g == v, row_f, float(V7X_LANES)), axis=0, keepdims=True)
        hit = (row_f == idx) & valid
        sel = jnp.where(hit, 1.0, sel)
        g = jnp.where(hit, -jnp.inf, g)
    return sel


def _moba_prompt_kernel(q_ref, k_ref, v_ref, o_ref, kmean_sc, kcat_sc, vt_sc, sel_sc):
    i = pl.program_id(2)
    blk = MOBA_BLOCK
    t_len = k_ref.shape[0]
    n_heads = V7X_LANES // HD_D
    g_rows = sel_sc.shape[1]
    scale = HD_D ** -0.5

    @pl.when(i == 0)
    def _():
        ind = jnp.where((_iota((V7X_LANES, t_len), 1) >> 8) == _iota((V7X_LANES, t_len), 0), 1.0, 0.0)
        kmean_sc[...] = _xdot(ind.astype(BF16), k_ref[...]) * (1.0 / blk)
        ones_row = jnp.where(_iota((HD_D, blk), 0) == 0, 1.0, 0.0).astype(BF16)
        for n in range(t_len // blk):
            rows = slice(n * blk, (n + 1) * blk)
            kh, kl = _split2(k_ref[rows, :])
            vt = jnp.transpose(v_ref[rows, :]).astype(BF16)
            for hh in range(n_heads):
                hs = slice(hh * HD_D, (hh + 1) * HD_D)
                kcat_sc[hh, rows, :] = _cat_pieces(kh[:, hs], kh[:, hs], kl[:, hs])
                vt_sc[hh, n, 0:HD_D, :] = vt[hs, :]
                vt_sc[hh, n, HD_D:2 * HD_D, :] = ones_row

    def scores_t(hh, qcat, j):
        return _mm(kcat_sc[hh, pl.ds(pl.multiple_of(j * blk, blk), blk), :], qcat, NT)

    def weighted_t(hh, p, j):
        return _mm(vt_sc[hh, j], p.astype(BF16))

    q = q_ref[...]
    key_le_query = _iota((blk, blk), 0) <= _iota((blk, blk), 1)
    brow = _iota((g_rows, blk), 0)
    qcats, state = [], []
    for hh in range(n_heads):
        hs = slice(hh * HD_D, (hh + 1) * HD_D)
        qh = q[:, hs]
        gate_t = _dot3_cat(kmean_sc[0:g_rows, hs], qh, NT)
        sel_sc[hh] = _top_rows(gate_t, brow.astype(F32), brow < i, MOBA_TOPK)
        q_hi, q_lo = _split2(qh * scale)
        qcats.append(_cat_pieces(q_hi, q_lo, q_hi))
        s = jnp.where(key_le_query, scores_t(hh, qcats[hh], i), -jnp.inf)
        m = jnp.max(s, axis=0, keepdims=True)
        state += [m, weighted_t(hh, jnp.exp(s - m), i)]

    def body(jj, carry):
        out = []
        for hh in range(n_heads):
            m, acc = carry[2 * hh:2 * hh + 2]
            js = [jnp.minimum(MOBA_TRIP * jj + t, i) for t in range(MOBA_TRIP)]
            ss = []
            for j in js:
                on = sel_sc[hh, pl.ds(j, 1), :] > 0.0
                ss.append(jnp.where(on, scores_t(hh, qcats[hh], j), -jnp.inf))
            top = functools.reduce(jnp.maximum, ss)
            m2 = jnp.maximum(m, jnp.max(top, axis=0, keepdims=True))
            new = functools.reduce(lambda x, y: x + y,
                                   [weighted_t(hh, jnp.exp(sj - m2), j) for sj, j in zip(ss, js)])
            out += [m2, jnp.exp(m - m2) * acc + new]
        return tuple(out)

    state = lax.fori_loop(0, (i + MOBA_TRIP - 1) >> (MOBA_TRIP.bit_length() - 1), body, tuple(state))
    for hh in range(n_heads):
        acc = jnp.transpose(state[2 * hh + 1])
        o_ref[:, hh * HD_D:(hh + 1) * HD_D] = acc[:, 0:HD_D] / acc[:, HD_D:HD_D + 1]


def _moba_prompt(q, k, v, *, nb, t_len):
    blk = MOBA_BLOCK
    nq = t_len // blk
    assert t_len % blk == 0 and nq <= V7X_LANES
    n_heads = V7X_LANES // HD_D
    g_rows = -(-nq // V7X_SUBLANES) * V7X_SUBLANES
    qspec = pl.BlockSpec((blk, V7X_LANES), lambda b, hp, i: (b * nq + i, hp))
    kspec = pl.BlockSpec((t_len, V7X_LANES), lambda b, hp, i: (b, hp))
    return pl.pallas_call(
        _moba_prompt_kernel,
        grid=(nb, D_D // V7X_LANES, nq),
        in_specs=[qspec, kspec, kspec],
        out_specs=qspec,
        out_shape=jax.ShapeDtypeStruct((nb * t_len, D_D), F32),
        scratch_shapes=[pltpu.VMEM((V7X_LANES, V7X_LANES), F32),
                        pltpu.VMEM((n_heads, t_len, 4 * HD_D), BF16),
                        pltpu.VMEM((n_heads, nq, 2 * HD_D, blk), BF16),
                        pltpu.VMEM((n_heads, g_rows, blk), F32)],
        compiler_params=_params(3),
        name="moba_prompt",
    )(q, k, v)


KM_PAGES = 32
PAGES_PER_BLOCK = MOBA_BLOCK // PAGE_SIZE


def _kmean_kernel(pt_ref, *refs):
    pages, o_ref = refs[:KM_PAGES], refs[KM_PAGES]
    c = pl.program_id(1)
    blocks_per_step = KM_PAGES // PAGES_PER_BLOCK

    @pl.when(c == 0)
    def _():
        o_ref[...] = jnp.zeros(o_ref.shape, F32)

    acc = o_ref[0]
    lane = _iota(acc.shape, 2)
    for blk in range(blocks_per_step):
        tot = pages[blk * PAGES_PER_BLOCK][0]
        for p in range(1, PAGES_PER_BLOCK):
            tot = tot + pages[blk * PAGES_PER_BLOCK + p][0]
        mean = jnp.sum(tot, axis=-1, keepdims=True) * (1.0 / MOBA_BLOCK)
        acc = jnp.where(lane == c * blocks_per_step + blk, mean, acc)
    o_ref[0] = acc


def _block_means(cache_kt, page_table):
    db, n_pages = page_table.shape
    n_blocks = n_pages // PAGES_PER_BLOCK
    assert n_pages % KM_PAGES == 0
    page_spec = lambda j: pl.BlockSpec((1, H_D, HD_D, PAGE_SIZE), lambda b, c, pt: (pt[b, c * KM_PAGES + j], 0, 0, 0))
    grid_spec = pltpu.PrefetchScalarGridSpec(
        num_scalar_prefetch=1,
        grid=(db, n_pages // KM_PAGES),
        in_specs=[page_spec(j) for j in range(KM_PAGES)],
        out_specs=pl.BlockSpec((1, H_D, HD_D, n_blocks), lambda b, c, pt: (b, 0, 0, 0)),
    )
    return pl.pallas_call(
        _kmean_kernel,
        grid_spec=grid_spec,
        out_shape=jax.ShapeDtypeStruct((db, H_D, HD_D, n_blocks), F32),
        compiler_params=_params(2),
        name="moba_block_means",
    )(page_table, *([cache_kt] * KM_PAGES))


def _select_kernel(q_ref, km_ref, o_ref):
    q = q_ref[...]
    n_blk = km_ref.shape[3]
    lane_f = _iota((q.shape[0], n_blk), 1).astype(F32)
    out_lane = _iota((q.shape[0], V7X_LANES), 1)
    for h in range(H_D):
        gate = _dot3(q[:, h * HD_D:(h + 1) * HD_D], km_ref[0, h])
        _, picks = _top_blocks(gate, lane_f, lane_f >= 0.0, MOBA_TOPK)
        res = jnp.zeros((q.shape[0], V7X_LANES), F32)
        for r, idx in enumerate(picks):
            res = jnp.where(out_lane == r, idx, res)
        o_ref[0, h] = res.astype(jnp.int32)


def _select_blocks(q, kmean_t):
    db = kmean_t.shape[0]
    return pl.pallas_call(
        _select_kernel,
        grid=(db,),
        in_specs=[pl.BlockSpec((SAMPLE_PAD, D_D), lambda b: (b, 0)),
                  pl.BlockSpec((1,) + kmean_t.shape[1:], lambda b: (b, 0, 0, 0))],
        out_specs=pl.BlockSpec((1, H_D, SAMPLE_PAD, V7X_LANES), lambda b: (b, 0, 0, 0)),
        out_shape=jax.ShapeDtypeStruct((db, H_D, SAMPLE_PAD, V7X_LANES), jnp.int32),
        compiler_params=_params(1),
        name="moba_select",
    )(q, kmean_t)


HEADS_PER_STEP = V7X_LANES // HD_D


def _attend_kernel(phys_ref, q_ref, kn_ref, vn_ref, ck_hbm, cv_hbm, o_ref, kbuf, vbuf, sem, *, n_q):
    b = pl.program_id(0)
    hp = pl.program_id(1)
    n_hp = pl.num_programs(1)
    step = b * n_hp + hp
    slot = lax.rem(step, 2)
    scale = HD_D ** -0.5

    def fetch(fb, fhp, fslot):
        for hh in range(HEADS_PER_STEP):
            h = fhp * HEADS_PER_STEP + hh
            for s in range(n_q):
                for r in range(MOBA_TOPK):
                    for pg in range(PAGES_PER_BLOCK):
                        flat = (((fb * H_D + h) * n_q + s) * MOBA_TOPK + r) * PAGES_PER_BLOCK + pg
                        cols = pl.ds((r * PAGES_PER_BLOCK + pg) * PAGE_SIZE, PAGE_SIZE)
                        page = phys_ref[flat]
                        for which, (src, dst) in enumerate(((ck_hbm, kbuf), (cv_hbm, vbuf))):
                            pltpu.make_async_copy(src.at[page, h], dst.at[fslot, hh, s, :, cols],
                                                  sem.at[fslot, which]).start()

    @pl.when(step == 0)
    def _():
        fetch(b, hp, slot)

    @pl.when(step + 1 < pl.num_programs(0) * n_hp)
    def _():
        wrap = hp + 1 == n_hp
        fetch(jnp.where(wrap, b + 1, b), jnp.where(wrap, 0, hp + 1), 1 - slot)

    pltpu.make_async_copy(vbuf.at[1 - slot], kbuf.at[slot], sem.at[slot, 0]).wait()
    pltpu.make_async_copy(kbuf.at[1 - slot], vbuf.at[slot], sem.at[slot, 1]).wait()

    q = q_ref[...]
    rows_n = q.shape[0]
    rowi = _iota((rows_n, rows_n), 0)
    coli = _iota((rows_n, rows_n), 1)
    own_ok = (coli <= rowi) & (coli < n_q)
    rsel = _iota((rows_n, HD_D), 0)
    for hh in range(HEADS_PER_STEP):
        hs = slice(hh * HD_D, (hh + 1) * HD_D)
        qh = q[:, hs]
        s_own = jnp.where(own_ok, _dot3(qh, kn_ref[...][:, hs], NT) * scale, -jnp.inf)
        m_own = jnp.max(s_own, axis=-1, keepdims=True)
        out = jnp.zeros((rows_n, HD_D), F32)
        for s in range(n_q):
            s_sel = _dot3_cat(qh, kbuf[slot, hh, s]) * scale
            m = jnp.maximum(m_own, jnp.max(s_sel, axis=-1, keepdims=True))
            p_sel = jnp.exp(s_sel - m)
            p_own = jnp.exp(s_own - m)
            l = jnp.sum(p_sel, axis=-1, keepdims=True) + jnp.sum(p_own, axis=-1, keepdims=True)
            o_s = (_dot3(p_sel, vbuf[slot, hh, s], NT) + _dot3(p_own, vn_ref[...][:, hs])) / l
            out = jnp.where(rsel == s, o_s, out)
        o_ref[:, hs] = out


def _moba_sample_attend(phys, q, k_new, v_new, cache_kt, cache_vt, *, db, n_q):
    row = pl.BlockSpec((SAMPLE_PAD, V7X_LANES), lambda b, hp, ph: (b, hp))
    any_spec = pl.BlockSpec(memory_space=pl.ANY)
    buf = pltpu.VMEM((2, HEADS_PER_STEP, n_q, HD_D, MOBA_TOPK * MOBA_BLOCK), F32)
    grid_spec = pltpu.PrefetchScalarGridSpec(
        num_scalar_prefetch=1,
        grid=(db, D_D // V7X_LANES),
        in_specs=[row, row, row, any_spec, any_spec],
        out_specs=row,
        scratch_shapes=[buf, buf, pltpu.SemaphoreType.DMA((2, 2))],
    )
    return pl.pallas_call(
        functools.partial(_attend_kernel, n_q=n_q),
        grid_spec=grid_spec,
        out_shape=jax.ShapeDtypeStruct((db * SAMPLE_PAD, D_D), F32),
        compiler_params=_params(2),
        name="moba_sample_attend",
    )(phys, q, k_new, v_new, cache_kt, cache_vt)


def _outproj2_ln_kernel(a1_ref, a2_ref, x_ref, w1h_ref, w1l_ref, w2h_ref, w2l_ref, g_ref, b_ref, o_ref):
    y = _dot_hl(*_split2(a1_ref[...]), w1h_ref[...], w1l_ref[...])
    y = y + _dot_hl(*_split2(a2_ref[...]), w2h_ref[...], w2l_ref[...])
    o_ref[...] = _layer_norm(DN_ALPHA * x_ref[...] + y, g_ref[...], b_ref[...])


def _outproj2_ln(a1, a2, x, w1, w2, g, b, *, n_rows, tm):
    d = x.shape[1]
    assert n_rows % tm == 0
    row = lambda c: pl.BlockSpec((tm, c), lambda i: (i, 0))
    consts = [w1[0], w1[1], w2[0], w2[1], g, b]
    return pl.pallas_call(
        _outproj2_ln_kernel,
        grid=(n_rows // tm,),
        in_specs=[row(a1.shape[1]), row(a2.shape[1]), row(d)] + [_const_spec(c.shape) for c in consts],
        out_specs=row(d),
        out_shape=jax.ShapeDtypeStruct((n_rows, d), F32),
        compiler_params=_params(1),
        name="outproj2_ln",
    )(a1, a2, x, *consts)


SAMPLE_PAD = 16


def _pad_rows(x, t_pad):
    nb, t, d = x.shape
    return jnp.zeros((nb, t_pad, d), x.dtype).at[:, :t].set(x).reshape(nb * t_pad, d)


def kernel(x_prompt, x_sample, state_gla, state_ssm, state_conv, state_pool, cache_k, cache_v, page_table, ab_w_in, ab_w_gate2, ab_b_gate, ab_gla_norm, ab_conv_w, ab_conv_b, ab_dt_bias, ab_a_log, ab_d_skip, ab_ssm_norm, ab_w_out, cd_w_in, cd_w_pool, cd_pool_scale, cd_w_out, moe_w_coarse, moe_b_coarse, moe_w_fine, moe_b_fine, moe_w_gate_up, moe_w_down, ln_g, ln_b):
    bp, tp, d = x_prompt.shape
    bs, ts, _ = x_sample.shape
    (wih, wil), ab_prm, (woh, wol) = _ab_prepare(ab_w_in, ab_w_gate2, ab_b_gate, ab_gla_norm, ab_conv_w,
                                                 ab_conv_b, ab_dt_bias, ab_a_log, ab_d_skip, ab_ssm_norm, ab_w_out)
    n_p, n_s = bp * tp, bs * ts
    n_pages = page_table.shape[1]
    past = n_pages * PAGE_SIZE
    assert past % MOBA_BLOCK == 0 and past // MOBA_BLOCK >= MOBA_TOPK and ts <= SAMPLE_PAD
    zeros = lambda *s: jnp.zeros(s, F32)
    ln = lambda l, j: (ln_g[l, j].reshape(1, d), ln_b[l, j].reshape(1, d))
    unpad = lambda rows: rows.reshape(bs, SAMPLE_PAD, -1)[:, :ts]
    assert n_s == V7X_LANES
    moe = lambda l, xa, xb: _moe_layer(xa, unpad(xb).reshape(n_s, d), moe_w_coarse[l], moe_b_coarse[l],
                                       moe_w_fine[l], moe_b_fine[l], moe_w_gate_up, moe_w_down, l, *ln(l, 1),
                                       tm=V7X_LANES,
                                       three_pass=l < DEPTH - 1)

    xp = x_prompt.reshape(n_p, d)
    xs = _pad_rows(x_sample, SAMPLE_PAD)
    proj_p = _proj(xp, wih, wil)
    oy_p, gla_p, ssm_p, conv_p = _scan_from_proj(
        proj_p, zeros(bp, H_A, DK_A, DV_A), zeros(bp, H_B, P_B, N_B), zeros(bp, CONV_W - 1, CONV_DIM), ab_prm,
        nb=bp, t_len=tp, tc=256, chunk=CHUNK, valid=256)
    xp = _outproj_ln(oy_p, xp, woh, wol, *ln(0, 0))
    proj_s = _proj(xs, wih, wil)
    oy_s, gla_s, ssm_s, conv_s = _scan_from_proj(
        proj_s, state_gla, state_ssm, state_conv, ab_prm,
        nb=bs, t_len=SAMPLE_PAD, tc=SAMPLE_PAD, chunk=SAMPLE_PAD, valid=ts)
    xs = _outproj_ln(oy_s, xs, woh, wol, *ln(0, 0))
    xp, xs = moe(0, xp, xs)

    wch, wcl = _split_w(cd_w_in)
    wph, wpl = _split_w(cd_w_pool)
    pscale = cd_pool_scale.reshape(1, D_C)
    wo_c, wo_d = _split_w(cd_w_out[:D_C]), _split_w(cd_w_out[D_C:])
    cos_p, sin_p, perm = _rope_tables(jnp.arange(tp))
    u, q, k, v = _cdproj(xp, wch, wcl, cos_p, sin_p, perm, n_rows=n_p, tm=256)
    pooled, pool_p = _pool(u, zeros(bp, POOL_BUF, D_C), wph, wpl, pscale, nb=bp, t_len=tp, tc=256, valid=256, start=0)
    att = _moba_prompt(q, k, v, nb=bp, t_len=tp)
    xp = _outproj2_ln(pooled, att, xp, wo_c, wo_d, *ln(1, 0), n_rows=n_p, tm=256)
    k_p, v_p = k.reshape(bp, tp, H_D, HD_D), v.reshape(bp, tp, H_D, HD_D)

    xs = _pad_rows(xs.reshape(bs, ts, d), SAMPLE_PAD)
    cos_s, sin_s, _ = _rope_tables(past + jnp.arange(SAMPLE_PAD))
    n_rows_s = bs * SAMPLE_PAD
    u, q, k, v = _cdproj(xs, wch, wcl, jnp.tile(cos_s, (bs, 1)), jnp.tile(sin_s, (bs, 1)), perm,
                         n_rows=n_rows_s, tm=256)
    pooled, pool_s = _pool(u, state_pool, wph, wpl, pscale, nb=bs, t_len=SAMPLE_PAD, tc=SAMPLE_PAD, valid=ts,
                           start=past)
    cache_kt, cache_vt = jnp.transpose(cache_k, (0, 2, 3, 1)), jnp.transpose(cache_v, (0, 2, 3, 1))
    picks = _select_blocks(q, _block_means(cache_kt, page_table))[:, :, :ts, :MOBA_TOPK]
    logical = picks[..., None] * PAGES_PER_BLOCK + jnp.arange(PAGES_PER_BLOCK)
    phys = page_table[jnp.arange(bs)[:, None, None, None, None], logical].reshape(-1)
    att = _moba_sample_attend(phys, q, k, v, cache_kt, cache_vt, db=bs, n_q=ts)
    xs = _outproj2_ln(pooled, att, xs, wo_c, wo_d, *ln(1, 0), n_rows=n_rows_s, tm=256)
    k_s, v_s = unpad(k).reshape(bs, ts, H_D, HD_D), unpad(v).reshape(bs, ts, H_D, HD_D)
    xp, xs = moe(1, xp, xs)

    return (xp.reshape(bp, tp, d), xs.reshape(bs, ts, d), gla_p, ssm_p, conv_p, pool_p, k_p, v_p,
            gla_s, ssm_s, conv_s, pool_s, k_s, v_s)
```

```python
import functools

import jax
import jax.numpy as jnp
import numpy as np
from jax import lax
from jax.experimental import pallas as pl
from jax.experimental.pallas import tpu as pltpu

F32 = jnp.float32
BF16 = jnp.bfloat16

D_MODEL = 1024
DEPTH = 2
PAGE_SIZE = 128
H_A, DK_A, DV_A = 4, 64, 128
GATE_RANK = 16
GATE_TAU = 16.0
CHUNK = 64
H_B, P_B, N_B, G_B = 8, 64, 64, 2
CONV_W = 4
D_B = H_B * P_B
CONV_DIM = D_B + 2 * G_B * N_B
POOL_WINDOWS = (2, 4, 8, 16)
POOL_GROUP = 128
D_C = len(POOL_WINDOWS) * POOL_GROUP
POOL_BUF = max(POOL_WINDOWS) - 1
H_D, HD_D = 8, 64
D_D = H_D * HD_D
MOBA_BLOCK = 256
MOBA_TOPK = 3
ROT_DIM = HD_D // 4
ROPE_THETA = 500000.0
N_GROUPS, EXP_PER_GROUP, TOP_FINE, D_EXPERT = 4, 8, 2, 256
N_EXPERTS = N_GROUPS * EXP_PER_GROUP
DN_ALPHA = (2 * DEPTH) ** 0.25
LN_EPS = 1e-5
RMS_EPS = 1e-6

V7X_LANES = 128
V7X_SUBLANES = 8
V7X_VMEM_LIMIT_BYTES = 56 * 1024 * 1024

NN = (((1,), (0,)), ((), ()))
NT = (((1,), (1,)), ((), ()))
TN = (((0,), (0,)), ((), ()))


def _params(n_axes):
    return pltpu.CompilerParams(dimension_semantics=("arbitrary",) * n_axes,
                                vmem_limit_bytes=V7X_VMEM_LIMIT_BYTES)


def _const_spec(shape):
    zeros = (0,) * len(shape)
    return pl.BlockSpec(shape, lambda *_: zeros, pipeline_mode=pl.Buffered(1))


def _split2(a):
    hi = a.astype(BF16)
    lo = (a - hi.astype(F32)).astype(BF16)
    return hi, lo


def _split3(a):
    hi = a.astype(BF16)
    r = a - hi.astype(F32)
    mid = r.astype(BF16)
    lo = (r - mid.astype(F32)).astype(BF16)
    return hi, mid, lo


def _mm(a, b, dims=NN):
    return lax.dot_general(a, b, dims, preferred_element_type=F32)


def _dot_hl(ah, al, bh, bl, dims=NN):
    return (_mm(ah, bl, dims) + _mm(al, bh, dims)) + _mm(ah, bh, dims)


def _dot3(a, b, dims=NN):
    ah, al = _split2(a)
    bh, bl = _split2(b)
    return _dot_hl(ah, al, bh, bl, dims)


def _dot3_cat(a, b, dims=NN):
    (ca,), (cb,) = dims[0]
    ah, al = _split2(a)
    bh, bl = _split2(b)
    return _mm(jnp.concatenate([ah, al, ah], axis=ca), jnp.concatenate([bh, bh, bl], axis=cb), dims)


def _dotx(a, e, dims=NN):
    h, m, l = _split3(a)
    return (_mm(l, e, dims) + _mm(m, e, dims)) + _mm(h, e, dims)


def _xdot(e, a, dims=NN):
    h, m, l = _split3(a)
    return (_mm(e, l, dims) + _mm(e, m, dims)) + _mm(e, h, dims)


def _silu(x):
    return x / (1.0 + jnp.exp(-x))


def _softplus(x):
    return jnp.maximum(x, 0.0) + jnp.log1p(jnp.exp(-jnp.abs(x)))


def _log_sigmoid(x):
    return jnp.minimum(x, 0.0) - jnp.log1p(jnp.exp(-jnp.abs(x)))


def _layer_norm(z, g, b):
    mu = jnp.mean(z, axis=-1, keepdims=True)
    zc = z - mu
    var = jnp.mean(zc * zc, axis=-1, keepdims=True)
    return zc * lax.rsqrt(var + LN_EPS) * g + b


def _iota(shape, dim):
    return lax.broadcasted_iota(jnp.int32, shape, dim)


def _proj_kernel(x_ref, wh_ref, wl_ref, o_ref, *, col_chunk):
    xh, xl = _split2(x_ref[...])
    m = o_ref.shape[1]
    for j in range(0, m, col_chunk):
        cs = slice(j, min(j + col_chunk, m))
        o_ref[:, cs] = _dot_hl(xh, xl, wh_ref[:, cs], wl_ref[:, cs])


def _proj(x, wh, wl, *, n_rows=None, tm=256, col_chunk=512):
    n = x.shape[0] if n_rows is None else n_rows
    k = x.shape[1]
    m = wh.shape[1]
    assert n % tm == 0
    return pl.pallas_call(
        functools.partial(_proj_kernel, col_chunk=col_chunk),
        grid=(n // tm,),
        in_specs=[pl.BlockSpec((tm, k), lambda i: (i, 0)), _const_spec((k, m)), _const_spec((k, m))],
        out_specs=pl.BlockSpec((tm, m), lambda i: (i, 0)),
        out_shape=jax.ShapeDtypeStruct((n, m), F32),
        compiler_params=_params(1),
        name="proj",
    )(x, wh, wl)


def _outproj_ln_kernel(a_ref, x_ref, wh_ref, wl_ref, g_ref, b_ref, o_ref):
    ah, al = _split2(a_ref[...])
    y = _dot_hl(ah, al, wh_ref[...], wl_ref[...])
    o_ref[...] = _layer_norm(DN_ALPHA * x_ref[...] + y, g_ref[...], b_ref[...])


def _outproj_ln(a, x, wh, wl, g, b, *, tm=256):
    n, k = a.shape
    d = wh.shape[1]
    assert n % tm == 0
    return pl.pallas_call(
        _outproj_ln_kernel,
        grid=(n // tm,),
        in_specs=[pl.BlockSpec((tm, k), lambda i: (i, 0)), pl.BlockSpec((tm, d), lambda i: (i, 0)),
                  _const_spec((k, d)), _const_spec((k, d)), _const_spec((1, d)), _const_spec((1, d))],
        out_specs=pl.BlockSpec((tm, d), lambda i: (i, 0)),
        out_shape=jax.ShapeDtypeStruct((n, d), F32),
        compiler_params=_params(1),
        name="outproj_ln",
    )(a, x, wh, wl, g, b)


SM_GLR0 = 0
SM_DT0 = GATE_RANK
CONV_HDR = V7X_SUBLANES


def _scan_kernel(qkvr_ref, z_ref, xbc_ref, sm_ref, s0_ref, h0_ref, c0_ref,
                 wg2h_ref, wg2l_ref, bg_ref, gnorm_ref, cw_ref, cb_ref, dtb_ref, alog_ref,
                 dskip_ref, snorm_ref,
                 oy_ref, s_out_ref, h_out_ref, c_out_ref,
                 st_sc, h_sc, ext_sc, o_sc, y_sc, *, tc, chunk, valid):
    t = pl.program_id(1)
    nt = pl.num_programs(1)
    n_chunks = tc // chunk
    shift = chunk.bit_length() - 1
    assert (1 << shift) == chunk and tc % chunk == 0

    @pl.when(t == 0)
    def _():
        st_sc[...] = s0_ref[0]
        h_sc[...] = h0_ref[0]
        ext_sc[0:CONV_HDR, :] = c0_ref[0]

    row = _iota((tc, tc), 0)
    col = _iota((tc, tc), 1)
    causal = ((row >> shift) == (col >> shift)) & (col <= row)
    l_tri = jnp.where(causal, 1.0, 0.0).astype(BF16)
    lane = _iota((tc, V7X_LANES), 1)
    rowv = _iota((tc, V7X_LANES), 0)
    dt_lane = (lane >= SM_DT0) & (lane < SM_DT0 + H_B)
    if valid < tc:
        row_ok = rowv < valid
        dt_lane = dt_lane & row_ok
    e8 = jnp.where((_iota((V7X_LANES, D_B), 1) >> 6) + SM_DT0 == _iota((V7X_LANES, D_B), 0), 1.0, 0.0).astype(BF16)

    sm = sm_ref[...]
    pre = _dot_hl(*_split2(sm), wg2h_ref[...], wg2l_ref[...])
    log_a = _log_sigmoid(pre + bg_ref[...]) / GATE_TAU
    if valid < tc:
        log_a = jnp.where(_iota((tc, H_A * DK_A), 0) < valid, log_a, 0.0)
    b_cum = _xdot(l_tri, log_a)
    dt_pad = jnp.where(dt_lane, _softplus(sm + dtb_ref[...]), 0.0)
    a_pad = -jnp.exp(alog_ref[...])
    la_pad = jnp.where(dt_lane, dt_pad * a_pad, 0.0)
    cum_pad = _xdot(l_tri, la_pad)
    dt_rep = _dotx(dt_pad, e8)
    cum_rep = _dotx(cum_pad, e8)

    ext_sc[CONV_HDR:CONV_HDR + tc, :] = xbc_ref[...]
    cw = cw_ref[...]
    acc = ext_sc[CONV_HDR - 3:CONV_HDR - 3 + tc, :] * cw[0:1, :]
    for i in range(1, CONV_W):
        acc = acc + ext_sc[CONV_HDR - 3 + i:CONV_HDR - 3 + i + tc, :] * cw[i:i + 1, :]
    xc = _silu(acc + cb_ref[...])

    @pl.when(t == nt - 1)
    def _():
        c_out_ref[0] = ext_sc[valid:valid + CONV_HDR, :]

    ext_sc[0:CONV_HDR, :] = ext_sc[tc:tc + CONV_HDR, :]

    xs = xc[:, 0:D_B]
    bm = xc[:, D_B:D_B + G_B * N_B]
    cm = xc[:, D_B + G_B * N_B:CONV_DIM]
    xdt = xs * dt_rep

    qkvr = qkvr_ref[...]
    q = qkvr[:, 0:H_A * DK_A] * (DK_A ** -0.5)
    k = qkvr[:, H_A * DK_A:2 * H_A * DK_A]
    v = qkvr[:, 2 * H_A * DK_A:2 * H_A * DK_A + H_A * DV_A]
    r = qkvr[:, 2 * H_A * DK_A + H_A * DV_A:]
    q_dec = q * jnp.exp(b_cum)
    k_dec = k * jnp.exp(-b_cum)
    for h in range(H_A):
        ks = slice(h * DK_A, (h + 1) * DK_A)
        vs = slice(h * DV_A, (h + 1) * DV_A)
        att = jnp.where(causal, _dot3_cat(q_dec[:, ks], k_dec[:, ks], NT), 0.0)
        o_sc[:, vs] = _dot3(att, v[:, vs])
    for c in range(n_chunks):
        rows = slice(c * chunk, (c + 1) * chunk)
        last = slice((c + 1) * chunk - 1, (c + 1) * chunk)
        b_last = b_cum[last, :]
        k_end = k[rows, :] * jnp.exp(b_last - b_cum[rows, :])
        dec = jnp.exp(b_last)
        for h in range(H_A):
            ks = slice(h * DK_A, (h + 1) * DK_A)
            vs = slice(h * DV_A, (h + 1) * DV_A)
            st = st_sc[h]
            o_sc[rows, vs] = o_sc[rows, vs] + _dot3_cat(q_dec[rows, ks], st, NT)
            st_sc[h] = st * dec[:, ks] + _dot3_cat(v[rows, vs], k_end[:, ks], TN)

    c_pieces = [p.astype(F32) for p in _split3(cum_pad)]
    in_group = lambda first: (lane >= first * SM_DT0) & (lane < first * SM_DT0 + 3 * SM_DT0) & ((lane & 15) < H_B)
    seg_x = jnp.where(in_group(4), 1.0, 0.0)
    seg_y = jnp.where(in_group(1), 1.0, 0.0)
    for n, piece in enumerate(c_pieces):
        seg_x = seg_x + (piece if n == 0 else pltpu.roll(piece, n * SM_DT0, 1))
        seg_y = seg_y - pltpu.roll(piece, (3 + n) * SM_DT0, 1)
    seg_y = seg_y.astype(BF16)
    for g in range(G_B):
        gs = slice(g * N_B, (g + 1) * N_B)
        cb = _dot3_cat(cm[:, gs], bm[:, gs], NT)
        for hh in range(g * (H_B // G_B), (g + 1) * (H_B // G_B)):
            ps = slice(hh * P_B, (hh + 1) * P_B)
            seg = _mm(jnp.where((lane & 15) == hh, seg_x, 0.0).astype(BF16), seg_y, NT)
            w = cb * jnp.exp(jnp.where(causal, seg, -jnp.inf))
            y_sc[:, ps] = _dot3(w, xdt[:, ps])
    for c in range(n_chunks):
        rows = slice(c * chunk, (c + 1) * chunk)
        last = slice((c + 1) * chunk - 1, (c + 1) * chunk)
        cum_c = cum_rep[rows, :]
        cum_l = cum_rep[last, :]
        e_cum = jnp.exp(cum_c)
        x_dec = xdt[rows, :] * jnp.exp(cum_l - cum_c)
        e_last = jnp.exp(cum_l)
        for hh in range(H_B):
            g = hh // (H_B // G_B)
            gs = slice(g * N_B, (g + 1) * N_B)
            ps = slice(hh * P_B, (hh + 1) * P_B)
            hs = h_sc[hh]
            y_sc[rows, ps] = y_sc[rows, ps] + _dot3_cat(cm[rows, gs], hs, NT) * e_cum[:, ps]
            h_sc[hh] = hs * e_last[:, ps] + _dot3_cat(x_dec[:, ps], bm[rows, gs], TN)

    gn = gnorm_ref[...]
    for h in range(H_A):
        vs = slice(h * DV_A, (h + 1) * DV_A)
        oh = o_sc[:, vs]
        oh = oh * lax.rsqrt(jnp.mean(oh * oh, axis=-1, keepdims=True) + RMS_EPS) * gn
        oy_ref[:, vs] = oh * _silu(r[:, vs])
    y = (y_sc[...] + dskip_ref[...] * xs) * _silu(z_ref[...])
    y = y * lax.rsqrt(jnp.mean(y * y, axis=-1, keepdims=True) + RMS_EPS) * snorm_ref[...]
    oy_ref[:, H_A * DV_A:] = y

    @pl.when(t == nt - 1)
    def _():
        s_out_ref[0] = st_sc[...]
        h_out_ref[0] = h_sc[...]


def _scan_call(proj, s0t, h0, c0, prm, *, specs, nb, t_len, tc, chunk, valid):
    nt = t_len // tc
    assert t_len % tc == 0 and (valid == tc or nt == 1)
    rowspec = lambda c: pl.BlockSpec((tc, c), lambda b, t: (b * nt + t, 0))
    stspec = lambda shp: pl.BlockSpec((1,) + shp, lambda b, t: (b,) + (0,) * len(shp))
    names = ("wg2h", "wg2l", "bg", "gnorm", "cw", "cb", "dtb", "alog", "dskip", "snorm")
    consts = [prm[n] for n in names]
    return pl.pallas_call(
        functools.partial(_scan_kernel, tc=tc, chunk=chunk, valid=valid),
        grid=(nb, nt),
        in_specs=list(specs)
                 + [stspec((H_A, DV_A, DK_A)), stspec((H_B, P_B, N_B)), stspec((CONV_HDR, CONV_DIM))]
                 + [_const_spec(c.shape) for c in consts],
        out_specs=[rowspec(H_A * DV_A + D_B), stspec((H_A, DV_A, DK_A)), stspec((H_B, P_B, N_B)),
                   stspec((CONV_HDR, CONV_DIM))],
        out_shape=[jax.ShapeDtypeStruct((nb * t_len, H_A * DV_A + D_B), F32),
                   jax.ShapeDtypeStruct((nb, H_A, DV_A, DK_A), F32),
                   jax.ShapeDtypeStruct((nb, H_B, P_B, N_B), F32),
                   jax.ShapeDtypeStruct((nb, CONV_HDR, CONV_DIM), F32)],
        scratch_shapes=[pltpu.VMEM((H_A, DV_A, DK_A), F32), pltpu.VMEM((H_B, P_B, N_B), F32),
                        pltpu.VMEM((tc + CONV_HDR, CONV_DIM), F32),
                        pltpu.VMEM((tc, H_A * DV_A), F32), pltpu.VMEM((tc, D_B), F32)],
        compiler_params=_params(2),
        name="gla_ssd_scan",
    )(proj, proj, proj, proj, s0t, h0, c0, *consts)


AB_QKVR = 2 * H_A * DK_A + 2 * H_A * DV_A
AB_XBC0 = AB_QKVR
AB_SM0 = AB_XBC0 + CONV_DIM
AB_Z0 = 2560
AB_COLS = AB_Z0 + D_B


def _split_w(w):
    hi = w.astype(BF16)
    lo = (w - hi.astype(F32)).astype(BF16)
    return hi, lo


def _lane_pad(v, start, width=V7X_LANES):
    out = jnp.zeros((1, width), F32)
    return out.at[0, start:start + v.shape[0]].set(v.astype(F32))


def _ab_prepare(w_in, w_gate2, b_gate, gla_norm, conv_w, conv_b, dt_bias, a_log, d_skip, ssm_norm, w_out):
    o = np.cumsum([0, H_A * DK_A, H_A * DK_A, H_A * DV_A, H_A * DV_A, GATE_RANK, D_B, CONV_DIM, H_B])
    q0, gl0, z0, xbc0, dt0, end = o[0], o[4], o[5], o[6], o[7], o[8]
    k_dim = w_in.shape[0]
    w = jnp.concatenate([
        w_in[:, q0:gl0], w_in[:, xbc0:dt0], w_in[:, gl0:z0], w_in[:, dt0:end],
        jnp.zeros((k_dim, AB_Z0 - AB_SM0 - GATE_RANK - H_B), F32), w_in[:, z0:xbc0]], axis=1)
    assert w.shape[1] == AB_COLS
    wg2 = jnp.zeros((V7X_LANES, H_A * DK_A), F32).at[0:GATE_RANK].set(w_gate2)
    wg2h, wg2l = _split_w(wg2)
    prm = dict(
        wg2h=wg2h, wg2l=wg2l, bg=b_gate.reshape(1, -1), gnorm=gla_norm.reshape(1, -1),
        cw=conv_w, cb=conv_b.reshape(1, -1), dtb=_lane_pad(dt_bias, SM_DT0), alog=_lane_pad(a_log, SM_DT0),
        dskip=jnp.repeat(d_skip, P_B).reshape(1, -1), snorm=ssm_norm.reshape(1, -1))
    return _split_w(w), prm, _split_w(w_out)


def _scan_from_proj(proj, s0, h0, c0, prm, *, nb, t_len, tc, chunk, valid):
    nt = t_len // tc
    s0t = jnp.swapaxes(s0, -1, -2)
    c0p = jnp.zeros((nb, CONV_HDR, CONV_DIM), F32).at[:, CONV_HDR - (CONV_W - 1):].set(c0)
    colspec = lambda width, start: pl.BlockSpec((tc, width), lambda b, t: (b * nt + t, start // width))
    oy, s_new, h_new, c_new = _scan_call(
        proj, s0t, h0, c0p, prm,
        specs=[colspec(AB_QKVR, 0), colspec(D_B, AB_Z0), colspec(CONV_DIM, AB_XBC0), colspec(V7X_LANES, AB_SM0)],
        nb=nb, t_len=t_len, tc=tc, chunk=chunk, valid=valid)
    return oy, jnp.swapaxes(s_new, -1, -2), h_new, c_new[:, CONV_HDR - (CONV_W - 1):]


RT_FINE0 = N_GROUPS
RT_HALF = V7X_LANES // 2
MOE_TILE = 256


def _two_source_specs(na, nb, tm, width):
    assert na % tm == 0 and nb % tm == 0
    na_t = na // tm
    return (pl.BlockSpec((tm, width), lambda i, *_: (jnp.minimum(i, na_t - 1), 0)),
            pl.BlockSpec((tm, width), lambda i, *_: (jnp.maximum(i - na_t, 0), 0)), na_t)


def _router_kernel(xa_ref, xb_ref, wh_ref, wl_ref, b_ref, id_ref, rank_ref, gate_ref, cnt_ref, base_sc, *, na_t):
    @pl.when(pl.program_id(0) == 0)
    def _():
        base_sc[...] = jnp.zeros(base_sc.shape, F32)

    xh, xl = _split2(jnp.where(pl.program_id(0) < na_t, xa_ref[...], xb_ref[...]))
    logits = _dot_hl(xh, xl, wh_ref[...], wl_ref[...]) + b_ref[...]
    shp = logits.shape
    lane = _iota(shp, 1)
    lane_f = lane.astype(F32)
    big = float(V7X_LANES)
    ninf = -jnp.inf

    def first_max(mask):
        v = jnp.max(jnp.where(mask, logits, ninf), axis=-1, keepdims=True)
        i = jnp.min(jnp.where(mask & (logits == v), lane_f, big), axis=-1, keepdims=True)
        return v, i

    is_c = lane < N_GROUPS
    mc, grp = first_max(is_c)
    p_grp = 1.0 / jnp.sum(jnp.where(is_c, jnp.exp(logits - mc), 0.0), axis=-1, keepdims=True)
    fine = (lane >= RT_FINE0) & (lane < RT_FINE0 + N_EXPERTS)
    cand = fine & (((lane - RT_FINE0) >> 3).astype(F32) == grp)
    v1, i1 = first_max(cand)
    v2, i2 = first_max(cand & (lane_f != i1))
    e = jnp.exp(v2 - v1)
    w1 = p_grp / (1.0 + e)
    w2 = p_grp * (e / (1.0 + e))
    first = lane < RT_HALF
    id_ref[...] = (jnp.where(first, i1, i2) - float(RT_FINE0)).astype(jnp.int32)
    gate_ref[...] = jnp.where(first, w1, w2)

    tm = shp[0]
    oh1 = jnp.where(lane_f == i1, 1.0, 0.0)
    oh2 = jnp.where(lane_f == i2, 1.0, 0.0)
    oh = oh1 + oh2
    earlier = jnp.where(_iota((tm, tm), 1) < _iota((tm, tm), 0), 1.0, 0.0).astype(BF16)
    before = _mm(earlier, oh.astype(BF16)) + base_sc[...]
    r1 = jnp.sum(oh1 * before, axis=-1, keepdims=True)
    r2 = jnp.sum(oh2 * before, axis=-1, keepdims=True)
    rank_ref[...] = jnp.where(first, r1, r2).astype(jnp.int32)
    base_sc[...] = base_sc[...] + jnp.sum(oh, axis=0, keepdims=True)
    cnt_ref[...] = base_sc[...]


def _router(xa, xb, wh, wl, b, *, tm):
    d = xa.shape[1]
    n = xa.shape[0] + xb.shape[0]
    spec_a, spec_b, na_t = _two_source_specs(xa.shape[0], xb.shape[0], tm, d)
    row = pl.BlockSpec((tm, V7X_LANES), lambda i: (i, 0))
    return pl.pallas_call(
        functools.partial(_router_kernel, na_t=na_t),
        grid=(n // tm,),
        in_specs=[spec_a, spec_b, _const_spec(wh.shape), _const_spec(wl.shape), _const_spec(b.shape)],
        out_specs=[row, row, row, pl.BlockSpec((1, V7X_LANES), lambda i: (0, 0))],
        out_shape=[jax.ShapeDtypeStruct((n, V7X_LANES), jnp.int32), jax.ShapeDtypeStruct((n, V7X_LANES), jnp.int32),
                   jax.ShapeDtypeStruct((n, V7X_LANES), F32), jax.ShapeDtypeStruct((1, V7X_LANES), F32)],
        scratch_shapes=[pltpu.VMEM((1, V7X_LANES), F32)],
        compiler_params=_params(1),
        name="moe_router",
    )(xa, xb, wh, wl, b)


DMA_UNROLL = 8


def _pos_copy(pos_hbm, pos_sm, sem_p, step, slot):
    return pltpu.make_async_copy(pos_hbm.at[pl.ds(step, 1)], pos_sm.at[slot], sem_p.at[slot])


def _dispatch_kernel(zf_ref, xa_hbm, xb_hbm, pos_hbm, xs_hbm, pos_sm, zbuf, sem_p, sem_z, sem_s, *, na_t, tm):
    i = pl.program_id(0)
    n = pl.num_programs(0)
    tile = zbuf.shape[0]
    n_tiles = xs_hbm.shape[0] // tile
    slot = lax.rem(i, 2)

    def zero_copy(j):
        return pltpu.make_async_copy(zbuf, xs_hbm.at[pl.ds(pl.multiple_of(j * tile, tile), tile)], sem_z)

    @pl.when(i == 0)
    def _():
        _pos_copy(pos_hbm, pos_sm, sem_p, 0, 0).start()
        zbuf[...] = jnp.zeros(zbuf.shape, F32)

        def z_start(j, c):
            @pl.when(zf_ref[j] > 0)
            def _():
                zero_copy(j).start()
            return c

        def z_wait(j, c):
            @pl.when(zf_ref[j] > 0)
            def _():
                zero_copy(j).wait()
            return c

        lax.fori_loop(0, n_tiles, z_start, 0)
        lax.fori_loop(0, n_tiles, z_wait, 0)

    @pl.when(i + 1 < n)
    def _():
        _pos_copy(pos_hbm, pos_sm, sem_p, i + 1, 1 - slot).start()

    _pos_copy(pos_hbm, pos_sm, sem_p, i, slot).wait()

    def scatter_rows(x_hbm, row0):
        def issue(r, c):
            for k in range(TOP_FINE):
                dst = pos_sm[slot, 0, k * tm + r]
                pltpu.make_async_copy(x_hbm.at[pl.ds(row0 + r, 1)], xs_hbm.at[pl.ds(dst, 1)], sem_s).start()
            return c

        lax.fori_loop(0, tm, issue, 0, unroll=DMA_UNROLL)

    def wait_tile():
        for _ in range(TOP_FINE):
            pltpu.make_async_copy(xa_hbm.at[pl.ds(0, tm)], xs_hbm.at[pl.ds(0, tm)], sem_s).wait()

    @pl.when(i < na_t)
    def _():
        scatter_rows(xa_hbm, i * tm)

    @pl.when(i >= na_t)
    def _():
        scatter_rows(xb_hbm, (i - na_t) * tm)

    @pl.when(i > 0)
    def _():
        wait_tile()

    @pl.when(i == n - 1)
    def _():
        wait_tile()


def _dispatch(xa, xb, pos_tab, zero_flag, *, n_tiles, tm):
    d = xa.shape[1]
    n = xa.shape[0] + xb.shape[0]
    assert xa.shape[0] % tm == 0 and xb.shape[0] % tm == 0
    na_t = xa.shape[0] // tm
    any_spec = pl.BlockSpec(memory_space=pl.ANY)
    grid_spec = pltpu.PrefetchScalarGridSpec(
        num_scalar_prefetch=1,
        grid=(n // tm,),
        in_specs=[any_spec, any_spec, any_spec],
        out_specs=any_spec,
        scratch_shapes=[pltpu.SMEM((2, 1, TOP_FINE * tm), jnp.int32), pltpu.VMEM((MOE_TILE, d), F32),
                        pltpu.SemaphoreType.DMA((2,)), pltpu.SemaphoreType.DMA(()), pltpu.SemaphoreType.DMA(())],
    )
    return pl.pallas_call(
        functools.partial(_dispatch_kernel, na_t=na_t, tm=tm),
        grid_spec=grid_spec,
        out_shape=jax.ShapeDtypeStruct((n_tiles * MOE_TILE, d), F32),
        compiler_params=_params(1),
        name="moe_dispatch",
    )(zero_flag, xa, xb, pos_tab)


def _expert_kernel(te_ref, xs_ref, wgu_ref, wd_ref, ys_ref, *, three_pass):
    if three_pass:
        xh, xl = _split2(xs_ref[...])
        wh, wl = _split2(wgu_ref[0])
        hgu = _dot_hl(xh, xl, wh, wl)
    else:
        hgu = _mm(xs_ref[...].astype(BF16), wgu_ref[0].astype(BF16))
    act = _silu(hgu[:, :D_EXPERT]) * hgu[:, D_EXPERT:]
    if three_pass:
        ys_ref[...] = _dot3(act, wd_ref[0])
    else:
        ys_ref[...] = _mm(act.astype(BF16), wd_ref[0].astype(BF16))


def _experts(xs, w_gate_up, w_down, tile_expert, *, three_pass):
    p, d = xs.shape
    f2 = w_gate_up.shape[-1]
    row = pl.BlockSpec((MOE_TILE, d), lambda i, te: (i, 0))
    grid_spec = pltpu.PrefetchScalarGridSpec(
        num_scalar_prefetch=1,
        grid=(p // MOE_TILE,),
        in_specs=[row, pl.BlockSpec((1, d, f2), lambda i, te: (te[i], 0, 0)),
                  pl.BlockSpec((1, f2 // 2, d), lambda i, te: (te[i], 0, 0))],
        out_specs=row,
    )
    return pl.pallas_call(
        functools.partial(_expert_kernel, three_pass=three_pass),
        grid_spec=grid_spec,
        out_shape=jax.ShapeDtypeStruct((p, d), F32),
        compiler_params=_params(1),
        name="moe_experts",
    )(tile_expert, xs, w_gate_up, w_down)


def _combine_ln_kernel(xa_ref, xb_ref, gate_ref, g_ref, b_ref, pos_hbm, ys_hbm, oa_ref, ob_ref,
                       pos_sm, ybuf, sem_p, sem_y, *, na_t):
    i = pl.program_id(0)
    n = pl.num_programs(0)
    tm = xa_ref.shape[0]
    slot = lax.rem(i, 2)

    def issue_gathers(s):
        def issue(r, c):
            for k in range(TOP_FINE):
                src = pos_sm[s, 0, k * tm + r]
                pltpu.make_async_copy(ys_hbm.at[pl.ds(src, 1)], ybuf.at[s, pl.ds(k * tm + r, 1)],
                                      sem_y.at[s]).start()
            return c

        lax.fori_loop(0, tm, issue, 0, unroll=DMA_UNROLL)

    @pl.when(i == 0)
    def _():
        _pos_copy(pos_hbm, pos_sm, sem_p, 0, 0).start()
        _pos_copy(pos_hbm, pos_sm, sem_p, 0, 0).wait()
        issue_gathers(0)

        @pl.when(n > 1)
        def _():
            _pos_copy(pos_hbm, pos_sm, sem_p, 1, 1).start()

    @pl.when(i + 1 < n)
    def _():
        _pos_copy(pos_hbm, pos_sm, sem_p, i + 1, 1 - slot).wait()
        issue_gathers(1 - slot)

    @pl.when(i + 2 < n)
    def _():
        _pos_copy(pos_hbm, pos_sm, sem_p, i + 2, slot).start()

    pltpu.make_async_copy(ys_hbm.at[pl.ds(0, TOP_FINE * tm)], ybuf.at[slot], sem_y.at[slot]).wait()
    gate = gate_ref[...]
    y = ybuf[slot]
    moe = gate[:, 0:1] * y[0:tm] + gate[:, RT_HALF:RT_HALF + 1] * y[tm:2 * tm]

    @pl.when(i < na_t)
    def _():
        oa_ref[...] = _layer_norm(DN_ALPHA * xa_ref[...] + moe, g_ref[...], b_ref[...])

    @pl.when(i >= na_t)
    def _():
        ob_ref[...] = _layer_norm(DN_ALPHA * xb_ref[...] + moe, g_ref[...], b_ref[...])


def _combine_ln(xa, xb, ys, pos_tab, gate, g, b, *, tm):
    d = xa.shape[1]
    na, nb = xa.shape[0], xb.shape[0]
    assert TOP_FINE == 2 and nb == tm
    spec_a, spec_b, na_t = _two_source_specs(na, nb, tm, d)
    any_spec = pl.BlockSpec(memory_space=pl.ANY)
    return pl.pallas_call(
        functools.partial(_combine_ln_kernel, na_t=na_t),
        grid=((na + nb) // tm,),
        in_specs=[spec_a, spec_b, pl.BlockSpec((tm, V7X_LANES), lambda i: (i, 0)), _const_spec((1, d)),
                  _const_spec((1, d)), any_spec, any_spec],
        out_specs=[spec_a, spec_b],
        out_shape=[jax.ShapeDtypeStruct((na, d), F32), jax.ShapeDtypeStruct((nb, d), F32)],
        scratch_shapes=[pltpu.SMEM((2, 1, TOP_FINE * tm), jnp.int32), pltpu.VMEM((2, TOP_FINE * tm, d), F32),
                        pltpu.SemaphoreType.DMA((2,)), pltpu.SemaphoreType.DMA((2,))],
        compiler_params=_params(1),
        name="moe_combine_ln",
    )(xa, xb, gate, g, b, pos_tab, ys)


def _slots_kernel(id_ref, rank_ref, base_ref, pos_ref, *, tm):
    lane = _iota((tm, V7X_LANES), 1)
    lane_f = lane.astype(F32)
    base = base_ref[...]
    for t in range(pos_ref.shape[0]):
        rows = slice(t * tm, (t + 1) * tm)
        ids = id_ref[rows, :].astype(F32)

        def first_row(choice, ids=ids):
            hit = lane_f == ids[:, choice:choice + 1] + float(RT_FINE0)
            return jnp.sum(jnp.where(hit, base, 0.0), axis=-1, keepdims=True)

        pos = rank_ref[rows, :].astype(F32) + jnp.where(lane < RT_HALF, first_row(0), first_row(RT_HALF))
        pos_t = jnp.transpose(pos)
        pos_ref[t] = jnp.concatenate([pos_t[0:1, :], pos_t[RT_HALF:RT_HALF + 1, :]], axis=1).astype(jnp.int32)


def _slots(ids, rank, base, *, tm):
    n = ids.shape[0]
    assert tm == V7X_LANES and n % tm == 0
    per_step = max(k for k in range(1, 9) if (n // tm) % k == 0)
    row = pl.BlockSpec((per_step * tm, V7X_LANES), lambda i: (i, 0))
    out = pl.pallas_call(
        functools.partial(_slots_kernel, tm=tm),
        grid=(n // tm // per_step,),
        in_specs=[row, row, _const_spec((1, V7X_LANES))],
        out_specs=pl.BlockSpec((per_step, 1, TOP_FINE * tm), lambda i: (i, 0, 0)),
        out_shape=jax.ShapeDtypeStruct((n // tm, 1, TOP_FINE * tm), jnp.int32),
        compiler_params=_params(1),
        name="moe_slots",
    )(ids, rank, base)
    return out.reshape(n // tm, TOP_FINE * tm)


def _moe_plan(ids, rank, cnt, n, tm):
    counts = cnt[0, RT_FINE0:RT_FINE0 + N_EXPERTS].astype(jnp.int32)
    tiles_per = (counts + MOE_TILE - 1) // MOE_TILE
    e_idx = jnp.arange(N_EXPERTS)
    tile_end = tiles_per @ (e_idx[:, None] <= e_idx[None, :]).astype(jnp.int32)
    tile_beg = tile_end - tiles_per
    n_tiles = -(-TOP_FINE * n // MOE_TILE) + N_EXPERTS
    ti = jnp.arange(n_tiles)
    te = jnp.minimum(jnp.sum((tile_end[None, :] <= ti[:, None]).astype(jnp.int32), axis=1), N_EXPERTS - 1)
    is_last = jnp.any((ti[:, None] == tile_end[None, :] - 1) & (tiles_per[None, :] > 0), axis=1)
    zero_flag = (is_last | (ti >= tile_end[-1])).astype(jnp.int32)
    base = jnp.zeros((1, V7X_LANES), F32).at[0, RT_FINE0:RT_FINE0 + N_EXPERTS].set(
        (tile_beg * MOE_TILE).astype(F32))
    pos_tab = _slots(ids, rank, base, tm=tm)
    return te.astype(jnp.int32), zero_flag, pos_tab, n_tiles


def _moe_prepare(w_coarse, b_coarse, w_fine, b_fine):
    d = w_coarse.shape[0]
    wf = jnp.transpose(w_fine, (1, 0, 2)).reshape(d, N_EXPERTS)
    w = jnp.zeros((d, V7X_LANES), F32).at[:, :N_GROUPS].set(w_coarse).at[:, RT_FINE0:RT_FINE0 + N_EXPERTS].set(wf)
    b = jnp.zeros((1, V7X_LANES), F32).at[0, :N_GROUPS].set(b_coarse)
    b = b.at[0, RT_FINE0:RT_FINE0 + N_EXPERTS].set(b_fine.reshape(-1))
    return _split_w(w), b


def _moe_layer(xa, xb, w_coarse, b_coarse, w_fine, b_fine, w_gate_up, w_down, layer, g, b, *, tm, three_pass):
    d = xa.shape[1]
    n = xa.shape[0] + xb.shape[0]
    (wrh, wrl), br = _moe_prepare(w_coarse, b_coarse, w_fine, b_fine)
    ids, rank, gate, cnt = _router(xa, xb, wrh, wrl, br, tm=tm)
    te, zero_flag, pos_tab, n_tiles = _moe_plan(ids, rank, cnt, n, tm)
    xs = _dispatch(xa, xb, pos_tab, zero_flag, n_tiles=n_tiles, tm=tm)
    ys = _experts(xs, w_gate_up.reshape(-1, d, 2 * D_EXPERT), w_down.reshape(-1, D_EXPERT, d),
                  te + layer * N_EXPERTS, three_pass=three_pass)
    return _combine_ln(xa, xb, ys, pos_tab, gate, g, b, tm=tm)


def _cdproj_kernel(x_ref, wh_ref, wl_ref, cos_ref, sin_ref, perm_ref, u_ref, q_ref, k_ref, v_ref):
    xh, xl = _split2(x_ref[...])

    def col(j):
        cs = slice(j * D_C, (j + 1) * D_C)
        return _dot_hl(xh, xl, wh_ref[:, cs], wl_ref[:, cs])

    u_ref[...] = col(0)
    for j, ref in ((1, q_ref), (2, k_ref)):
        t = col(j)
        ref[...] = t * cos_ref[...] + _dotx(t, perm_ref[...]) * sin_ref[...]
    v_ref[...] = col(3)


def _cdproj(x, wh, wl, cos_t, sin_t, perm, *, n_rows, tm):
    k = x.shape[1]
    assert D_C == D_D and n_rows % tm == 0 and cos_t.shape[0] % tm == 0
    nt = cos_t.shape[0] // tm
    row = pl.BlockSpec((tm, D_D), lambda i: (i, 0))
    tab = pl.BlockSpec((tm, D_D), lambda i: (i % nt, 0))
    return pl.pallas_call(
        _cdproj_kernel,
        grid=(n_rows // tm,),
        in_specs=[pl.BlockSpec((tm, k), lambda i: (i, 0)), _const_spec(wh.shape), _const_spec(wl.shape),
                  tab, tab, _const_spec(perm.shape)],
        out_specs=[row] * 4,
        out_shape=[jax.ShapeDtypeStruct((n_rows, D_D), F32)] * 4,
        compiler_params=_params(1),
        name="cd_proj_rope",
    )(x, wh, wl, cos_t, sin_t, perm)


def _rope_tables(pos):
    half = ROT_DIM // 2
    inv = ROPE_THETA ** (-jnp.arange(half, dtype=F32) / half)
    ang = pos.astype(F32)[:, None] * inv
    cos, sin = jnp.cos(ang), jnp.sin(ang)
    n = pos.shape[0]
    rest = HD_D - ROT_DIM
    cos_h = jnp.concatenate([cos, cos, jnp.ones((n, rest), F32)], axis=-1)
    sin_h = jnp.concatenate([-sin, sin, jnp.zeros((n, rest), F32)], axis=-1)
    perm = np.zeros((D_D, D_D), np.float32)
    for dst in range(D_D):
        j = dst % HD_D
        if j < half:
            perm[dst + half, dst] = 1.0
        elif j < ROT_DIM:
            perm[dst - half, dst] = 1.0
    return jnp.tile(cos_h, (1, H_D)), jnp.tile(sin_h, (1, H_D)), jnp.asarray(perm, BF16)


POOL_HDR = 16


def _pool_kernel(u_ref, b0_ref, wph_ref, wpl_ref, sc_ref, o_ref, bo_ref, ext_sc, *, tc, valid, start):
    t = pl.program_id(1)
    nt = pl.num_programs(1)

    @pl.when(t == 0)
    def _():
        ext_sc[0:POOL_HDR, :] = b0_ref[0]

    ext_sc[POOL_HDR:POOL_HDR + tc, :] = u_ref[...]
    u = u_ref[...]
    rowi = _iota((tc, POOL_HDR + tc), 0) + POOL_HDR
    colj = _iota((tc, POOL_HDR + tc), 1)
    pos = (start + t * tc + _iota((tc, POOL_GROUP), 0)).astype(F32)
    for gi, w in enumerate(POOL_WINDOWS):
        ls = slice(gi * POOL_GROUP, (gi + 1) * POOL_GROUP)
        band = jnp.where((colj <= rowi) & (colj > rowi - w), 1.0, 0.0).astype(BF16)
        win = _xdot(band, ext_sc[:, ls])
        cnt = jnp.minimum(float(w), pos + 1.0)
        dh, dl = _split2(win / cnt - u[:, ls])
        o_ref[:, ls] = _dot_hl(dh, dl, wph_ref[gi], wpl_ref[gi]) * sc_ref[:, ls]

    @pl.when(t == nt - 1)
    def _():
        bo_ref[0] = ext_sc[valid:valid + POOL_HDR, :]

    ext_sc[0:POOL_HDR, :] = ext_sc[tc:tc + POOL_HDR, :]


def _pool(u, buf, wph, wpl, scale, *, nb, t_len, tc, valid, start):
    nt = t_len // tc
    assert t_len % tc == 0 and (valid == tc or nt == 1)
    b0 = jnp.zeros((nb, POOL_HDR, D_C), F32).at[:, POOL_HDR - POOL_BUF:].set(buf)
    row = pl.BlockSpec((tc, D_C), lambda b, t: (b * nt + t, 0))
    st = pl.BlockSpec((1, POOL_HDR, D_C), lambda b, t: (b, 0, 0))
    pooled, bnew = pl.pallas_call(
        functools.partial(_pool_kernel, tc=tc, valid=valid, start=start),
        grid=(nb, nt),
        in_specs=[row, st, _const_spec(wph.shape), _const_spec(wpl.shape), _const_spec(scale.shape)],
        out_specs=[row, st],
        out_shape=[jax.ShapeDtypeStruct((nb * t_len, D_C), F32), jax.ShapeDtypeStruct((nb, POOL_HDR, D_C), F32)],
        scratch_shapes=[pltpu.VMEM((POOL_HDR + tc, D_C), F32)],
        compiler_params=_params(2),
        name="pool_mix",
    )(u, b0, wph, wpl, scale)
    return pooled, bnew[:, POOL_HDR - POOL_BUF:]


def _top_blocks(gate, lane_f, valid, n_pick):
    sel = jnp.zeros(gate.shape, jnp.bool_)
    g = jnp.where(valid, gate, -jnp.inf)
    picks = []
    for _ in range(n_pick):
        v = jnp.max(g, axis=-1, keepdims=True)
        idx = jnp.min(jnp.where(g == v, lane_f, float(V7X_LANES)), axis=-1, keepdims=True)
        hit = (lane_f == idx) & valid
        sel = sel | hit
        g = jnp.where(hit, -jnp.inf, g)
        picks.append(idx)
    return sel, picks


def _cat_pieces(a, b, c):
    return jnp.concatenate([a, b, c, jnp.zeros_like(a)], axis=1)


MOBA_TRIP = 4


def _top_rows(gate, row_f, valid, n_pick):
    sel = jnp.zeros(gate.shape, F32)
    g = jnp.where(valid, gate, -jnp.inf)
    for _ in range(n_pick):
        v = jnp.max(g, axis=0, keepdims=True)
        idx = jnp.min(jnp.where(g == v, row_f, float(V7X_LANES)), axis=0, keepdims=True)
        hit = (row_f == idx) & valid
        sel = jnp.where(hit, 1.0, sel)
        g = jnp.where(hit, -jnp.inf, g)
    return sel


def _moba_prompt_kernel(q_ref, k_ref, v_ref, o_ref, kmean_sc, kcat_sc, vt_sc, sel_sc):
    i = pl.program_id(2)
    blk = MOBA_BLOCK
    t_len = k_ref.shape[0]
    n_heads = V7X_LANES // HD_D
    g_rows = sel_sc.shape[1]
    scale = HD_D ** -0.5

    @pl.when(i == 0)
    def _():
        ind = jnp.where((_iota((V7X_LANES, t_len), 1) >> 8) == _iota((V7X_LANES, t_len), 0), 1.0, 0.0)
        kmean_sc[...] = _xdot(ind.astype(BF16), k_ref[...]) * (1.0 / blk)
        ones_row = jnp.where(_iota((HD_D, blk), 0) == 0, 1.0, 0.0).astype(BF16)
        for n in range(t_len // blk):
            rows = slice(n * blk, (n + 1) * blk)
            kh, kl = _split2(k_ref[rows, :])
            vt = jnp.transpose(v_ref[rows, :]).astype(BF16)
            for hh in range(n_heads):
                hs = slice(hh * HD_D, (hh + 1) * HD_D)
                kcat_sc[hh, rows, :] = _cat_pieces(kh[:, hs], kh[:, hs], kl[:, hs])
                vt_sc[hh, n, 0:HD_D, :] = vt[hs, :]
                vt_sc[hh, n, HD_D:2 * HD_D, :] = ones_row

    def scores_t(hh, qcat, j):
        return _mm(kcat_sc[hh, pl.ds(pl.multiple_of(j * blk, blk), blk), :], qcat, NT)

    def weighted_t(hh, p, j):
        return _mm(vt_sc[hh, j], p.astype(BF16))

    q = q_ref[...]
    key_le_query = _iota((blk, blk), 0) <= _iota((blk, blk), 1)
    brow = _iota((g_rows, blk), 0)
    qcats, state = [], []
    for hh in range(n_heads):
        hs = slice(hh * HD_D, (hh + 1) * HD_D)
        qh = q[:, hs]
        gate_t = _dot3_cat(kmean_sc[0:g_rows, hs], qh, NT)
        sel_sc[hh] = _top_rows(gate_t, brow.astype(F32), brow < i, MOBA_TOPK)
        q_hi, q_lo = _split2(qh * scale)
        qcats.append(_cat_pieces(q_hi, q_lo, q_hi))
        s = jnp.where(key_le_query, scores_t(hh, qcats[hh], i), -jnp.inf)
        m = jnp.max(s, axis=0, keepdims=True)
        state += [m, weighted_t(hh, jnp.exp(s - m), i)]

    def body(jj, carry):
        out = []
        for hh in range(n_heads):
            m, acc = carry[2 * hh:2 * hh + 2]
            js = [jnp.minimum(MOBA_TRIP * jj + t, i) for t in range(MOBA_TRIP)]
            ss = []
            for j in js:
                on = sel_sc[hh, pl.ds(j, 1), :] > 0.0
                ss.append(jnp.where(on, scores_t(hh, qcats[hh], j), -jnp.inf))
            top = functools.reduce(jnp.maximum, ss)
            m2 = jnp.maximum(m, jnp.max(top, axis=0, keepdims=True))
            new = functools.reduce(lambda x, y: x + y,
                                   [weighted_t(hh, jnp.exp(sj - m2), j) for sj, j in zip(ss, js)])
            out += [m2, jnp.exp(m - m2) * acc + new]
        return tuple(out)

    state = lax.fori_loop(0, (i + MOBA_TRIP - 1) >> (MOBA_TRIP.bit_length() - 1), body, tuple(state))
    for hh in range(n_heads):
        acc = jnp.transpose(state[2 * hh + 1])
        o_ref[:, hh * HD_D:(hh + 1) * HD_D] = acc[:, 0:HD_D] / acc[:, HD_D:HD_D + 1]


def _moba_prompt(q, k, v, *, nb, t_len):
    blk = MOBA_BLOCK
    nq = t_len // blk
    assert t_len % blk == 0 and nq <= V7X_LANES
    n_heads = V7X_LANES // HD_D
    g_rows = -(-nq // V7X_SUBLANES) * V7X_SUBLANES
    qspec = pl.BlockSpec((blk, V7X_LANES), lambda b, hp, i: (b * nq + i, hp))
    kspec = pl.BlockSpec((t_len, V7X_LANES), lambda b, hp, i: (b, hp))
    return pl.pallas_call(
        _moba_prompt_kernel,
        grid=(nb, D_D // V7X_LANES, nq),
        in_specs=[qspec, kspec, kspec],
        out_specs=qspec,
        out_shape=jax.ShapeDtypeStruct((nb * t_len, D_D), F32),
        scratch_shapes=[pltpu.VMEM((V7X_LANES, V7X_LANES), F32),
                        pltpu.VMEM((n_heads, t_len, 4 * HD_D), BF16),
                        pltpu.VMEM((n_heads, nq, 2 * HD_D, blk), BF16),
                        pltpu.VMEM((n_heads, g_rows, blk), F32)],
        compiler_params=_params(3),
        name="moba_prompt",
    )(q, k, v)


KM_PAGES = 32
PAGES_PER_BLOCK = MOBA_BLOCK // PAGE_SIZE


def _kmean_kernel(pt_ref, *refs):
    pages, o_ref = refs[:KM_PAGES], refs[KM_PAGES]
    c = pl.program_id(1)
    blocks_per_step = KM_PAGES // PAGES_PER_BLOCK

    @pl.when(c == 0)
    def _():
        o_ref[...] = jnp.zeros(o_ref.shape, F32)

    acc = o_ref[0]
    lane = _iota(acc.shape, 2)
    for blk in range(blocks_per_step):
        tot = pages[blk * PAGES_PER_BLOCK][0]
        for p in range(1, PAGES_PER_BLOCK):
            tot = tot + pages[blk * PAGES_PER_BLOCK + p][0]
        mean = jnp.sum(tot, axis=-1, keepdims=True) * (1.0 / MOBA_BLOCK)
        acc = jnp.where(lane == c * blocks_per_step + blk, mean, acc)
    o_ref[0] = acc


def _block_means(cache_kt, page_table):
    db, n_pages = page_table.shape
    n_blocks = n_pages // PAGES_PER_BLOCK
    assert n_pages % KM_PAGES == 0
    page_spec = lambda j: pl.BlockSpec((1, H_D, HD_D, PAGE_SIZE), lambda b, c, pt: (pt[b, c * KM_PAGES + j], 0, 0, 0))
    grid_spec = pltpu.PrefetchScalarGridSpec(
        num_scalar_prefetch=1,
        grid=(db, n_pages // KM_PAGES),
        in_specs=[page_spec(j) for j in range(KM_PAGES)],
        out_specs=pl.BlockSpec((1, H_D, HD_D, n_blocks), lambda b, c, pt: (b, 0, 0, 0)),
    )
    return pl.pallas_call(
        _kmean_kernel,
        grid_spec=grid_spec,
        out_shape=jax.ShapeDtypeStruct((db, H_D, HD_D, n_blocks), F32),
        compiler_params=_params(2),
        name="moba_block_means",
    )(page_table, *([cache_kt] * KM_PAGES))


def _select_kernel(q_ref, km_ref, o_ref):
    q = q_ref[...]
    n_blk = km_ref.shape[3]
    lane_f = _iota((q.shape[0], n_blk), 1).astype(F32)
    out_lane = _iota((q.shape[0], V7X_LANES), 1)
    for h in range(H_D):
        gate = _dot3(q[:, h * HD_D:(h + 1) * HD_D], km_ref[0, h])
        _, picks = _top_blocks(gate, lane_f, lane_f >= 0.0, MOBA_TOPK)
        res = jnp.zeros((q.shape[0], V7X_LANES), F32)
        for r, idx in enumerate(picks):
            res = jnp.where(out_lane == r, idx, res)
        o_ref[0, h] = res.astype(jnp.int32)


def _select_blocks(q, kmean_t):
    db = kmean_t.shape[0]
    return pl.pallas_call(
        _select_kernel,
        grid=(db,),
        in_specs=[pl.BlockSpec((SAMPLE_PAD, D_D), lambda b: (b, 0)),
                  pl.BlockSpec((1,) + kmean_t.shape[1:], lambda b: (b, 0, 0, 0))],
        out_specs=pl.BlockSpec((1, H_D, SAMPLE_PAD, V7X_LANES), lambda b: (b, 0, 0, 0)),
        out_shape=jax.ShapeDtypeStruct((db, H_D, SAMPLE_PAD, V7X_LANES), jnp.int32),
        compiler_params=_params(1),
        name="moba_select",
    )(q, kmean_t)


HEADS_PER_STEP = V7X_LANES // HD_D


def _attend_kernel(phys_ref, q_ref, kn_ref, vn_ref, ck_hbm, cv_hbm, o_ref, kbuf, vbuf, sem, *, n_q):
    b = pl.program_id(0)
    hp = pl.program_id(1)
    n_hp = pl.num_programs(1)
    step = b * n_hp + hp
    slot = lax.rem(step, 2)
    scale = HD_D ** -0.5

    def fetch(fb, fhp, fslot):
        for hh in range(HEADS_PER_STEP):
            h = fhp * HEADS_PER_STEP + hh
            for s in range(n_q):
                for r in range(MOBA_TOPK):
                    for pg in range(PAGES_PER_BLOCK):
                        flat = (((fb * H_D + h) * n_q + s) * MOBA_TOPK + r) * PAGES_PER_BLOCK + pg
                        cols = pl.ds((r * PAGES_PER_BLOCK + pg) * PAGE_SIZE, PAGE_SIZE)
                        page = phys_ref[flat]
                        for which, (src, dst) in enumerate(((ck_hbm, kbuf), (cv_hbm, vbuf))):
                            pltpu.make_async_copy(src.at[page, h], dst.at[fslot, hh, s, :, cols],
                                                  sem.at[fslot, which]).start()

    @pl.when(step == 0)
    def _():
        fetch(b, hp, slot)

    @pl.when(step + 1 < pl.num_programs(0) * n_hp)
    def _():
        wrap = hp + 1 == n_hp
        fetch(jnp.where(wrap, b + 1, b), jnp.where(wrap, 0, hp + 1), 1 - slot)

    pltpu.make_async_copy(vbuf.at[1 - slot], kbuf.at[slot], sem.at[slot, 0]).wait()
    pltpu.make_async_copy(kbuf.at[1 - slot], vbuf.at[slot], sem.at[slot, 1]).wait()

    q = q_ref[...]
    rows_n = q.shape[0]
    rowi = _iota((rows_n, rows_n), 0)
    coli = _iota((rows_n, rows_n), 1)
    own_ok = (coli <= rowi) & (coli < n_q)
    rsel = _iota((rows_n, HD_D), 0)
    for hh in range(HEADS_PER_STEP):
        hs = slice(hh * HD_D, (hh + 1) * HD_D)
        qh = q[:, hs]
        s_own = jnp.where(own_ok, _dot3(qh, kn_ref[...][:, hs], NT) * scale, -jnp.inf)
        m_own = jnp.max(s_own, axis=-1, keepdims=True)
        out = jnp.zeros((rows_n, HD_D), F32)
        for s in range(n_q):
            s_sel = _dot3_cat(qh, kbuf[slot, hh, s]) * scale
            m = jnp.maximum(m_own, jnp.max(s_sel, axis=-1, keepdims=True))
            p_sel = jnp.exp(s_sel - m)
            p_own = jnp.exp(s_own - m)
            l = jnp.sum(p_sel, axis=-1, keepdims=True) + jnp.sum(p_own, axis=-1, keepdims=True)
            o_s = (_dot3(p_sel, vbuf[slot, hh, s], NT) + _dot3(p_own, vn_ref[...][:, hs])) / l
            out = jnp.where(rsel == s, o_s, out)
        o_ref[:, hs] = out


def _moba_sample_attend(phys, q, k_new, v_new, cache_kt, cache_vt, *, db, n_q):
    row = pl.BlockSpec((SAMPLE_PAD, V7X_LANES), lambda b, hp, ph: (b, hp))
    any_spec = pl.BlockSpec(memory_space=pl.ANY)
    buf = pltpu.VMEM((2, HEADS_PER_STEP, n_q, HD_D, MOBA_TOPK * MOBA_BLOCK), F32)
    grid_spec = pltpu.PrefetchScalarGridSpec(
        num_scalar_prefetch=1,
        grid=(db, D_D // V7X_LANES),
        in_specs=[row, row, row, any_spec, any_spec],
        out_specs=row,
        scratch_shapes=[buf, buf, pltpu.SemaphoreType.DMA((2, 2))],
    )
    return pl.pallas_call(
        functools.partial(_attend_kernel, n_q=n_q),
        grid_spec=grid_spec,
        out_shape=jax.ShapeDtypeStruct((db * SAMPLE_PAD, D_D), F32),
        compiler_params=_params(2),
        name="moba_sample_attend",
    )(phys, q, k_new, v_new, cache_kt, cache_vt)


def _outproj2_ln_kernel(a1_ref, a2_ref, x_ref, w1h_ref, w1l_ref, w2h_ref, w2l_ref, g_ref, b_ref, o_ref):
    y = _dot_hl(*_split2(a1_ref[...]), w1h_ref[...], w1l_ref[...])
    y = y + _dot_hl(*_split2(a2_ref[...]), w2h_ref[...], w2l_ref[...])
    o_ref[...] = _layer_norm(DN_ALPHA * x_ref[...] + y, g_ref[...], b_ref[...])


def _outproj2_ln(a1, a2, x, w1, w2, g, b, *, n_rows, tm):
    d = x.shape[1]
    assert n_rows % tm == 0
    row = lambda c: pl.BlockSpec((tm, c), lambda i: (i, 0))
    consts = [w1[0], w1[1], w2[0], w2[1], g, b]
    return pl.pallas_call(
        _outproj2_ln_kernel,
        grid=(n_rows // tm,),
        in_specs=[row(a1.shape[1]), row(a2.shape[1]), row(d)] + [_const_spec(c.shape) for c in consts],
        out_specs=row(d),
        out_shape=jax.ShapeDtypeStruct((n_rows, d), F32),
        compiler_params=_params(1),
        name="outproj2_ln",
    )(a1, a2, x, *consts)


SAMPLE_PAD = 16


def _pad_rows(x, t_pad):
    nb, t, d = x.shape
    return jnp.zeros((nb, t_pad, d), x.dtype).at[:, :t].set(x).reshape(nb * t_pad, d)


def kernel(x_prompt, x_sample, state_gla, state_ssm, state_conv, state_pool, cache_k, cache_v, page_table, ab_w_in, ab_w_gate2, ab_b_gate, ab_gla_norm, ab_conv_w, ab_conv_b, ab_dt_bias, ab_a_log, ab_d_skip, ab_ssm_norm, ab_w_out, cd_w_in, cd_w_pool, cd_pool_scale, cd_w_out, moe_w_coarse, moe_b_coarse, moe_w_fine, moe_b_fine, moe_w_gate_up, moe_w_down, ln_g, ln_b):
    bp, tp, d = x_prompt.shape
    bs, ts, _ = x_sample.shape
    (wih, wil), ab_prm, (woh, wol) = _ab_prepare(ab_w_in, ab_w_gate2, ab_b_gate, ab_gla_norm, ab_conv_w,
                                                 ab_conv_b, ab_dt_bias, ab_a_log, ab_d_skip, ab_ssm_norm, ab_w_out)
    n_p, n_s = bp * tp, bs * ts
    n_pages = page_table.shape[1]
    past = n_pages * PAGE_SIZE
    assert past % MOBA_BLOCK == 0 and past // MOBA_BLOCK >= MOBA_TOPK and ts <= SAMPLE_PAD
    zeros = lambda *s: jnp.zeros(s, F32)
    ln = lambda l, j: (ln_g[l, j].reshape(1, d), ln_b[l, j].reshape(1, d))
    unpad = lambda rows: rows.reshape(bs, SAMPLE_PAD, -1)[:, :ts]
    assert n_s == V7X_LANES
    moe = lambda l, xa, xb: _moe_layer(xa, unpad(xb).reshape(n_s, d), moe_w_coarse[l], moe_b_coarse[l],
                                       moe_w_fine[l], moe_b_fine[l], moe_w_gate_up, moe_w_down, l, *ln(l, 1),
                                       tm=V7X_LANES,
                                       three_pass=l < DEPTH - 1)

    xp = x_prompt.reshape(n_p, d)
    xs = _pad_rows(x_sample, SAMPLE_PAD)
    proj_p = _proj(xp, wih, wil)
    oy_p, gla_p, ssm_p, conv_p = _scan_from_proj(
        proj_p, zeros(bp, H_A, DK_A, DV_A), zeros(bp, H_B, P_B, N_B), zeros(bp, CONV_W - 1, CONV_DIM), ab_prm,
        nb=bp, t_len=tp, tc=256, chunk=CHUNK, valid=256)
    xp = _outproj_ln(oy_p, xp, woh, wol, *ln(0, 0))
    proj_s = _proj(xs, wih, wil)
    oy_s, gla_s, ssm_s, conv_s = _scan_from_proj(
        proj_s, state_gla, state_ssm, state_conv, ab_prm,
        nb=bs, t_len=SAMPLE_PAD, tc=SAMPLE_PAD, chunk=SAMPLE_PAD, valid=ts)
    xs = _outproj_ln(oy_s, xs, woh, wol, *ln(0, 0))
    xp, xs = moe(0, xp, xs)

    wch, wcl = _split_w(cd_w_in)
    wph, wpl = _split_w(cd_w_pool)
    pscale = cd_pool_scale.reshape(1, D_C)
    wo_c, wo_d = _split_w(cd_w_out[:D_C]), _split_w(cd_w_out[D_C:])
    cos_p, sin_p, perm = _rope_tables(jnp.arange(tp))
    u, q, k, v = _cdproj(xp, wch, wcl, cos_p, sin_p, perm, n_rows=n_p, tm=256)
    pooled, pool_p = _pool(u, zeros(bp, POOL_BUF, D_C), wph, wpl, pscale, nb=bp, t_len=tp, tc=256, valid=256, start=0)
    att = _moba_prompt(q, k, v, nb=bp, t_len=tp)
    xp = _outproj2_ln(pooled, att, xp, wo_c, wo_d, *ln(1, 0), n_rows=n_p, tm=256)
    k_p, v_p = k.reshape(bp, tp, H_D, HD_D), v.reshape(bp, tp, H_D, HD_D)

    xs = _pad_rows(xs.reshape(bs, ts, d), SAMPLE_PAD)
    cos_s, sin_s, _ = _rope_tables(past + jnp.arange(SAMPLE_PAD))
    n_rows_s = bs * SAMPLE_PAD
    u, q, k, v = _cdproj(xs, wch, wcl, jnp.tile(cos_s, (bs, 1)), jnp.tile(sin_s, (bs, 1)), perm,
                         n_rows=n_rows_s, tm=256)
    pooled, pool_s = _pool(u, state_pool, wph, wpl, pscale, nb=bs, t_len=SAMPLE_PAD, tc=SAMPLE_PAD, valid=ts,
                           start=past)
    cache_kt, cache_vt = jnp.transpose(cache_k, (0, 2, 3, 1)), jnp.transpose(cache_v, (0, 2, 3, 1))
    picks = _select_blocks(q, _block_means(cache_kt, page_table))[:, :, :ts, :MOBA_TOPK]
    logical = picks[..., None] * PAGES_PER_BLOCK + jnp.arange(PAGES_PER_BLOCK)
    phys = page_table[jnp.arange(bs)[:, None, None, None, None], logical].reshape(-1)
    att = _moba_sample_attend(phys, q, k, v, cache_kt, cache_vt, db=bs, n_q=ts)
    xs = _outproj2_ln(pooled, att, xs, wo_c, wo_d, *ln(1, 0), n_rows=n_rows_s, tm=256)
    k_s, v_s = unpad(k).reshape(bs, ts, H_D, HD_D), unpad(v).reshape(bs, ts, H_D, HD_D)
    xp, xs = moe(1, xp, xs)

    return (xp.reshape(bp, tp, d), xs.reshape(bs, ts, d), gla_p, ssm_p, conv_p, pool_p, k_p, v_p,
            gla_s, ssm_s, conv_s, pool_s, k_s, v_s)
```

```python
import functools

import jax
import jax.numpy as jnp
import numpy as np
from jax import lax
from jax.experimental import pallas as pl
from jax.experimental.pallas import tpu as pltpu

F32 = jnp.float32
BF16 = jnp.bfloat16

D_MODEL = 1024
DEPTH = 2
PAGE_SIZE = 128
H_A, DK_A, DV_A = 4, 64, 128
GATE_RANK = 16
GATE_TAU = 16.0
CHUNK = 64
H_B, P_B, N_B, G_B = 8, 64, 64, 2
CONV_W = 4
D_B = H_B * P_B
CONV_DIM = D_B + 2 * G_B * N_B
POOL_WINDOWS = (2, 4, 8, 16)
POOL_GROUP = 128
D_C = len(POOL_WINDOWS) * POOL_GROUP
POOL_BUF = max(POOL_WINDOWS) - 1
H_D, HD_D = 8, 64
D_D = H_D * HD_D
MOBA_BLOCK = 256
MOBA_TOPK = 3
ROT_DIM = HD_D // 4
ROPE_THETA = 500000.0
N_GROUPS, EXP_PER_GROUP, TOP_FINE, D_EXPERT = 4, 8, 2, 256
N_EXPERTS = N_GROUPS * EXP_PER_GROUP
DN_ALPHA = (2 * DEPTH) ** 0.25
LN_EPS = 1e-5
RMS_EPS = 1e-6

V7X_LANES = 128
V7X_SUBLANES = 8
V7X_VMEM_LIMIT_BYTES = 56 * 1024 * 1024

NN = (((1,), (0,)), ((), ()))
NT = (((1,), (1,)), ((), ()))
TN = (((0,), (0,)), ((), ()))


def _params(n_axes):
    return pltpu.CompilerParams(dimension_semantics=("arbitrary",) * n_axes,
                                vmem_limit_bytes=V7X_VMEM_LIMIT_BYTES)


def _const_spec(shape):
    zeros = (0,) * len(shape)
    return pl.BlockSpec(shape, lambda *_: zeros, pipeline_mode=pl.Buffered(1))


def _split2(a):
    hi = a.astype(BF16)
    lo = (a - hi.astype(F32)).astype(BF16)
    return hi, lo


def _split3(a):
    hi = a.astype(BF16)
    r = a - hi.astype(F32)
    mid = r.astype(BF16)
    lo = (r - mid.astype(F32)).astype(BF16)
    return hi, mid, lo


def _mm(a, b, dims=NN):
    return lax.dot_general(a, b, dims, preferred_element_type=F32)


def _dot_hl(ah, al, bh, bl, dims=NN):
    return (_mm(ah, bl, dims) + _mm(al, bh, dims)) + _mm(ah, bh, dims)


def _dot3(a, b, dims=NN):
    ah, al = _split2(a)
    bh, bl = _split2(b)
    return _dot_hl(ah, al, bh, bl, dims)


def _dot3_cat(a, b, dims=NN):
    (ca,), (cb,) = dims[0]
    ah, al = _split2(a)
    bh, bl = _split2(b)
    return _mm(jnp.concatenate([ah, al, ah], axis=ca), jnp.concatenate([bh, bh, bl], axis=cb), dims)


def _dotx(a, e, dims=NN):
    h, m, l = _split3(a)
    return (_mm(l, e, dims) + _mm(m, e, dims)) + _mm(h, e, dims)


def _xdot(e, a, dims=NN):
    h, m, l = _split3(a)
    return (_mm(e, l, dims) + _mm(e, m, dims)) + _mm(e, h, dims)


def _silu(x):
    return x / (1.0 + jnp.exp(-x))


def _softplus(x):
    return jnp.maximum(x, 0.0) + jnp.log1p(jnp.exp(-jnp.abs(x)))


def _log_sigmoid(x):
    return jnp.minimum(x, 0.0) - jnp.log1p(jnp.exp(-jnp.abs(x)))


def _layer_norm(z, g, b):
    mu = jnp.mean(z, axis=-1, keepdims=True)
    zc = z - mu
    var = jnp.mean(zc * zc, axis=-1, keepdims=True)
    return zc * lax.rsqrt(var + LN_EPS) * g + b


def _iota(shape, dim):
    return lax.broadcasted_iota(jnp.int32, shape, dim)


def _proj_kernel(x_ref, wh_ref, wl_ref, o_ref, *, col_chunk):
    xh, xl = _split2(x_ref[...])
    m = o_ref.shape[1]
    for j in range(0, m, col_chunk):
        cs = slice(j, min(j + col_chunk, m))
        o_ref[:, cs] = _dot_hl(xh, xl, wh_ref[:, cs], wl_ref[:, cs])


def _proj(x, wh, wl, *, n_rows=None, tm=256, col_chunk=512):
    n = x.shape[0] if n_rows is None else n_rows
    k = x.shape[1]
    m = wh.shape[1]
    assert n % tm == 0
    return pl.pallas_call(
        functools.partial(_proj_kernel, col_chunk=col_chunk),
        grid=(n // tm,),
        in_specs=[pl.BlockSpec((tm, k), lambda i: (i, 0)), _const_spec((k, m)), _const_spec((k, m))],
        out_specs=pl.BlockSpec((tm, m), lambda i: (i, 0)),
        out_shape=jax.ShapeDtypeStruct((n, m), F32),
        compiler_params=_params(1),
        name="proj",
    )(x, wh, wl)


def _outproj_ln_kernel(a_ref, x_ref, wh_ref, wl_ref, g_ref, b_ref, o_ref):
    ah, al = _split2(a_ref[...])
    y = _dot_hl(ah, al, wh_ref[...], wl_ref[...])
    o_ref[...] = _layer_norm(DN_ALPHA * x_ref[...] + y, g_ref[...], b_ref[...])


def _outproj_ln(a, x, wh, wl, g, b, *, tm=256):
    n, k = a.shape
    d = wh.shape[1]
    assert n % tm == 0
    return pl.pallas_call(
        _outproj_ln_kernel,
        grid=(n // tm,),
        in_specs=[pl.BlockSpec((tm, k), lambda i: (i, 0)), pl.BlockSpec((tm, d), lambda i: (i, 0)),
                  _const_spec((k, d)), _const_spec((k, d)), _const_spec((1, d)), _const_spec((1, d))],
        out_specs=pl.BlockSpec((tm, d), lambda i: (i, 0)),
        out_shape=jax.ShapeDtypeStruct((n, d), F32),
        compiler_params=_params(1),
        name="outproj_ln",
    )(a, x, wh, wl, g, b)


SM_GLR0 = 0
SM_DT0 = GATE_RANK
CONV_HDR = V7X_SUBLANES


def _scan_kernel(qkvr_ref, z_ref, xbc_ref, sm_ref, s0_ref, h0_ref, c0_ref,
                 wg2h_ref, wg2l_ref, bg_ref, gnorm_ref, cw_ref, cb_ref, dtb_ref, alog_ref,
                 dskip_ref, snorm_ref,
                 oy_ref, s_out_ref, h_out_ref, c_out_ref,
                 st_sc, h_sc, ext_sc, o_sc, y_sc, *, tc, chunk, valid):
    t = pl.program_id(1)
    nt = pl.num_programs(1)
    n_chunks = tc // chunk
    shift = chunk.bit_length() - 1
    assert (1 << shift) == chunk and tc % chunk == 0

    @pl.when(t == 0)
    def _():
        st_sc[...] = s0_ref[0]
        h_sc[...] = h0_ref[0]
        ext_sc[0:CONV_HDR, :] = c0_ref[0]

    row = _iota((tc, tc), 0)
    col = _iota((tc, tc), 1)
    causal = ((row >> shift) == (col >> shift)) & (col <= row)
    l_tri = jnp.where(causal, 1.0, 0.0).astype(BF16)
    lane = _iota((tc, V7X_LANES), 1)
    rowv = _iota((tc, V7X_LANES), 0)
    dt_lane = (lane >= SM_DT0) & (lane < SM_DT0 + H_B)
    if valid < tc:
        row_ok = rowv < valid
        dt_lane = dt_lane & row_ok
    e8 = jnp.where((_iota((V7X_LANES, D_B), 1) >> 6) + SM_DT0 == _iota((V7X_LANES, D_B), 0), 1.0, 0.0).astype(BF16)

    sm = sm_ref[...]
    pre = _dot_hl(*_split2(sm), wg2h_ref[...], wg2l_ref[...])
    log_a = _log_sigmoid(pre + bg_ref[...]) / GATE_TAU
    if valid < tc:
        log_a = jnp.where(_iota((tc, H_A * DK_A), 0) < valid, log_a, 0.0)
    b_cum = _xdot(l_tri, log_a)
    dt_pad = jnp.where(dt_lane, _softplus(sm + dtb_ref[...]), 0.0)
    a_pad = -jnp.exp(alog_ref[...])
    la_pad = jnp.where(dt_lane, dt_pad * a_pad, 0.0)
    cum_pad = _xdot(l_tri, la_pad)
    dt_rep = _dotx(dt_pad, e8)
    cum_rep = _dotx(cum_pad, e8)

    ext_sc[CONV_HDR:CONV_HDR + tc, :] = xbc_ref[...]
    cw = cw_ref[...]
    acc = ext_sc[CONV_HDR - 3:CONV_HDR - 3 + tc, :] * cw[0:1, :]
    for i in range(1, CONV_W):
        acc = acc + ext_sc[CONV_HDR - 3 + i:CONV_HDR - 3 + i + tc, :] * cw[i:i + 1, :]
    xc = _silu(acc + cb_ref[...])

    @pl.when(t == nt - 1)
    def _():
        c_out_ref[0] = ext_sc[valid:valid + CONV_HDR, :]

    ext_sc[0:CONV_HDR, :] = ext_sc[tc:tc + CONV_HDR, :]

    xs = xc[:, 0:D_B]
    bm = xc[:, D_B:D_B + G_B * N_B]
    cm = xc[:, D_B + G_B * N_B:CONV_DIM]
    xdt = xs * dt_rep

    qkvr = qkvr_ref[...]
    q = qkvr[:, 0:H_A * DK_A] * (DK_A ** -0.5)
    k = qkvr[:, H_A * DK_A:2 * H_A * DK_A]
    v = qkvr[:, 2 * H_A * DK_A:2 * H_A * DK_A + H_A * DV_A]
    r = qkvr[:, 2 * H_A * DK_A + H_A * DV_A:]
    q_dec = q * jnp.exp(b_cum)
    k_dec = k * jnp.exp(-b_cum)
    for h in range(H_A):
        ks = slice(h * DK_A, (h + 1) * DK_A)
        vs = slice(h * DV_A, (h + 1) * DV_A)
        att = jnp.where(causal, _dot3_cat(q_dec[:, ks], k_dec[:, ks], NT), 0.0)
        o_sc[:, vs] = _dot3(att, v[:, vs])
    for c in range(n_chunks):
        rows = slice(c * chunk, (c + 1) * chunk)
        last = slice((c + 1) * chunk - 1, (c + 1) * chunk)
        b_last = b_cum[last, :]
        k_end = k[rows, :] * jnp.exp(b_last - b_cum[rows, :])
        dec = jnp.exp(b_last)
        for h in range(H_A):
            ks = slice(h * DK_A, (h + 1) * DK_A)
            vs = slice(h * DV_A, (h + 1) * DV_A)
            st = st_sc[h]
            o_sc[rows, vs] = o_sc[rows, vs] + _dot3_cat(q_dec[rows, ks], st, NT)
            st_sc[h] = st * dec[:, ks] + _dot3_cat(v[rows, vs], k_end[:, ks], TN)

    c_pieces = [p.astype(F32) for p in _split3(cum_pad)]
    in_group = lambda first: (lane >= first * SM_DT0) & (lane < first * SM_DT0 + 3 * SM_DT0) & ((lane & 15) < H_B)
    seg_x = jnp.where(in_group(4), 1.0, 0.0)
    seg_y = jnp.where(in_group(1), 1.0, 0.0)
    for n, piece in enumerate(c_pieces):
        seg_x = seg_x + (piece if n == 0 else pltpu.roll(piece, n * SM_DT0, 1))
        seg_y = seg_y - pltpu.roll(piece, (3 + n) * SM_DT0, 1)
    seg_y = seg_y.astype(BF16)
    for g in range(G_B):
        gs = slice(g * N_B, (g + 1) * N_B)
        cb = _dot3_cat(cm[:, gs], bm[:, gs], NT)
        for hh in range(g * (H_B // G_B), (g + 1) * (H_B // G_B)):
            ps = slice(hh * P_B, (hh + 1) * P_B)
            seg = _mm(jnp.where((lane & 15) == hh, seg_x, 0.0).astype(BF16), seg_y, NT)
            w = cb * jnp.exp(jnp.where(causal, seg, -jnp.inf))
            y_sc[:, ps] = _dot3(w, xdt[:, ps])
    for c in range(n_chunks):
        rows = slice(c * chunk, (c + 1) * chunk)
        last = slice((c + 1) * chunk - 1, (c + 1) * chunk)
        cum_c = cum_rep[rows, :]
        cum_l = cum_rep[last, :]
        e_cum = jnp.exp(cum_c)
        x_dec = xdt[rows, :] * jnp.exp(cum_l - cum_c)
        e_last = jnp.exp(cum_l)
        for hh in range(H_B):
            g = hh // (H_B // G_B)
            gs = slice(g * N_B, (g + 1) * N_B)
            ps = slice(hh * P_B, (hh + 1) * P_B)
            hs = h_sc[hh]
            y_sc[rows, ps] = y_sc[rows, ps] + _dot3_cat(cm[rows, gs], hs, NT) * e_cum[:, ps]
            h_sc[hh] = hs * e_last[:, ps] + _dot3_cat(x_dec[:, ps], bm[rows, gs], TN)

    gn = gnorm_ref[...]
    for h in range(H_A):
        vs = slice(h * DV_A, (h + 1) * DV_A)
        oh = o_sc[:, vs]
        oh = oh * lax.rsqrt(jnp.mean(oh * oh, axis=-1, keepdims=True) + RMS_EPS) * gn
        oy_ref[:, vs] = oh * _silu(r[:, vs])
    y = (y_sc[...] + dskip_ref[...] * xs) * _silu(z_ref[...])
    y = y * lax.rsqrt(jnp.mean(y * y, axis=-1, keepdims=True) + RMS_EPS) * snorm_ref[...]
    oy_ref[:, H_A * DV_A:] = y

    @pl.when(t == nt - 1)
    def _():
        s_out_ref[0] = st_sc[...]
        h_out_ref[0] = h_sc[...]


def _scan_call(proj, s0t, h0, c0, prm, *, specs, nb, t_len, tc, chunk, valid):
    nt = t_len // tc
    assert t_len % tc == 0 and (valid == tc or nt == 1)
    rowspec = lambda c: pl.BlockSpec((tc, c), lambda b, t: (b * nt + t, 0))
    stspec = lambda shp: pl.BlockSpec((1,) + shp, lambda b, t: (b,) + (0,) * len(shp))
    names = ("wg2h", "wg2l", "bg", "gnorm", "cw", "cb", "dtb", "alog", "dskip", "snorm")
    consts = [prm[n] for n in names]
    return pl.pallas_call(
        functools.partial(_scan_kernel, tc=tc, chunk=chunk, valid=valid),
        grid=(nb, nt),
        in_specs=list(specs)
                 + [stspec((H_A, DV_A, DK_A)), stspec((H_B, P_B, N_B)), stspec((CONV_HDR, CONV_DIM))]
                 + [_const_spec(c.shape) for c in consts],
        out_specs=[rowspec(H_A * DV_A + D_B), stspec((H_A, DV_A, DK_A)), stspec((H_B, P_B, N_B)),
                   stspec((CONV_HDR, CONV_DIM))],
        out_shape=[jax.ShapeDtypeStruct((nb * t_len, H_A * DV_A + D_B), F32),
                   jax.ShapeDtypeStruct((nb, H_A, DV_A, DK_A), F32),
                   jax.ShapeDtypeStruct((nb, H_B, P_B, N_B), F32),
                   jax.ShapeDtypeStruct((nb, CONV_HDR, CONV_DIM), F32)],
        scratch_shapes=[pltpu.VMEM((H_A, DV_A, DK_A), F32), pltpu.VMEM((H_B, P_B, N_B), F32),
                        pltpu.VMEM((tc + CONV_HDR, CONV_DIM), F32),
                        pltpu.VMEM((tc, H_A * DV_A), F32), pltpu.VMEM((tc, D_B), F32)],
        compiler_params=_params(2),
        name="gla_ssd_scan",
    )(proj, proj, proj, proj, s0t, h0, c0, *consts)


AB_QKVR = 2 * H_A * DK_A + 2 * H_A * DV_A
AB_XBC0 = AB_QKVR
AB_SM0 = AB_XBC0 + CONV_DIM
AB_Z0 = 2560
AB_COLS = AB_Z0 + D_B


def _split_w(w):
    hi = w.astype(BF16)
    lo = (w - hi.astype(F32)).astype(BF16)
    return hi, lo


def _lane_pad(v, start, width=V7X_LANES):
    out = jnp.zeros((1, width), F32)
    return out.at[0, start:start + v.shape[0]].set(v.astype(F32))


def _ab_prepare(w_in, w_gate2, b_gate, gla_norm, conv_w, conv_b, dt_bias, a_log, d_skip, ssm_norm, w_out):
    o = np.cumsum([0, H_A * DK_A, H_A * DK_A, H_A * DV_A, H_A * DV_A, GATE_RANK, D_B, CONV_DIM, H_B])
    q0, gl0, z0, xbc0, dt0, end = o[0], o[4], o[5], o[6], o[7], o[8]
    k_dim = w_in.shape[0]
    w = jnp.concatenate([
        w_in[:, q0:gl0], w_in[:, xbc0:dt0], w_in[:, gl0:z0], w_in[:, dt0:end],
        jnp.zeros((k_dim, AB_Z0 - AB_SM0 - GATE_RANK - H_B), F32), w_in[:, z0:xbc0]], axis=1)
    assert w.shape[1] == AB_COLS
    wg2 = jnp.zeros((V7X_LANES, H_A * DK_A), F32).at[0:GATE_RANK].set(w_gate2)
    wg2h, wg2l = _split_w(wg2)
    prm = dict(
        wg2h=wg2h, wg2l=wg2l, bg=b_gate.reshape(1, -1), gnorm=gla_norm.reshape(1, -1),
        cw=conv_w, cb=conv_b.reshape(1, -1), dtb=_lane_pad(dt_bias, SM_DT0), alog=_lane_pad(a_log, SM_DT0),
        dskip=jnp.repeat(d_skip, P_B).reshape(1, -1), snorm=ssm_norm.reshape(1, -1))
    return _split_w(w), prm, _split_w(w_out)


def _scan_from_proj(proj, s0, h0, c0, prm, *, nb, t_len, tc, chunk, valid):
    nt = t_len // tc
    s0t = jnp.swapaxes(s0, -1, -2)
    c0p = jnp.zeros((nb, CONV_HDR, CONV_DIM), F32).at[:, CONV_HDR - (CONV_W - 1):].set(c0)
    colspec = lambda width, start: pl.BlockSpec((tc, width), lambda b, t: (b * nt + t, start // width))
    oy, s_new, h_new, c_new = _scan_call(
        proj, s0t, h0, c0p, prm,
        specs=[colspec(AB_QKVR, 0), colspec(D_B, AB_Z0), colspec(CONV_DIM, AB_XBC0), colspec(V7X_LANES, AB_SM0)],
        nb=nb, t_len=t_len, tc=tc, chunk=chunk, valid=valid)
    return oy, jnp.swapaxes(s_new, -1, -2), h_new, c_new[:, CONV_HDR - (CONV_W - 1):]


RT_FINE0 = N_GROUPS
RT_HALF = V7X_LANES // 2
MOE_TILE = 256


def _two_source_specs(na, nb, tm, width):
    assert na % tm == 0 and nb % tm == 0
    na_t = na // tm
    return (pl.BlockSpec((tm, width), lambda i, *_: (jnp.minimum(i, na_t - 1), 0)),
            pl.BlockSpec((tm, width), lambda i, *_: (jnp.maximum(i - na_t, 0), 0)), na_t)


def _router_kernel(xa_ref, xb_ref, wh_ref, wl_ref, b_ref, id_ref, rank_ref, gate_ref, cnt_ref, base_sc, *, na_t):
    @pl.when(pl.program_id(0) == 0)
    def _():
        base_sc[...] = jnp.zeros(base_sc.shape, F32)

    xh, xl = _split2(jnp.where(pl.program_id(0) < na_t, xa_ref[...], xb_ref[...]))
    logits = _dot_hl(xh, xl, wh_ref[...], wl_ref[...]) + b_ref[...]
    shp = logits.shape
    lane = _iota(shp, 1)
    lane_f = lane.astype(F32)
    big = float(V7X_LANES)
    ninf = -jnp.inf

    def first_max(mask):
        v = jnp.max(jnp.where(mask, logits, ninf), axis=-1, keepdims=True)
        i = jnp.min(jnp.where(mask & (logits == v), lane_f, big), axis=-1, keepdims=True)
        return v, i

    is_c = lane < N_GROUPS
    mc, grp = first_max(is_c)
    p_grp = 1.0 / jnp.sum(jnp.where(is_c, jnp.exp(logits - mc), 0.0), axis=-1, keepdims=True)
    fine = (lane >= RT_FINE0) & (lane < RT_FINE0 + N_EXPERTS)
    cand = fine & (((lane - RT_FINE0) >> 3).astype(F32) == grp)
    v1, i1 = first_max(cand)
    v2, i2 = first_max(cand & (lane_f != i1))
    e = jnp.exp(v2 - v1)
    w1 = p_grp / (1.0 + e)
    w2 = p_grp * (e / (1.0 + e))
    first = lane < RT_HALF
    id_ref[...] = (jnp.where(first, i1, i2) - float(RT_FINE0)).astype(jnp.int32)
    gate_ref[...] = jnp.where(first, w1, w2)

    tm = shp[0]
    oh1 = jnp.where(lane_f == i1, 1.0, 0.0)
    oh2 = jnp.where(lane_f == i2, 1.0, 0.0)
    oh = oh1 + oh2
    earlier = jnp.where(_iota((tm, tm), 1) < _iota((tm, tm), 0), 1.0, 0.0).astype(BF16)
    before = _mm(earlier, oh.astype(BF16)) + base_sc[...]
    r1 = jnp.sum(oh1 * before, axis=-1, keepdims=True)
    r2 = jnp.sum(oh2 * before, axis=-1, keepdims=True)
    rank_ref[...] = jnp.where(first, r1, r2).astype(jnp.int32)
    base_sc[...] = base_sc[...] + jnp.sum(oh, axis=0, keepdims=True)
    cnt_ref[...] = base_sc[...]


def _router(xa, xb, wh, wl, b, *, tm):
    d = xa.shape[1]
    n = xa.shape[0] + xb.shape[0]
    spec_a, spec_b, na_t = _two_source_specs(xa.shape[0], xb.shape[0], tm, d)
    row = pl.BlockSpec((tm, V7X_LANES), lambda i: (i, 0))
    return pl.pallas_call(
        functools.partial(_router_kernel, na_t=na_t),
        grid=(n // tm,),
        in_specs=[spec_a, spec_b, _const_spec(wh.shape), _const_spec(wl.shape), _const_spec(b.shape)],
        out_specs=[row, row, row, pl.BlockSpec((1, V7X_LANES), lambda i: (0, 0))],
        out_shape=[jax.ShapeDtypeStruct((n, V7X_LANES), jnp.int32), jax.ShapeDtypeStruct((n, V7X_LANES), jnp.int32),
                   jax.ShapeDtypeStruct((n, V7X_LANES), F32), jax.ShapeDtypeStruct((1, V7X_LANES), F32)],
        scratch_shapes=[pltpu.VMEM((1, V7X_LANES), F32)],
        compiler_params=_params(1),
        name="moe_router",
    )(xa, xb, wh, wl, b)


DMA_UNROLL = 8


def _pos_copy(pos_hbm, pos_sm, sem_p, step, slot):
    return pltpu.make_async_copy(pos_hbm.at[pl.ds(step, 1)], pos_sm.at[slot], sem_p.at[slot])


def _dispatch_kernel(zf_ref, xa_ref, xb_ref, pos_hbm, xs_hbm, pos_sm, zbuf, sem_p, sem_z, sem_s, *, na_t):
    i = pl.program_id(0)
    n = pl.num_programs(0)
    tm = xa_ref.shape[0]
    tile = zbuf.shape[0]
    n_tiles = xs_hbm.shape[0] // tile
    slot = lax.rem(i, 2)

    def zero_copy(j):
        return pltpu.make_async_copy(zbuf, xs_hbm.at[pl.ds(pl.multiple_of(j * tile, tile), tile)], sem_z)

    @pl.when(i == 0)
    def _():
        _pos_copy(pos_hbm, pos_sm, sem_p, 0, 0).start()
        zbuf[...] = jnp.zeros(zbuf.shape, F32)

        def z_start(j, c):
            @pl.when(zf_ref[j] > 0)
            def _():
                zero_copy(j).start()
            return c

        def z_wait(j, c):
            @pl.when(zf_ref[j] > 0)
            def _():
                zero_copy(j).wait()
            return c

        lax.fori_loop(0, n_tiles, z_start, 0)
        lax.fori_loop(0, n_tiles, z_wait, 0)

    @pl.when(i + 1 < n)
    def _():
        _pos_copy(pos_hbm, pos_sm, sem_p, i + 1, 1 - slot).start()

    _pos_copy(pos_hbm, pos_sm, sem_p, i, slot).wait()

    def scatter_rows(x_ref):
        def issue(r, c):
            for k in range(TOP_FINE):
                dst = pos_sm[slot, 0, k * tm + r]
                pltpu.make_async_copy(x_ref.at[pl.ds(r, 1)], xs_hbm.at[pl.ds(dst, 1)], sem_s).start(priority=k)
            return c

        lax.fori_loop(0, tm, issue, 0, unroll=DMA_UNROLL)
        for _ in range(TOP_FINE):
            pltpu.make_async_copy(x_ref, xs_hbm.at[pl.ds(0, tm)], sem_s).wait()

    @pl.when(i < na_t)
    def _():
        scatter_rows(xa_ref)

    @pl.when(i >= na_t)
    def _():
        scatter_rows(xb_ref)


def _dispatch(xa, xb, pos_tab, zero_flag, *, n_tiles, tm):
    d = xa.shape[1]
    n = xa.shape[0] + xb.shape[0]
    spec_a, spec_b, na_t = _two_source_specs(xa.shape[0], xb.shape[0], tm, d)
    any_spec = pl.BlockSpec(memory_space=pl.ANY)
    grid_spec = pltpu.PrefetchScalarGridSpec(
        num_scalar_prefetch=1,
        grid=(n // tm,),
        in_specs=[spec_a, spec_b, any_spec],
        out_specs=any_spec,
        scratch_shapes=[pltpu.SMEM((2, 1, TOP_FINE * tm), jnp.int32), pltpu.VMEM((MOE_TILE, d), F32),
                        pltpu.SemaphoreType.DMA((2,)), pltpu.SemaphoreType.DMA(()), pltpu.SemaphoreType.DMA(())],
    )
    return pl.pallas_call(
        functools.partial(_dispatch_kernel, na_t=na_t),
        grid_spec=grid_spec,
        out_shape=jax.ShapeDtypeStruct((n_tiles * MOE_TILE, d), F32),
        compiler_params=_params(1),
        name="moe_dispatch",
    )(zero_flag, xa, xb, pos_tab)


def _expert_kernel(te_ref, xs_ref, wgu_ref, wd_ref, ys_ref, *, three_pass):
    if three_pass:
        xh, xl = _split2(xs_ref[...])
        wh, wl = _split2(wgu_ref[0])
        hgu = _dot_hl(xh, xl, wh, wl)
    else:
        hgu = _mm(xs_ref[...].astype(BF16), wgu_ref[0].astype(BF16))
    act = _silu(hgu[:, :D_EXPERT]) * hgu[:, D_EXPERT:]
    if three_pass:
        ys_ref[...] = _dot3(act, wd_ref[0])
    else:
        ys_ref[...] = _mm(act.astype(BF16), wd_ref[0].astype(BF16))


def _experts(xs, w_gate_up, w_down, tile_expert, *, three_pass):
    p, d = xs.shape
    f2 = w_gate_up.shape[-1]
    row = pl.BlockSpec((MOE_TILE, d), lambda i, te: (i, 0))
    grid_spec = pltpu.PrefetchScalarGridSpec(
        num_scalar_prefetch=1,
        grid=(p // MOE_TILE,),
        in_specs=[row, pl.BlockSpec((1, d, f2), lambda i, te: (te[i], 0, 0)),
                  pl.BlockSpec((1, f2 // 2, d), lambda i, te: (te[i], 0, 0))],
        out_specs=row,
    )
    return pl.pallas_call(
        functools.partial(_expert_kernel, three_pass=three_pass),
        grid_spec=grid_spec,
        out_shape=jax.ShapeDtypeStruct((p, d), F32),
        compiler_params=_params(1),
        name="moe_experts",
    )(tile_expert, xs, w_gate_up, w_down)


def _combine_ln_kernel(xa_ref, xb_ref, gate_ref, g_ref, b_ref, pos_hbm, ys_hbm, oa_ref, ob_ref,
                       pos_sm, ybuf, sem_p, sem_y, *, na_t):
    i = pl.program_id(0)
    n = pl.num_programs(0)
    tm = xa_ref.shape[0]
    slot = lax.rem(i, 2)

    def issue_gathers(s):
        def issue(r, c):
            for k in range(TOP_FINE):
                src = pos_sm[s, 0, k * tm + r]
                pltpu.make_async_copy(ys_hbm.at[pl.ds(src, 1)], ybuf.at[s, pl.ds(k * tm + r, 1)],
                                      sem_y.at[s]).start(priority=k)
            return c

        lax.fori_loop(0, tm, issue, 0, unroll=DMA_UNROLL)

    @pl.when(i == 0)
    def _():
        _pos_copy(pos_hbm, pos_sm, sem_p, 0, 0).start()
        _pos_copy(pos_hbm, pos_sm, sem_p, 0, 0).wait()
        issue_gathers(0)

        @pl.when(n > 1)
        def _():
            _pos_copy(pos_hbm, pos_sm, sem_p, 1, 1).start()

    @pl.when(i + 1 < n)
    def _():
        _pos_copy(pos_hbm, pos_sm, sem_p, i + 1, 1 - slot).wait()
        issue_gathers(1 - slot)

    @pl.when(i + 2 < n)
    def _():
        _pos_copy(pos_hbm, pos_sm, sem_p, i + 2, slot).start()

    pltpu.make_async_copy(ys_hbm.at[pl.ds(0, TOP_FINE * tm)], ybuf.at[slot], sem_y.at[slot]).wait()
    gate = gate_ref[...]
    y = ybuf[slot]
    moe = gate[:, 0:1] * y[0:tm] + gate[:, RT_HALF:RT_HALF + 1] * y[tm:2 * tm]

    @pl.when(i < na_t)
    def _():
        oa_ref[...] = _layer_norm(DN_ALPHA * xa_ref[...] + moe, g_ref[...], b_ref[...])

    @pl.when(i >= na_t)
    def _():
        ob_ref[...] = _layer_norm(DN_ALPHA * xb_ref[...] + moe, g_ref[...], b_ref[...])


def _combine_ln(xa, xb, ys, pos_tab, gate, g, b, *, tm):
    d = xa.shape[1]
    na, nb = xa.shape[0], xb.shape[0]
    assert TOP_FINE == 2 and nb == tm
    spec_a, spec_b, na_t = _two_source_specs(na, nb, tm, d)
    any_spec = pl.BlockSpec(memory_space=pl.ANY)
    return pl.pallas_call(
        functools.partial(_combine_ln_kernel, na_t=na_t),
        grid=((na + nb) // tm,),
        in_specs=[spec_a, spec_b, pl.BlockSpec((tm, V7X_LANES), lambda i: (i, 0)), _const_spec((1, d)),
                  _const_spec((1, d)), any_spec, any_spec],
        out_specs=[spec_a, spec_b],
        out_shape=[jax.ShapeDtypeStruct((na, d), F32), jax.ShapeDtypeStruct((nb, d), F32)],
        scratch_shapes=[pltpu.SMEM((2, 1, TOP_FINE * tm), jnp.int32), pltpu.VMEM((2, TOP_FINE * tm, d), F32),
                        pltpu.SemaphoreType.DMA((2,)), pltpu.SemaphoreType.DMA((2,))],
        compiler_params=_params(1),
        name="moe_combine_ln",
    )(xa, xb, gate, g, b, pos_tab, ys)


def _slots_kernel(id_ref, rank_ref, base_ref, pos_ref, *, tm):
    lane = _iota((tm, V7X_LANES), 1)
    lane_f = lane.astype(F32)
    base = base_ref[...]
    for t in range(pos_ref.shape[0]):
        rows = slice(t * tm, (t + 1) * tm)
        ids = id_ref[rows, :].astype(F32)

        def first_row(choice, ids=ids):
            hit = lane_f == ids[:, choice:choice + 1] + float(RT_FINE0)
            return jnp.sum(jnp.where(hit, base, 0.0), axis=-1, keepdims=True)

        pos = rank_ref[rows, :].astype(F32) + jnp.where(lane < RT_HALF, first_row(0), first_row(RT_HALF))
        pos_t = jnp.transpose(pos)
        pos_ref[t] = jnp.concatenate([pos_t[0:1, :], pos_t[RT_HALF:RT_HALF + 1, :]], axis=1).astype(jnp.int32)


def _slots(ids, rank, base, *, tm):
    n = ids.shape[0]
    assert tm == V7X_LANES and n % tm == 0
    per_step = max(k for k in range(1, 9) if (n // tm) % k == 0)
    row = pl.BlockSpec((per_step * tm, V7X_LANES), lambda i: (i, 0))
    out = pl.pallas_call(
        functools.partial(_slots_kernel, tm=tm),
        grid=(n // tm // per_step,),
        in_specs=[row, row, _const_spec((1, V7X_LANES))],
        out_specs=pl.BlockSpec((per_step, 1, TOP_FINE * tm), lambda i: (i, 0, 0)),
        out_shape=jax.ShapeDtypeStruct((n // tm, 1, TOP_FINE * tm), jnp.int32),
        compiler_params=_params(1),
        name="moe_slots",
    )(ids, rank, base)
    return out.reshape(n // tm, TOP_FINE * tm)


def _moe_plan(ids, rank, cnt, n, tm):
    counts = cnt[0, RT_FINE0:RT_FINE0 + N_EXPERTS].astype(jnp.int32)
    tiles_per = (counts + MOE_TILE - 1) // MOE_TILE
    e_idx = jnp.arange(N_EXPERTS)
    tile_end = tiles_per @ (e_idx[:, None] <= e_idx[None, :]).astype(jnp.int32)
    tile_beg = tile_end - tiles_per
    n_tiles = -(-TOP_FINE * n // MOE_TILE) + N_EXPERTS
    ti = jnp.arange(n_tiles)
    te = jnp.minimum(jnp.sum((tile_end[None, :] <= ti[:, None]).astype(jnp.int32), axis=1), N_EXPERTS - 1)
    is_last = jnp.any((ti[:, None] == tile_end[None, :] - 1) & (tiles_per[None, :] > 0), axis=1)
    zero_flag = (is_last | (ti >= tile_end[-1])).astype(jnp.int32)
    base = jnp.zeros((1, V7X_LANES), F32).at[0, RT_FINE0:RT_FINE0 + N_EXPERTS].set(
        (tile_beg * MOE_TILE).astype(F32))
    pos_tab = _slots(ids, rank, base, tm=tm)
    return te.astype(jnp.int32), zero_flag, pos_tab, n_tiles


def _moe_prepare(w_coarse, b_coarse, w_fine, b_fine):
    d = w_coarse.shape[0]
    wf = jnp.transpose(w_fine, (1, 0, 2)).reshape(d, N_EXPERTS)
    w = jnp.zeros((d, V7X_LANES), F32).at[:, :N_GROUPS].set(w_coarse).at[:, RT_FINE0:RT_FINE0 + N_EXPERTS].set(wf)
    b = jnp.zeros((1, V7X_LANES), F32).at[0, :N_GROUPS].set(b_coarse)
    b = b.at[0, RT_FINE0:RT_FINE0 + N_EXPERTS].set(b_fine.reshape(-1))
    return _split_w(w), b


def _moe_layer(xa, xb, w_coarse, b_coarse, w_fine, b_fine, w_gate_up, w_down, layer, g, b, *, tm, three_pass):
    d = xa.shape[1]
    n = xa.shape[0] + xb.shape[0]
    (wrh, wrl), br = _moe_prepare(w_coarse, b_coarse, w_fine, b_fine)
    ids, rank, gate, cnt = _router(xa, xb, wrh, wrl, br, tm=tm)
    te, zero_flag, pos_tab, n_tiles = _moe_plan(ids, rank, cnt, n, tm)
    xs = _dispatch(xa, xb, pos_tab, zero_flag, n_tiles=n_tiles, tm=tm)
    ys = _experts(xs, w_gate_up.reshape(-1, d, 2 * D_EXPERT), w_down.reshape(-1, D_EXPERT, d),
                  te + layer * N_EXPERTS, three_pass=three_pass)
    return _combine_ln(xa, xb, ys, pos_tab, gate, g, b, tm=tm)


def _cdproj_kernel(x_ref, wh_ref, wl_ref, cos_ref, sin_ref, perm_ref, u_ref, q_ref, k_ref, v_ref):
    xh, xl = _split2(x_ref[...])

    def col(j):
        cs = slice(j * D_C, (j + 1) * D_C)
        return _dot_hl(xh, xl, wh_ref[:, cs], wl_ref[:, cs])

    u_ref[...] = col(0)
    for j, ref in ((1, q_ref), (2, k_ref)):
        t = col(j)
        ref[...] = t * cos_ref[...] + _dotx(t, perm_ref[...]) * sin_ref[...]
    v_ref[...] = col(3)


def _cdproj(x, wh, wl, cos_t, sin_t, perm, *, n_rows, tm):
    k = x.shape[1]
    assert D_C == D_D and n_rows % tm == 0 and cos_t.shape[0] % tm == 0
    nt = cos_t.shape[0] // tm
    row = pl.BlockSpec((tm, D_D), lambda i: (i, 0))
    tab = pl.BlockSpec((tm, D_D), lambda i: (i % nt, 0))
    return pl.pallas_call(
        _cdproj_kernel,
        grid=(n_rows // tm,),
        in_specs=[pl.BlockSpec((tm, k), lambda i: (i, 0)), _const_spec(wh.shape), _const_spec(wl.shape),
                  tab, tab, _const_spec(perm.shape)],
        out_specs=[row] * 4,
        out_shape=[jax.ShapeDtypeStruct((n_rows, D_D), F32)] * 4,
        compiler_params=_params(1),
        name="cd_proj_rope",
    )(x, wh, wl, cos_t, sin_t, perm)


def _rope_tables(pos):
    half = ROT_DIM // 2
    inv = ROPE_THETA ** (-jnp.arange(half, dtype=F32) / half)
    ang = pos.astype(F32)[:, None] * inv
    cos, sin = jnp.cos(ang), jnp.sin(ang)
    n = pos.shape[0]
    rest = HD_D - ROT_DIM
    cos_h = jnp.concatenate([cos, cos, jnp.ones((n, rest), F32)], axis=-1)
    sin_h = jnp.concatenate([-sin, sin, jnp.zeros((n, rest), F32)], axis=-1)
    perm = np.zeros((D_D, D_D), np.float32)
    for dst in range(D_D):
        j = dst % HD_D
        if j < half:
            perm[dst + half, dst] = 1.0
        elif j < ROT_DIM:
            perm[dst - half, dst] = 1.0
    return jnp.tile(cos_h, (1, H_D)), jnp.tile(sin_h, (1, H_D)), jnp.asarray(perm, BF16)


POOL_HDR = 16


def _pool_kernel(u_ref, b0_ref, wph_ref, wpl_ref, sc_ref, o_ref, bo_ref, ext_sc, *, tc, valid, start):
    t = pl.program_id(1)
    nt = pl.num_programs(1)

    @pl.when(t == 0)
    def _():
        ext_sc[0:POOL_HDR, :] = b0_ref[0]

    ext_sc[POOL_HDR:POOL_HDR + tc, :] = u_ref[...]
    u = u_ref[...]
    rowi = _iota((tc, POOL_HDR + tc), 0) + POOL_HDR
    colj = _iota((tc, POOL_HDR + tc), 1)
    pos = (start + t * tc + _iota((tc, POOL_GROUP), 0)).astype(F32)
    for gi, w in enumerate(POOL_WINDOWS):
        ls = slice(gi * POOL_GROUP, (gi + 1) * POOL_GROUP)
        band = jnp.where((colj <= rowi) & (colj > rowi - w), 1.0, 0.0).astype(BF16)
        win = _xdot(band, ext_sc[:, ls])
        cnt = jnp.minimum(float(w), pos + 1.0)
        dh, dl = _split2(win / cnt - u[:, ls])
        o_ref[:, ls] = _dot_hl(dh, dl, wph_ref[gi], wpl_ref[gi]) * sc_ref[:, ls]

    @pl.when(t == nt - 1)
    def _():
        bo_ref[0] = ext_sc[valid:valid + POOL_HDR, :]

    ext_sc[0:POOL_HDR, :] = ext_sc[tc:tc + POOL_HDR, :]


def _pool(u, buf, wph, wpl, scale, *, nb, t_len, tc, valid, start):
    nt = t_len // tc
    assert t_len % tc == 0 and (valid == tc or nt == 1)
    b0 = jnp.zeros((nb, POOL_HDR, D_C), F32).at[:, POOL_HDR - POOL_BUF:].set(buf)
    row = pl.BlockSpec((tc, D_C), lambda b, t: (b * nt + t, 0))
    st = pl.BlockSpec((1, POOL_HDR, D_C), lambda b, t: (b, 0, 0))
    pooled, bnew = pl.pallas_call(
        functools.partial(_pool_kernel, tc=tc, valid=valid, start=start),
        grid=(nb, nt),
        in_specs=[row, st, _const_spec(wph.shape), _const_spec(wpl.shape), _const_spec(scale.shape)],
        out_specs=[row, st],
        out_shape=[jax.ShapeDtypeStruct((nb * t_len, D_C), F32), jax.ShapeDtypeStruct((nb, POOL_HDR, D_C), F32)],
        scratch_shapes=[pltpu.VMEM((POOL_HDR + tc, D_C), F32)],
        compiler_params=_params(2),
        name="pool_mix",
    )(u, b0, wph, wpl, scale)
    return pooled, bnew[:, POOL_HDR - POOL_BUF:]


def _top_blocks(gate, lane_f, valid, n_pick):
    sel = jnp.zeros(gate.shape, jnp.bool_)
    g = jnp.where(valid, gate, -jnp.inf)
    picks = []
    for _ in range(n_pick):
        v = jnp.max(g, axis=-1, keepdims=True)
        idx = jnp.min(jnp.where(g == v, lane_f, float(V7X_LANES)), axis=-1, keepdims=True)
        hit = (lane_f == idx) & valid
        sel = sel | hit
        g = jnp.where(hit, -jnp.inf, g)
        picks.append(idx)
    return sel, picks


def _cat_pieces(a, b, c):
    return jnp.concatenate([a, b, c, jnp.zeros_like(a)], axis=1)


MOBA_TRIP = 4


def _top_rows(gate, row_f, valid, n_pick):
    sel = jnp.zeros(gate.shape, F32)
    g = jnp.where(valid, gate, -jnp.inf)
    for _ in range(n_pick):
        v = jnp.max(g, axis=0, keepdims=True)
        idx = jnp.min(jnp.where(g == v, row_f, float(V7X_LANES)), axis=0, keepdims=True)
        hit = (row_f == idx) & valid
        sel = jnp.where(hit, 1.0, sel)
        g = jnp.where(hit, -jnp.inf, g)
    return sel


def _moba_prompt_kernel(q_ref, k_ref, v_ref, o_ref, kmean_sc, kcat_sc, vt_sc, sel_sc):
    i = pl.program_id(2)
    blk = MOBA_BLOCK
    t_len = k_ref.shape[0]
    n_heads = V7X_LANES // HD_D
    g_rows = sel_sc.shape[1]
    scale = HD_D ** -0.5

    @pl.when(i == 0)
    def _():
        ind = jnp.where((_iota((V7X_LANES, t_len), 1) >> 8) == _iota((V7X_LANES, t_len), 0), 1.0, 0.0)
        kmean_sc[...] = _xdot(ind.astype(BF16), k_ref[...]) * (1.0 / blk)
        ones_row = jnp.where(_iota((HD_D, blk), 0) == 0, 1.0, 0.0).astype(BF16)
        for n in range(t_len // blk):
            rows = slice(n * blk, (n + 1) * blk)
            kh, kl = _split2(k_ref[rows, :])
            vt = jnp.transpose(v_ref[rows, :]).astype(BF16)
            for hh in range(n_heads):
                hs = slice(hh * HD_D, (hh + 1) * HD_D)
                kcat_sc[hh, rows, :] = _cat_pieces(kh[:, hs], kh[:, hs], kl[:, hs])
                vt_sc[hh, n, 0:HD_D, :] = vt[hs, :]
                vt_sc[hh, n, HD_D:2 * HD_D, :] = ones_row

    def scores_t(hh, qcat, j):
        return _mm(kcat_sc[hh, pl.ds(pl.multiple_of(j * blk, blk), blk), :], qcat, NT)

    def weighted_t(hh, p, j):
        return _mm(vt_sc[hh, j], p.astype(BF16))

    q = q_ref[...]
    key_le_query = _iota((blk, blk), 0) <= _iota((blk, blk), 1)
    brow = _iota((g_rows, blk), 0)
    qcats, state = [], []
    for hh in range(n_heads):
        hs = slice(hh * HD_D, (hh + 1) * HD_D)
        qh = q[:, hs]
        gate_t = _dot3_cat(kmean_sc[0:g_rows, hs], qh, NT)
        sel_sc[hh] = _top_rows(gate_t, brow.astype(F32), brow < i, MOBA_TOPK)
        q_hi, q_lo = _split2(qh * scale)
        qcats.append(_cat_pieces(q_hi, q_lo, q_hi))
        s = jnp.where(key_le_query, scores_t(hh, qcats[hh], i), -jnp.inf)
        m = jnp.max(s, axis=0, keepdims=True)
        state += [m, weighted_t(hh, jnp.exp(s - m), i)]

    def body(jj, carry):
        out = []
        for hh in range(n_heads):
            m, acc = carry[2 * hh:2 * hh + 2]
            js = [jnp.minimum(MOBA_TRIP * jj + t, i) for t in range(MOBA_TRIP)]
            ss = []
            for j in js:
                on = sel_sc[hh, pl.ds(j, 1), :] > 0.0
                ss.append(jnp.where(on, scores_t(hh, qcats[hh], j), -jnp.inf))
            top = functools.reduce(jnp.maximum, ss)
            m2 = jnp.maximum(m, jnp.max(top, axis=0, keepdims=True))
            new = functools.reduce(lambda x, y: x + y,
                                   [weighted_t(hh, jnp.exp(sj - m2), j) for sj, j in zip(ss, js)])
            out += [m2, jnp.exp(m - m2) * acc + new]
        return tuple(out)

    state = lax.fori_loop(0, (i + MOBA_TRIP - 1) >> (MOBA_TRIP.bit_length() - 1), body, tuple(state))
    for hh in range(n_heads):
        acc = jnp.transpose(state[2 * hh + 1])
        o_ref[:, hh * HD_D:(hh + 1) * HD_D] = acc[:, 0:HD_D] / acc[:, HD_D:HD_D + 1]


def _moba_prompt(q, k, v, *, nb, t_len):
    blk = MOBA_BLOCK
    nq = t_len // blk
    assert t_len % blk == 0 and nq <= V7X_LANES
    n_heads = V7X_LANES // HD_D
    g_rows = -(-nq // V7X_SUBLANES) * V7X_SUBLANES
    qspec = pl.BlockSpec((blk, V7X_LANES), lambda b, hp, i: (b * nq + i, hp))
    kspec = pl.BlockSpec((t_len, V7X_LANES), lambda b, hp, i: (b, hp))
    return pl.pallas_call(
        _moba_prompt_kernel,
        grid=(nb, D_D // V7X_LANES, nq),
        in_specs=[qspec, kspec, kspec],
        out_specs=qspec,
        out_shape=jax.ShapeDtypeStruct((nb * t_len, D_D), F32),
        scratch_shapes=[pltpu.VMEM((V7X_LANES, V7X_LANES), F32),
                        pltpu.VMEM((n_heads, t_len, 4 * HD_D), BF16),
                        pltpu.VMEM((n_heads, nq, 2 * HD_D, blk), BF16),
                        pltpu.VMEM((n_heads, g_rows, blk), F32)],
        compiler_params=_params(3),
        name="moba_prompt",
    )(q, k, v)


KM_PAGES = 32
PAGES_PER_BLOCK = MOBA_BLOCK // PAGE_SIZE


def _kmean_kernel(pt_ref, *refs):
    pages, o_ref = refs[:KM_PAGES], refs[KM_PAGES]
    c = pl.program_id(1)
    blocks_per_step = KM_PAGES // PAGES_PER_BLOCK

    @pl.when(c == 0)
    def _():
        o_ref[...] = jnp.zeros(o_ref.shape, F32)

    acc = o_ref[0]
    lane = _iota(acc.shape, 2)
    for blk in range(blocks_per_step):
        tot = pages[blk * PAGES_PER_BLOCK][0]
        for p in range(1, PAGES_PER_BLOCK):
            tot = tot + pages[blk * PAGES_PER_BLOCK + p][0]
        mean = jnp.sum(tot, axis=-1, keepdims=True) * (1.0 / MOBA_BLOCK)
        acc = jnp.where(lane == c * blocks_per_step + blk, mean, acc)
    o_ref[0] = acc


def _block_means(cache_kt, page_table):
    db, n_pages = page_table.shape
    n_blocks = n_pages // PAGES_PER_BLOCK
    assert n_pages % KM_PAGES == 0
    page_spec = lambda j: pl.BlockSpec((1, H_D, HD_D, PAGE_SIZE), lambda b, c, pt: (pt[b, c * KM_PAGES + j], 0, 0, 0))
    grid_spec = pltpu.PrefetchScalarGridSpec(
        num_scalar_prefetch=1,
        grid=(db, n_pages // KM_PAGES),
        in_specs=[page_spec(j) for j in range(KM_PAGES)],
        out_specs=pl.BlockSpec((1, H_D, HD_D, n_blocks), lambda b, c, pt: (b, 0, 0, 0)),
    )
    return pl.pallas_call(
        _kmean_kernel,
        grid_spec=grid_spec,
        out_shape=jax.ShapeDtypeStruct((db, H_D, HD_D, n_blocks), F32),
        compiler_params=_params(2),
        name="moba_block_means",
    )(page_table, *([cache_kt] * KM_PAGES))


def _select_kernel(q_ref, km_ref, o_ref):
    q = q_ref[...]
    n_blk = km_ref.shape[3]
    lane_f = _iota((q.shape[0], n_blk), 1).astype(F32)
    out_lane = _iota((q.shape[0], V7X_LANES), 1)
    for h in range(H_D):
        gate = _dot3(q[:, h * HD_D:(h + 1) * HD_D], km_ref[0, h])
        _, picks = _top_blocks(gate, lane_f, lane_f >= 0.0, MOBA_TOPK)
        res = jnp.zeros((q.shape[0], V7X_LANES), F32)
        for r, idx in enumerate(picks):
            res = jnp.where(out_lane == r, idx, res)
        o_ref[0, h] = res.astype(jnp.int32)


def _select_blocks(q, kmean_t):
    db = kmean_t.shape[0]
    return pl.pallas_call(
        _select_kernel,
        grid=(db,),
        in_specs=[pl.BlockSpec((SAMPLE_PAD, D_D), lambda b: (b, 0)),
                  pl.BlockSpec((1,) + kmean_t.shape[1:], lambda b: (b, 0, 0, 0))],
        out_specs=pl.BlockSpec((1, H_D, SAMPLE_PAD, V7X_LANES), lambda b: (b, 0, 0, 0)),
        out_shape=jax.ShapeDtypeStruct((db, H_D, SAMPLE_PAD, V7X_LANES), jnp.int32),
        compiler_params=_params(1),
        name="moba_select",
    )(q, kmean_t)


HEADS_PER_STEP = V7X_LANES // HD_D


def _attend_kernel(phys_ref, q_ref, kn_ref, vn_ref, ck_hbm, cv_hbm, o_ref, kbuf, vbuf, sem, *, n_q):
    b = pl.program_id(0)
    hp = pl.program_id(1)
    n_hp = pl.num_programs(1)
    step = b * n_hp + hp
    slot = lax.rem(step, 2)
    scale = HD_D ** -0.5

    def fetch(fb, fhp, fslot):
        for hh in range(HEADS_PER_STEP):
            h = fhp * HEADS_PER_STEP + hh
            for s in range(n_q):
                for r in range(MOBA_TOPK):
                    for pg in range(PAGES_PER_BLOCK):
                        flat = (((fb * H_D + h) * n_q + s) * MOBA_TOPK + r) * PAGES_PER_BLOCK + pg
                        cols = pl.ds((r * PAGES_PER_BLOCK + pg) * PAGE_SIZE, PAGE_SIZE)
                        page = phys_ref[flat]
                        for which, (src, dst) in enumerate(((ck_hbm, kbuf), (cv_hbm, vbuf))):
                            pltpu.make_async_copy(src.at[page, h], dst.at[fslot, hh, s, :, cols],
                                                  sem.at[fslot, which]).start()

    @pl.when(step == 0)
    def _():
        fetch(b, hp, slot)

    @pl.when(step + 1 < pl.num_programs(0) * n_hp)
    def _():
        wrap = hp + 1 == n_hp
        fetch(jnp.where(wrap, b + 1, b), jnp.where(wrap, 0, hp + 1), 1 - slot)

    pltpu.make_async_copy(vbuf.at[1 - slot], kbuf.at[slot], sem.at[slot, 0]).wait()
    pltpu.make_async_copy(kbuf.at[1 - slot], vbuf.at[slot], sem.at[slot, 1]).wait()

    q = q_ref[...]
    rows_n = q.shape[0]
    rowi = _iota((rows_n, rows_n), 0)
    coli = _iota((rows_n, rows_n), 1)
    own_ok = (coli <= rowi) & (coli < n_q)
    rsel = _iota((rows_n, HD_D), 0)
    for hh in range(HEADS_PER_STEP):
        hs = slice(hh * HD_D, (hh + 1) * HD_D)
        qh = q[:, hs]
        s_own = jnp.where(own_ok, _dot3(qh, kn_ref[...][:, hs], NT) * scale, -jnp.inf)
        m_own = jnp.max(s_own, axis=-1, keepdims=True)
        out = jnp.zeros((rows_n, HD_D), F32)
        for s in range(n_q):
            s_sel = _dot3_cat(qh, kbuf[slot, hh, s]) * scale
            m = jnp.maximum(m_own, jnp.max(s_sel, axis=-1, keepdims=True))
            p_sel = jnp.exp(s_sel - m)
            p_own = jnp.exp(s_own - m)
            l = jnp.sum(p_sel, axis=-1, keepdims=True) + jnp.sum(p_own, axis=-1, keepdims=True)
            o_s = (_dot3(p_sel, vbuf[slot, hh, s], NT) + _dot3(p_own, vn_ref[...][:, hs])) / l
            out = jnp.where(rsel == s, o_s, out)
        o_ref[:, hs] = out


def _moba_sample_attend(phys, q, k_new, v_new, cache_kt, cache_vt, *, db, n_q):
    row = pl.BlockSpec((SAMPLE_PAD, V7X_LANES), lambda b, hp, ph: (b, hp))
    any_spec = pl.BlockSpec(memory_space=pl.ANY)
    buf = pltpu.VMEM((2, HEADS_PER_STEP, n_q, HD_D, MOBA_TOPK * MOBA_BLOCK), F32)
    grid_spec = pltpu.PrefetchScalarGridSpec(
        num_scalar_prefetch=1,
        grid=(db, D_D // V7X_LANES),
        in_specs=[row, row, row, any_spec, any_spec],
        out_specs=row,
        scratch_shapes=[buf, buf, pltpu.SemaphoreType.DMA((2, 2))],
    )
    return pl.pallas_call(
        functools.partial(_attend_kernel, n_q=n_q),
        grid_spec=grid_spec,
        out_shape=jax.ShapeDtypeStruct((db * SAMPLE_PAD, D_D), F32),
        compiler_params=_params(2),
        name="moba_sample_attend",
    )(phys, q, k_new, v_new, cache_kt, cache_vt)


def _outproj2_ln_kernel(a1_ref, a2_ref, x_ref, w1h_ref, w1l_ref, w2h_ref, w2l_ref, g_ref, b_ref, o_ref):
    y = _dot_hl(*_split2(a1_ref[...]), w1h_ref[...], w1l_ref[...])
    y = y + _dot_hl(*_split2(a2_ref[...]), w2h_ref[...], w2l_ref[...])
    o_ref[...] = _layer_norm(DN_ALPHA * x_ref[...] + y, g_ref[...], b_ref[...])


def _outproj2_ln(a1, a2, x, w1, w2, g, b, *, n_rows, tm):
    d = x.shape[1]
    assert n_rows % tm == 0
    row = lambda c: pl.BlockSpec((tm, c), lambda i: (i, 0))
    consts = [w1[0], w1[1], w2[0], w2[1], g, b]
    return pl.pallas_call(
        _outproj2_ln_kernel,
        grid=(n_rows // tm,),
        in_specs=[row(a1.shape[1]), row(a2.shape[1]), row(d)] + [_const_spec(c.shape) for c in consts],
        out_specs=row(d),
        out_shape=jax.ShapeDtypeStruct((n_rows, d), F32),
        compiler_params=_params(1),
        name="outproj2_ln",
    )(a1, a2, x, *consts)


SAMPLE_PAD = 16


def _pad_rows(x, t_pad):
    nb, t, d = x.shape
    return jnp.zeros((nb, t_pad, d), x.dtype).at[:, :t].set(x).reshape(nb * t_pad, d)


def kernel(x_prompt, x_sample, state_gla, state_ssm, state_conv, state_pool, cache_k, cache_v, page_table, ab_w_in, ab_w_gate2, ab_b_gate, ab_gla_norm, ab_conv_w, ab_conv_b, ab_dt_bias, ab_a_log, ab_d_skip, ab_ssm_norm, ab_w_out, cd_w_in, cd_w_pool, cd_pool_scale, cd_w_out, moe_w_coarse, moe_b_coarse, moe_w_fine, moe_b_fine, moe_w_gate_up, moe_w_down, ln_g, ln_b):
    bp, tp, d = x_prompt.shape
    bs, ts, _ = x_sample.shape
    (wih, wil), ab_prm, (woh, wol) = _ab_prepare(ab_w_in, ab_w_gate2, ab_b_gate, ab_gla_norm, ab_conv_w,
                                                 ab_conv_b, ab_dt_bias, ab_a_log, ab_d_skip, ab_ssm_norm, ab_w_out)
    n_p, n_s = bp * tp, bs * ts
    n_pages = page_table.shape[1]
    past = n_pages * PAGE_SIZE
    assert past % MOBA_BLOCK == 0 and past // MOBA_BLOCK >= MOBA_TOPK and ts <= SAMPLE_PAD
    zeros = lambda *s: jnp.zeros(s, F32)
    ln = lambda l, j: (ln_g[l, j].reshape(1, d), ln_b[l, j].reshape(1, d))
    unpad = lambda rows: rows.reshape(bs, SAMPLE_PAD, -1)[:, :ts]
    assert n_s == V7X_LANES
    moe = lambda l, xa, xb: _moe_layer(xa, unpad(xb).reshape(n_s, d), moe_w_coarse[l], moe_b_coarse[l],
                                       moe_w_fine[l], moe_b_fine[l], moe_w_gate_up, moe_w_down, l, *ln(l, 1),
                                       tm=V7X_LANES,
                                       three_pass=l < DEPTH - 1)

    xp = x_prompt.reshape(n_p, d)
    xs = _pad_rows(x_sample, SAMPLE_PAD)
    proj_p = _proj(xp, wih, wil)
    oy_p, gla_p, ssm_p, conv_p = _scan_from_proj(
        proj_p, zeros(bp, H_A, DK_A, DV_A), zeros(bp, H_B, P_B, N_B), zeros(bp, CONV_W - 1, CONV_DIM), ab_prm,
        nb=bp, t_len=tp, tc=256, chunk=CHUNK, valid=256)
    xp = _outproj_ln(oy_p, xp, woh, wol, *ln(0, 0))
    proj_s = _proj(xs, wih, wil)
    oy_s, gla_s, ssm_s, conv_s = _scan_from_proj(
        proj_s, state_gla, state_ssm, state_conv, ab_prm,
        nb=bs, t_len=SAMPLE_PAD, tc=SAMPLE_PAD, chunk=SAMPLE_PAD, valid=ts)
    xs = _outproj_ln(oy_s, xs, woh, wol, *ln(0, 0))
    xp, xs = moe(0, xp, xs)

    wch, wcl = _split_w(cd_w_in)
    wph, wpl = _split_w(cd_w_pool)
    pscale = cd_pool_scale.reshape(1, D_C)
    wo_c, wo_d = _split_w(cd_w_out[:D_C]), _split_w(cd_w_out[D_C:])
    cos_p, sin_p, perm = _rope_tables(jnp.arange(tp))
    u, q, k, v = _cdproj(xp, wch, wcl, cos_p, sin_p, perm, n_rows=n_p, tm=256)
    pooled, pool_p = _pool(u, zeros(bp, POOL_BUF, D_C), wph, wpl, pscale, nb=bp, t_len=tp, tc=256, valid=256, start=0)
    att = _moba_prompt(q, k, v, nb=bp, t_len=tp)
    xp = _outproj2_ln(pooled, att, xp, wo_c, wo_d, *ln(1, 0), n_rows=n_p, tm=256)
    k_p, v_p = k.reshape(bp, tp, H_D, HD_D), v.reshape(bp, tp, H_D, HD_D)

    xs = _pad_rows(xs.reshape(bs, ts, d), SAMPLE_PAD)
    cos_s, sin_s, _ = _rope_tables(past + jnp.arange(SAMPLE_PAD))
    n_rows_s = bs * SAMPLE_PAD
    u, q, k, v = _cdproj(xs, wch, wcl, jnp.tile(cos_s, (bs, 1)), jnp.tile(sin_s, (bs, 1)), perm,
                         n_rows=n_rows_s, tm=256)
    pooled, pool_s = _pool(u, state_pool, wph, wpl, pscale, nb=bs, t_len=SAMPLE_PAD, tc=SAMPLE_PAD, valid=ts,
                           start=past)
    cache_kt, cache_vt = jnp.transpose(cache_k, (0, 2, 3, 1)), jnp.transpose(cache_v, (0, 2, 3, 1))
    picks = _select_blocks(q, _block_means(cache_kt, page_table))[:, :, :ts, :MOBA_TOPK]
    logical = picks[..., None] * PAGES_PER_BLOCK + jnp.arange(PAGES_PER_BLOCK)
    phys = page_table[jnp.arange(bs)[:, None, None, None, None], logical].reshape(-1)
    att = _moba_sample_attend(phys, q, k, v, cache_kt, cache_vt, db=bs, n_q=ts)
    xs = _outproj2_ln(pooled, att, xs, wo_c, wo_d, *ln(1, 0), n_rows=n_rows_s, tm=256)
    k_s, v_s = unpad(k).reshape(bs, ts, H_D, HD_D), unpad(v).reshape(bs, ts, H_D, HD_D)
    xp, xs = moe(1, xp, xs)

    return (xp.reshape(bp, tp, d), xs.reshape(bs, ts, d), gla_p, ssm_p, conv_p, pool_p, k_p, v_p,
            gla_s, ssm_s, conv_s, pool_s, k_s, v_s)
```

```python
import functools

import jax
import jax.numpy as jnp
import numpy as np
from jax import lax
from jax.experimental import pallas as pl
from jax.experimental.pallas import tpu as pltpu

F32 = jnp.float32
BF16 = jnp.bfloat16

D_MODEL = 1024
DEPTH = 2
PAGE_SIZE = 128
H_A, DK_A, DV_A = 4, 64, 128
GATE_RANK = 16
GATE_TAU = 16.0
CHUNK = 64
H_B, P_B, N_B, G_B = 8, 64, 64, 2
CONV_W = 4
D_B = H_B * P_B
CONV_DIM = D_B + 2 * G_B * N_B
POOL_WINDOWS = (2, 4, 8, 16)
POOL_GROUP = 128
D_C = len(POOL_WINDOWS) * POOL_GROUP
POOL_BUF = max(POOL_WINDOWS) - 1
H_D, HD_D = 8, 64
D_D = H_D * HD_D
MOBA_BLOCK = 256
MOBA_TOPK = 3
ROT_DIM = HD_D // 4
ROPE_THETA = 500000.0
N_GROUPS, EXP_PER_GROUP, TOP_FINE, D_EXPERT = 4, 8, 2, 256
N_EXPERTS = N_GROUPS * EXP_PER_GROUP
DN_ALPHA = (2 * DEPTH) ** 0.25
LN_EPS = 1e-5
RMS_EPS = 1e-6

V7X_LANES = 128
V7X_SUBLANES = 8
V7X_VMEM_LIMIT_BYTES = 56 * 1024 * 1024

NN = (((1,), (0,)), ((), ()))
NT = (((1,), (1,)), ((), ()))
TN = (((0,), (0,)), ((), ()))


def _params(n_axes):
    return pltpu.CompilerParams(dimension_semantics=("arbitrary",) * n_axes,
                                vmem_limit_bytes=V7X_VMEM_LIMIT_BYTES)


def _const_spec(shape):
    zeros = (0,) * len(shape)
    return pl.BlockSpec(shape, lambda *_: zeros, pipeline_mode=pl.Buffered(1))


def _split2(a):
    hi = a.astype(BF16)
    lo = (a - hi.astype(F32)).astype(BF16)
    return hi, lo


def _split3(a):
    hi = a.astype(BF16)
    r = a - hi.astype(F32)
    mid = r.astype(BF16)
    lo = (r - mid.astype(F32)).astype(BF16)
    return hi, mid, lo


def _mm(a, b, dims=NN):
    return lax.dot_general(a, b, dims, preferred_element_type=F32)


def _dot_hl(ah, al, bh, bl, dims=NN):
    return (_mm(ah, bl, dims) + _mm(al, bh, dims)) + _mm(ah, bh, dims)


def _dot3(a, b, dims=NN):
    ah, al = _split2(a)
    bh, bl = _split2(b)
    return _dot_hl(ah, al, bh, bl, dims)


def _dot3_cat(a, b, dims=NN):
    (ca,), (cb,) = dims[0]
    ah, al = _split2(a)
    bh, bl = _split2(b)
    return _mm(jnp.concatenate([ah, al, ah], axis=ca), jnp.concatenate([bh, bh, bl], axis=cb), dims)


def _dotx(a, e, dims=NN):
    h, m, l = _split3(a)
    return (_mm(l, e, dims) + _mm(m, e, dims)) + _mm(h, e, dims)


def _xdot(e, a, dims=NN):
    h, m, l = _split3(a)
    return (_mm(e, l, dims) + _mm(e, m, dims)) + _mm(e, h, dims)


def _silu(x):
    return x / (1.0 + jnp.exp(-x))


def _softplus(x):
    return jnp.maximum(x, 0.0) + jnp.log1p(jnp.exp(-jnp.abs(x)))


def _log_sigmoid(x):
    return jnp.minimum(x, 0.0) - jnp.log1p(jnp.exp(-jnp.abs(x)))


def _layer_norm(z, g, b):
    mu = jnp.mean(z, axis=-1, keepdims=True)
    zc = z - mu
    var = jnp.mean(zc * zc, axis=-1, keepdims=True)
    return zc * lax.rsqrt(var + LN_EPS) * g + b


def _iota(shape, dim):
    return lax.broadcasted_iota(jnp.int32, shape, dim)


def _proj_kernel(x_ref, wh_ref, wl_ref, o_ref, *, col_chunk):
    xh, xl = _split2(x_ref[...])
    m = o_ref.shape[1]
    for j in range(0, m, col_chunk):
        cs = slice(j, min(j + col_chunk, m))
        o_ref[:, cs] = _dot_hl(xh, xl, wh_ref[:, cs], wl_ref[:, cs])


def _proj(x, wh, wl, *, n_rows=None, tm=256, col_chunk=512):
    n = x.shape[0] if n_rows is None else n_rows
    k = x.shape[1]
    m = wh.shape[1]
    assert n % tm == 0
    return pl.pallas_call(
        functools.partial(_proj_kernel, col_chunk=col_chunk),
        grid=(n // tm,),
        in_specs=[pl.BlockSpec((tm, k), lambda i: (i, 0)), _const_spec((k, m)), _const_spec((k, m))],
        out_specs=pl.BlockSpec((tm, m), lambda i: (i, 0)),
        out_shape=jax.ShapeDtypeStruct((n, m), F32),
        compiler_params=_params(1),
        name="proj",
    )(x, wh, wl)


def _outproj_ln_kernel(a_ref, x_ref, wh_ref, wl_ref, g_ref, b_ref, o_ref):
    ah, al = _split2(a_ref[...])
    y = _dot_hl(ah, al, wh_ref[...], wl_ref[...])
    o_ref[...] = _layer_norm(DN_ALPHA * x_ref[...] + y, g_ref[...], b_ref[...])


def _outproj_ln(a, x, wh, wl, g, b, *, tm=256):
    n, k = a.shape
    d = wh.shape[1]
    assert n % tm == 0
    return pl.pallas_call(
        _outproj_ln_kernel,
        grid=(n // tm,),
        in_specs=[pl.BlockSpec((tm, k), lambda i: (i, 0)), pl.BlockSpec((tm, d), lambda i: (i, 0)),
                  _const_spec((k, d)), _const_spec((k, d)), _const_spec((1, d)), _const_spec((1, d))],
        out_specs=pl.BlockSpec((tm, d), lambda i: (i, 0)),
        out_shape=jax.ShapeDtypeStruct((n, d), F32),
        compiler_params=_params(1),
        name="outproj_ln",
    )(a, x, wh, wl, g, b)


SM_DT0 = GATE_RANK
CONV_HDR = V7X_SUBLANES


def _scan_kernel(qkvr_ref, z_ref, xbc_ref, sm_ref, s0_ref, h0_ref, c0_ref,
                 wg2h_ref, wg2l_ref, bg_ref, gnorm_ref, cw_ref, cb_ref, dtb_ref, alog_ref,
                 dskip_ref, snorm_ref,
                 oy_ref, s_out_ref, h_out_ref, c_out_ref,
                 st_sc, h_sc, ext_sc, o_sc, y_sc, *, tc, chunk, valid):
    t = pl.program_id(1)
    nt = pl.num_programs(1)
    n_chunks = tc // chunk
    shift = chunk.bit_length() - 1
    assert (1 << shift) == chunk and tc % chunk == 0

    @pl.when(t == 0)
    def _():
        st_sc[...] = s0_ref[0]
        h_sc[...] = h0_ref[0]
        ext_sc[0:CONV_HDR, :] = c0_ref[0]

    row = _iota((tc, tc), 0)
    col = _iota((tc, tc), 1)
    causal = ((row >> shift) == (col >> shift)) & (col <= row)
    l_tri = jnp.where(causal, 1.0, 0.0).astype(BF16)
    lane = _iota((tc, V7X_LANES), 1)
    rowv = _iota((tc, V7X_LANES), 0)
    dt_lane = (lane >= SM_DT0) & (lane < SM_DT0 + H_B)
    if valid < tc:
        row_ok = rowv < valid
        dt_lane = dt_lane & row_ok
    e8 = jnp.where((_iota((V7X_LANES, D_B), 1) >> 6) + SM_DT0 == _iota((V7X_LANES, D_B), 0), 1.0, 0.0).astype(BF16)

    sm = sm_ref[...]
    pre = _dot_hl(*_split2(sm), wg2h_ref[...], wg2l_ref[...])
    log_a = _log_sigmoid(pre + bg_ref[...]) / GATE_TAU
    if valid < tc:
        log_a = jnp.where(_iota((tc, H_A * DK_A), 0) < valid, log_a, 0.0)
    b_cum = _xdot(l_tri, log_a)
    dt_pad = jnp.where(dt_lane, _softplus(sm + dtb_ref[...]), 0.0)
    a_pad = -jnp.exp(alog_ref[...])
    la_pad = jnp.where(dt_lane, dt_pad * a_pad, 0.0)
    cum_pad = _xdot(l_tri, la_pad)
    dt_rep = _dotx(dt_pad, e8)
    cum_rep = _dotx(cum_pad, e8)

    ext_sc[CONV_HDR:CONV_HDR + tc, :] = xbc_ref[...]
    cw = cw_ref[...]
    acc = ext_sc[CONV_HDR - 3:CONV_HDR - 3 + tc, :] * cw[0:1, :]
    for i in range(1, CONV_W):
        acc = acc + ext_sc[CONV_HDR - 3 + i:CONV_HDR - 3 + i + tc, :] * cw[i:i + 1, :]
    xc = _silu(acc + cb_ref[...])

    @pl.when(t == nt - 1)
    def _():
        c_out_ref[0] = ext_sc[valid:valid + CONV_HDR, :]

    ext_sc[0:CONV_HDR, :] = ext_sc[tc:tc + CONV_HDR, :]

    xs = xc[:, 0:D_B]
    bm = xc[:, D_B:D_B + G_B * N_B]
    cm = xc[:, D_B + G_B * N_B:CONV_DIM]
    xdt = xs * dt_rep

    qkvr = qkvr_ref[...]
    q = qkvr[:, 0:H_A * DK_A] * (DK_A ** -0.5)
    k = qkvr[:, H_A * DK_A:2 * H_A * DK_A]
    v = qkvr[:, 2 * H_A * DK_A:2 * H_A * DK_A + H_A * DV_A]
    r = qkvr[:, 2 * H_A * DK_A + H_A * DV_A:]
    q_dec = q * jnp.exp(b_cum)
    k_dec = k * jnp.exp(-b_cum)
    for h in range(H_A):
        ks = slice(h * DK_A, (h + 1) * DK_A)
        vs = slice(h * DV_A, (h + 1) * DV_A)
        att = jnp.where(causal, _dot3_cat(q_dec[:, ks], k_dec[:, ks], NT), 0.0)
        o_sc[:, vs] = _dot3(att, v[:, vs])
    for c in range(n_chunks):
        rows = slice(c * chunk, (c + 1) * chunk)
        last = slice((c + 1) * chunk - 1, (c + 1) * chunk)
        b_last = b_cum[last, :]
        k_end = k[rows, :] * jnp.exp(b_last - b_cum[rows, :])
        dec = jnp.exp(b_last)
        for h in range(H_A):
            ks = slice(h * DK_A, (h + 1) * DK_A)
            vs = slice(h * DV_A, (h + 1) * DV_A)
            st = st_sc[h]
            o_sc[rows, vs] = o_sc[rows, vs] + _dot3_cat(q_dec[rows, ks], st, NT)
            st_sc[h] = st * dec[:, ks] + _dot3_cat(v[rows, vs], k_end[:, ks], TN)

    c_pieces = [p.astype(F32) for p in _split3(cum_pad)]
    in_group = lambda first: (lane >= first * SM_DT0) & (lane < first * SM_DT0 + 3 * SM_DT0) & ((lane & 15) < H_B)
    seg_x = jnp.where(in_group(4), 1.0, 0.0)
    seg_y = jnp.where(in_group(1), 1.0, 0.0)
    for n, piece in enumerate(c_pieces):
        seg_x = seg_x + (piece if n == 0 else pltpu.roll(piece, n * SM_DT0, 1))
        seg_y = seg_y - pltpu.roll(piece, (3 + n) * SM_DT0, 1)
    seg_y = seg_y.astype(BF16)
    for g in range(G_B):
        gs = slice(g * N_B, (g + 1) * N_B)
        cb = _dot3_cat(cm[:, gs], bm[:, gs], NT)
        for hh in range(g * (H_B // G_B), (g + 1) * (H_B // G_B)):
            ps = slice(hh * P_B, (hh + 1) * P_B)
            seg = _mm(jnp.where((lane & 15) == hh, seg_x, 0.0).astype(BF16), seg_y, NT)
            w = cb * jnp.exp(jnp.where(causal, seg, -jnp.inf))
            y_sc[:, ps] = _dot3(w, xdt[:, ps])
    for c in range(n_chunks):
        rows = slice(c * chunk, (c + 1) * chunk)
        last = slice((c + 1) * chunk - 1, (c + 1) * chunk)
        cum_c = cum_rep[rows, :]
        cum_l = cum_rep[last, :]
        e_cum = jnp.exp(cum_c)
        x_dec = xdt[rows, :] * jnp.exp(cum_l - cum_c)
        e_last = jnp.exp(cum_l)
        for hh in range(H_B):
            g = hh // (H_B // G_B)
            gs = slice(g * N_B, (g + 1) * N_B)
            ps = slice(hh * P_B, (hh + 1) * P_B)
            hs = h_sc[hh]
            y_sc[rows, ps] = y_sc[rows, ps] + _dot3_cat(cm[rows, gs], hs, NT) * e_cum[:, ps]
            h_sc[hh] = hs * e_last[:, ps] + _dot3_cat(x_dec[:, ps], bm[rows, gs], TN)

    gn = gnorm_ref[...]
    for h in range(H_A):
        vs = slice(h * DV_A, (h + 1) * DV_A)
        oh = o_sc[:, vs]
        oh = oh * lax.rsqrt(jnp.mean(oh * oh, axis=-1, keepdims=True) + RMS_EPS) * gn
        oy_ref[:, vs] = oh * _silu(r[:, vs])
    y = (y_sc[...] + dskip_ref[...] * xs) * _silu(z_ref[...])
    y = y * lax.rsqrt(jnp.mean(y * y, axis=-1, keepdims=True) + RMS_EPS) * snorm_ref[...]
    oy_ref[:, H_A * DV_A:] = y

    @pl.when(t == nt - 1)
    def _():
        s_out_ref[0] = st_sc[...]
        h_out_ref[0] = h_sc[...]


def _scan_call(proj, s0t, h0, c0, prm, *, specs, nb, t_len, tc, chunk, valid):
    nt = t_len // tc
    assert t_len % tc == 0 and (valid == tc or nt == 1)
    rowspec = lambda c: pl.BlockSpec((tc, c), lambda b, t: (b * nt + t, 0))
    stspec = lambda shp: pl.BlockSpec((1,) + shp, lambda b, t: (b,) + (0,) * len(shp))
    names = ("wg2h", "wg2l", "bg", "gnorm", "cw", "cb", "dtb", "alog", "dskip", "snorm")
    consts = [prm[n] for n in names]
    return pl.pallas_call(
        functools.partial(_scan_kernel, tc=tc, chunk=chunk, valid=valid),
        grid=(nb, nt),
        in_specs=list(specs)
                 + [stspec((H_A, DV_A, DK_A)), stspec((H_B, P_B, N_B)), stspec((CONV_HDR, CONV_DIM))]
                 + [_const_spec(c.shape) for c in consts],
        out_specs=[rowspec(H_A * DV_A + D_B), stspec((H_A, DV_A, DK_A)), stspec((H_B, P_B, N_B)),
                   stspec((CONV_HDR, CONV_DIM))],
        out_shape=[jax.ShapeDtypeStruct((nb * t_len, H_A * DV_A + D_B), F32),
                   jax.ShapeDtypeStruct((nb, H_A, DV_A, DK_A), F32),
                   jax.ShapeDtypeStruct((nb, H_B, P_B, N_B), F32),
                   jax.ShapeDtypeStruct((nb, CONV_HDR, CONV_DIM), F32)],
        scratch_shapes=[pltpu.VMEM((H_A, DV_A, DK_A), F32), pltpu.VMEM((H_B, P_B, N_B), F32),
                        pltpu.VMEM((tc + CONV_HDR, CONV_DIM), F32),
                        pltpu.VMEM((tc, H_A * DV_A), F32), pltpu.VMEM((tc, D_B), F32)],
        compiler_params=_params(2),
        name="gla_ssd_scan",
    )(proj, proj, proj, proj, s0t, h0, c0, *consts)


AB_QKVR = 2 * H_A * DK_A + 2 * H_A * DV_A
AB_XBC0 = AB_QKVR
AB_SM0 = AB_XBC0 + CONV_DIM
AB_Z0 = 2560
AB_COLS = AB_Z0 + D_B


def _split_w(w):
    hi = w.astype(BF16)
    lo = (w - hi.astype(F32)).astype(BF16)
    return hi, lo


def _lane_pad(v, start, width=V7X_LANES):
    out = jnp.zeros((1, width), F32)
    return out.at[0, start:start + v.shape[0]].set(v.astype(F32))


def _ab_prepare(w_in, w_gate2, b_gate, gla_norm, conv_w, conv_b, dt_bias, a_log, d_skip, ssm_norm, w_out):
    o = np.cumsum([0, H_A * DK_A, H_A * DK_A, H_A * DV_A, H_A * DV_A, GATE_RANK, D_B, CONV_DIM, H_B])
    q0, gl0, z0, xbc0, dt0, end = o[0], o[4], o[5], o[6], o[7], o[8]
    k_dim = w_in.shape[0]
    w = jnp.concatenate([
        w_in[:, q0:gl0], w_in[:, xbc0:dt0], w_in[:, gl0:z0], w_in[:, dt0:end],
        jnp.zeros((k_dim, AB_Z0 - AB_SM0 - GATE_RANK - H_B), F32), w_in[:, z0:xbc0]], axis=1)
    assert w.shape[1] == AB_COLS
    wg2 = jnp.zeros((V7X_LANES, H_A * DK_A), F32).at[0:GATE_RANK].set(w_gate2)
    wg2h, wg2l = _split_w(wg2)
    prm = dict(
        wg2h=wg2h, wg2l=wg2l, bg=b_gate.reshape(1, -1), gnorm=gla_norm.reshape(1, -1),
        cw=conv_w, cb=conv_b.reshape(1, -1), dtb=_lane_pad(dt_bias, SM_DT0), alog=_lane_pad(a_log, SM_DT0),
        dskip=jnp.repeat(d_skip, P_B).reshape(1, -1), snorm=ssm_norm.reshape(1, -1))
    return _split_w(w), prm, _split_w(w_out)


def _scan_from_proj(proj, s0, h0, c0, prm, *, nb, t_len, tc, chunk, valid):
    nt = t_len // tc
    s0t = jnp.swapaxes(s0, -1, -2)
    c0p = jnp.zeros((nb, CONV_HDR, CONV_DIM), F32).at[:, CONV_HDR - (CONV_W - 1):].set(c0)
    colspec = lambda width, start: pl.BlockSpec((tc, width), lambda b, t: (b * nt + t, start // width))
    oy, s_new, h_new, c_new = _scan_call(
        proj, s0t, h0, c0p, prm,
        specs=[colspec(AB_QKVR, 0), colspec(D_B, AB_Z0), colspec(CONV_DIM, AB_XBC0), colspec(V7X_LANES, AB_SM0)],
        nb=nb, t_len=t_len, tc=tc, chunk=chunk, valid=valid)
    return oy, jnp.swapaxes(s_new, -1, -2), h_new, c_new[:, CONV_HDR - (CONV_W - 1):]


RT_FINE0 = N_GROUPS
RT_HALF = V7X_LANES // 2
MOE_TILE = 256


def _two_source_specs(na, nb, tm, width):
    assert na % tm == 0 and nb % tm == 0
    na_t = na // tm
    return (pl.BlockSpec((tm, width), lambda i, *_: (jnp.minimum(i, na_t - 1), 0)),
            pl.BlockSpec((tm, width), lambda i, *_: (jnp.maximum(i - na_t, 0), 0)), na_t)


def _router_kernel(xa_ref, xb_ref, wh_ref, wl_ref, b_ref, id_ref, rank_ref, gate_ref, cnt_ref, base_sc, *, na_t):
    @pl.when(pl.program_id(0) == 0)
    def _():
        base_sc[...] = jnp.zeros(base_sc.shape, F32)

    xh, xl = _split2(jnp.where(pl.program_id(0) < na_t, xa_ref[...], xb_ref[...]))
    logits = _dot_hl(xh, xl, wh_ref[...], wl_ref[...]) + b_ref[...]
    shp = logits.shape
    lane = _iota(shp, 1)
    lane_f = lane.astype(F32)
    big = float(V7X_LANES)
    ninf = -jnp.inf

    def first_max(mask):
        v = jnp.max(jnp.where(mask, logits, ninf), axis=-1, keepdims=True)
        i = jnp.min(jnp.where(mask & (logits == v), lane_f, big), axis=-1, keepdims=True)
        return v, i

    is_c = lane < N_GROUPS
    mc, grp = first_max(is_c)
    p_grp = 1.0 / jnp.sum(jnp.where(is_c, jnp.exp(logits - mc), 0.0), axis=-1, keepdims=True)
    fine = (lane >= RT_FINE0) & (lane < RT_FINE0 + N_EXPERTS)
    cand = fine & (((lane - RT_FINE0) >> 3).astype(F32) == grp)
    v1, i1 = first_max(cand)
    v2, i2 = first_max(cand & (lane_f != i1))
    e = jnp.exp(v2 - v1)
    w1 = p_grp / (1.0 + e)
    w2 = p_grp * (e / (1.0 + e))
    first = lane < RT_HALF
    id_ref[...] = (jnp.where(first, i1, i2) - float(RT_FINE0)).astype(jnp.int32)
    gate_ref[...] = jnp.where(first, w1, w2)

    tm = shp[0]
    oh1 = jnp.where(lane_f == i1, 1.0, 0.0)
    oh2 = jnp.where(lane_f == i2, 1.0, 0.0)
    oh = oh1 + oh2
    earlier = jnp.where(_iota((tm, tm), 1) < _iota((tm, tm), 0), 1.0, 0.0).astype(BF16)
    before = _mm(earlier, oh.astype(BF16)) + base_sc[...]
    r1 = jnp.sum(oh1 * before, axis=-1, keepdims=True)
    r2 = jnp.sum(oh2 * before, axis=-1, keepdims=True)
    rank_ref[...] = jnp.where(first, r1, r2).astype(jnp.int32)
    base_sc[...] = base_sc[...] + jnp.sum(oh, axis=0, keepdims=True)
    cnt_ref[...] = base_sc[...]


def _router(xa, xb, wh, wl, b, *, tm):
    d = xa.shape[1]
    n = xa.shape[0] + xb.shape[0]
    spec_a, spec_b, na_t = _two_source_specs(xa.shape[0], xb.shape[0], tm, d)
    row = pl.BlockSpec((tm, V7X_LANES), lambda i: (i, 0))
    return pl.pallas_call(
        functools.partial(_router_kernel, na_t=na_t),
        grid=(n // tm,),
        in_specs=[spec_a, spec_b, _const_spec(wh.shape), _const_spec(wl.shape), _const_spec(b.shape)],
        out_specs=[row, row, row, pl.BlockSpec((1, V7X_LANES), lambda i: (0, 0))],
        out_shape=[jax.ShapeDtypeStruct((n, V7X_LANES), jnp.int32), jax.ShapeDtypeStruct((n, V7X_LANES), jnp.int32),
                   jax.ShapeDtypeStruct((n, V7X_LANES), F32), jax.ShapeDtypeStruct((1, V7X_LANES), F32)],
        scratch_shapes=[pltpu.VMEM((1, V7X_LANES), F32)],
        compiler_params=_params(1),
        name="moe_router",
    )(xa, xb, wh, wl, b)


DMA_UNROLL = 8


def _pos_copy(pos_hbm, pos_sm, sem_p, step, slot):
    return pltpu.make_async_copy(pos_hbm.at[step], pos_sm.at[slot], sem_p.at[slot])


def _dispatch_kernel(zf_ref, xa_ref, xb_ref, pos_hbm, xs_hbm, pos_sm, zbuf, sem_p, sem_z, sem_s, *, na_t):
    i = pl.program_id(0)
    n = pl.num_programs(0)
    tm = xa_ref.shape[0]
    tile = zbuf.shape[0]
    n_tiles = xs_hbm.shape[0] // tile
    slot = lax.rem(i, 2)

    def zero_copy(j):
        return pltpu.make_async_copy(zbuf, xs_hbm.at[pl.ds(pl.multiple_of(j * tile, tile), tile)], sem_z)

    @pl.when(i == 0)
    def _():
        _pos_copy(pos_hbm, pos_sm, sem_p, 0, 0).start()
        zbuf[...] = jnp.zeros(zbuf.shape, F32)

        def z_start(j, c):
            @pl.when(zf_ref[j] > 0)
            def _():
                zero_copy(j).start()
            return c

        def z_wait(j, c):
            @pl.when(zf_ref[j] > 0)
            def _():
                zero_copy(j).wait()
            return c

        lax.fori_loop(0, n_tiles, z_start, 0)
        lax.fori_loop(0, n_tiles, z_wait, 0)

    @pl.when(i + 1 < n)
    def _():
        _pos_copy(pos_hbm, pos_sm, sem_p, i + 1, 1 - slot).start()

    _pos_copy(pos_hbm, pos_sm, sem_p, i, slot).wait()

    def scatter_rows(x_ref):
        def issue(r, c):
            for k in range(TOP_FINE):
                dst = pos_sm[slot, k, r]
                pltpu.make_async_copy(x_ref.at[pl.ds(r, 1)], xs_hbm.at[pl.ds(dst, 1)], sem_s).start()
            return c

        lax.fori_loop(0, tm, issue, 0, unroll=DMA_UNROLL)
        for _ in range(TOP_FINE):
            pltpu.make_async_copy(x_ref, xs_hbm.at[pl.ds(0, tm)], sem_s).wait()

    @pl.when(i < na_t)
    def _():
        scatter_rows(xa_ref)

    @pl.when(i >= na_t)
    def _():
        scatter_rows(xb_ref)


def _dispatch(xa, xb, pos_tab, zero_flag, *, n_tiles, tm):
    d = xa.shape[1]
    n = xa.shape[0] + xb.shape[0]
    spec_a, spec_b, na_t = _two_source_specs(xa.shape[0], xb.shape[0], tm, d)
    any_spec = pl.BlockSpec(memory_space=pl.ANY)
    grid_spec = pltpu.PrefetchScalarGridSpec(
        num_scalar_prefetch=1,
        grid=(n // tm,),
        in_specs=[spec_a, spec_b, any_spec],
        out_specs=any_spec,
        scratch_shapes=[pltpu.SMEM((2, TOP_FINE, tm), jnp.int32), pltpu.VMEM((MOE_TILE, d), F32),
                        pltpu.SemaphoreType.DMA((2,)), pltpu.SemaphoreType.DMA(()), pltpu.SemaphoreType.DMA(())],
    )
    return pl.pallas_call(
        functools.partial(_dispatch_kernel, na_t=na_t),
        grid_spec=grid_spec,
        out_shape=jax.ShapeDtypeStruct((n_tiles * MOE_TILE, d), F32),
        compiler_params=_params(1),
        name="moe_dispatch",
    )(zero_flag, xa, xb, pos_tab)


def _expert_kernel(te_ref, xs_ref, wgu_ref, wd_ref, ys_ref, *, three_pass):
    if three_pass:
        xh, xl = _split2(xs_ref[...])
        wh, wl = _split2(wgu_ref[0])
        hgu = _dot_hl(xh, xl, wh, wl)
    else:
        hgu = _mm(xs_ref[...].astype(BF16), wgu_ref[0].astype(BF16))
    act = _silu(hgu[:, :D_EXPERT]) * hgu[:, D_EXPERT:]
    if three_pass:
        ys_ref[...] = _dot3(act, wd_ref[0])
    else:
        ys_ref[...] = _mm(act.astype(BF16), wd_ref[0].astype(BF16))


def _experts(xs, w_gate_up, w_down, tile_expert, *, three_pass):
    p, d = xs.shape
    f2 = w_gate_up.shape[-1]
    row = pl.BlockSpec((MOE_TILE, d), lambda i, te: (i, 0))
    grid_spec = pltpu.PrefetchScalarGridSpec(
        num_scalar_prefetch=1,
        grid=(p // MOE_TILE,),
        in_specs=[row, pl.BlockSpec((1, d, f2), lambda i, te: (te[i], 0, 0)),
                  pl.BlockSpec((1, f2 // 2, d), lambda i, te: (te[i], 0, 0))],
        out_specs=row,
    )
    return pl.pallas_call(
        functools.partial(_expert_kernel, three_pass=three_pass),
        grid_spec=grid_spec,
        out_shape=jax.ShapeDtypeStruct((p, d), F32),
        compiler_params=_params(1),
        name="moe_experts",
    )(tile_expert, xs, w_gate_up, w_down)


def _combine_ln_kernel(xa_ref, xb_ref, gate_ref, g_ref, b_ref, pos_hbm, ys_hbm, oa_ref, ob_ref,
                       pos_sm, ybuf, sem_p, sem_y, *, na_t):
    i = pl.program_id(0)
    n = pl.num_programs(0)
    tm = xa_ref.shape[0]
    slot = lax.rem(i, 2)

    def issue_gathers(s):
        def issue(r, c):
            for k in range(TOP_FINE):
                src = pos_sm[s, k, r]
                pltpu.make_async_copy(ys_hbm.at[pl.ds(src, 1)], ybuf.at[s, pl.ds(k * tm + r, 1)],
                                      sem_y.at[s]).start()
            return c

        lax.fori_loop(0, tm, issue, 0, unroll=DMA_UNROLL)

    @pl.when(i == 0)
    def _():
        _pos_copy(pos_hbm, pos_sm, sem_p, 0, 0).start()
        _pos_copy(pos_hbm, pos_sm, sem_p, 0, 0).wait()
        issue_gathers(0)

        @pl.when(n > 1)
        def _():
            _pos_copy(pos_hbm, pos_sm, sem_p, 1, 1).start()

    @pl.when(i + 1 < n)
    def _():
        _pos_copy(pos_hbm, pos_sm, sem_p, i + 1, 1 - slot).wait()
        issue_gathers(1 - slot)

    @pl.when(i + 2 < n)
    def _():
        _pos_copy(pos_hbm, pos_sm, sem_p, i + 2, slot).start()

    pltpu.make_async_copy(ys_hbm.at[pl.ds(0, TOP_FINE * tm)], ybuf.at[slot], sem_y.at[slot]).wait()
    gate = gate_ref[...]
    y = ybuf[slot]
    moe = gate[:, 0:1] * y[0:tm] + gate[:, RT_HALF:RT_HALF + 1] * y[tm:2 * tm]

    @pl.when(i < na_t)
    def _():
        oa_ref[...] = _layer_norm(DN_ALPHA * xa_ref[...] + moe, g_ref[...], b_ref[...])

    @pl.when(i >= na_t)
    def _():
        ob_ref[...] = _layer_norm(DN_ALPHA * xb_ref[...] + moe, g_ref[...], b_ref[...])


def _combine_ln(xa, xb, ys, pos_tab, gate, g, b, *, tm):
    d = xa.shape[1]
    na, nb = xa.shape[0], xb.shape[0]
    assert TOP_FINE == 2 and nb == tm
    spec_a, spec_b, na_t = _two_source_specs(na, nb, tm, d)
    any_spec = pl.BlockSpec(memory_space=pl.ANY)
    return pl.pallas_call(
        functools.partial(_combine_ln_kernel, na_t=na_t),
        grid=((na + nb) // tm,),
        in_specs=[spec_a, spec_b, pl.BlockSpec((tm, V7X_LANES), lambda i: (i, 0)), _const_spec((1, d)),
                  _const_spec((1, d)), any_spec, any_spec],
        out_specs=[spec_a, spec_b],
        out_shape=[jax.ShapeDtypeStruct((na, d), F32), jax.ShapeDtypeStruct((nb, d), F32)],
        scratch_shapes=[pltpu.SMEM((2, TOP_FINE, tm), jnp.int32), pltpu.VMEM((2, TOP_FINE * tm, d), F32),
                        pltpu.SemaphoreType.DMA((2,)), pltpu.SemaphoreType.DMA((2,))],
        compiler_params=_params(1),
        name="moe_combine_ln",
    )(xa, xb, gate, g, b, pos_tab, ys)


def _slots_kernel(id_ref, rank_ref, base_ref, pos_ref, *, tm):
    lane = _iota((tm, V7X_LANES), 1)
    lane_f = lane.astype(F32)
    base = base_ref[...]
    for t in range(pos_ref.shape[0]):
        rows = slice(t * tm, (t + 1) * tm)
        ids = id_ref[rows, :].astype(F32)

        def first_row(choice, ids=ids):
            hit = lane_f == ids[:, choice:choice + 1] + float(RT_FINE0)
            return jnp.sum(jnp.where(hit, base, 0.0), axis=-1, keepdims=True)

        pos = rank_ref[rows, :].astype(F32) + jnp.where(lane < RT_HALF, first_row(0), first_row(RT_HALF))
        pos_t = jnp.transpose(pos)
        pos_ref[t] = jnp.concatenate([pos_t[0:1, :], pos_t[RT_HALF:RT_HALF + 1, :]], axis=0).astype(jnp.int32)


def _slots(ids, rank, base, *, tm):
    n = ids.shape[0]
    assert tm == V7X_LANES and n % tm == 0
    per_step = max(k for k in range(1, 9) if (n // tm) % k == 0)
    row = pl.BlockSpec((per_step * tm, V7X_LANES), lambda i: (i, 0))
    out = pl.pallas_call(
        functools.partial(_slots_kernel, tm=tm),
        grid=(n // tm // per_step,),
        in_specs=[row, row, _const_spec((1, V7X_LANES))],
        out_specs=pl.BlockSpec((per_step, TOP_FINE, tm), lambda i: (i, 0, 0)),
        out_shape=jax.ShapeDtypeStruct((n // tm, TOP_FINE, tm), jnp.int32),
        compiler_params=_params(1),
        name="moe_slots",
    )(ids, rank, base)
    return out


def _moe_plan(ids, rank, cnt, n, tm):
    counts = cnt[0, RT_FINE0:RT_FINE0 + N_EXPERTS].astype(jnp.int32)
    tiles_per = (counts + MOE_TILE - 1) // MOE_TILE
    e_idx = jnp.arange(N_EXPERTS)
    tile_end = tiles_per @ (e_idx[:, None] <= e_idx[None, :]).astype(jnp.int32)
    tile_beg = tile_end - tiles_per
    n_tiles = -(-TOP_FINE * n // MOE_TILE) + N_EXPERTS
    ti = jnp.arange(n_tiles)
    te = jnp.minimum(jnp.sum((tile_end[None, :] <= ti[:, None]).astype(jnp.int32), axis=1), N_EXPERTS - 1)
    is_last = jnp.any((ti[:, None] == tile_end[None, :] - 1) & (tiles_per[None, :] > 0), axis=1)
    zero_flag = (is_last | (ti >= tile_end[-1])).astype(jnp.int32)
    base = jnp.zeros((1, V7X_LANES), F32).at[0, RT_FINE0:RT_FINE0 + N_EXPERTS].set(
        (tile_beg * MOE_TILE).astype(F32))
    pos_tab = _slots(ids, rank, base, tm=tm)
    return te.astype(jnp.int32), zero_flag, pos_tab, n_tiles


def _moe_prepare(w_coarse, b_coarse, w_fine, b_fine):
    d = w_coarse.shape[0]
    wf = jnp.transpose(w_fine, (1, 0, 2)).reshape(d, N_EXPERTS)
    w = jnp.zeros((d, V7X_LANES), F32).at[:, :N_GROUPS].set(w_coarse).at[:, RT_FINE0:RT_FINE0 + N_EXPERTS].set(wf)
    b = jnp.zeros((1, V7X_LANES), F32).at[0, :N_GROUPS].set(b_coarse)
    b = b.at[0, RT_FINE0:RT_FINE0 + N_EXPERTS].set(b_fine.reshape(-1))
    return _split_w(w), b


def _moe_layer(xa, xb, w_coarse, b_coarse, w_fine, b_fine, w_gate_up, w_down, layer, g, b, *, tm, three_pass):
    d = xa.shape[1]
    n = xa.shape[0] + xb.shape[0]
    (wrh, wrl), br = _moe_prepare(w_coarse, b_coarse, w_fine, b_fine)
    ids, rank, gate, cnt = _router(xa, xb, wrh, wrl, br, tm=tm)
    te, zero_flag, pos_tab, n_tiles = _moe_plan(ids, rank, cnt, n, tm)
    xs = _dispatch(xa, xb, pos_tab, zero_flag, n_tiles=n_tiles, tm=tm)
    ys = _experts(xs, w_gate_up.reshape(-1, d, 2 * D_EXPERT), w_down.reshape(-1, D_EXPERT, d),
                  te + layer * N_EXPERTS, three_pass=three_pass)
    return _combine_ln(xa, xb, ys, pos_tab, gate, g, b, tm=tm)


def _cdproj_kernel(x_ref, wh_ref, wl_ref, cos_ref, sin_ref, perm_ref, u_ref, q_ref, k_ref, v_ref):
    xh, xl = _split2(x_ref[...])

    def col(j):
        cs = slice(j * D_C, (j + 1) * D_C)
        return _dot_hl(xh, xl, wh_ref[:, cs], wl_ref[:, cs])

    u_ref[...] = col(0)
    for j, ref in ((1, q_ref), (2, k_ref)):
        t = col(j)
        ref[...] = t * cos_ref[...] + _dotx(t, perm_ref[...]) * sin_ref[...]
    v_ref[...] = col(3)


def _cdproj(x, wh, wl, cos_t, sin_t, perm, *, n_rows, tm):
    k = x.shape[1]
    assert D_C == D_D and n_rows % tm == 0 and cos_t.shape[0] % tm == 0
    nt = cos_t.shape[0] // tm
    row = pl.BlockSpec((tm, D_D), lambda i: (i, 0))
    tab = pl.BlockSpec((tm, D_D), lambda i: (i % nt, 0))
    return pl.pallas_call(
        _cdproj_kernel,
        grid=(n_rows // tm,),
        in_specs=[pl.BlockSpec((tm, k), lambda i: (i, 0)), _const_spec(wh.shape), _const_spec(wl.shape),
                  tab, tab, _const_spec(perm.shape)],
        out_specs=[row] * 4,
        out_shape=[jax.ShapeDtypeStruct((n_rows, D_D), F32)] * 4,
        compiler_params=_params(1),
        name="cd_proj_rope",
    )(x, wh, wl, cos_t, sin_t, perm)


def _rope_tables(pos):
    half = ROT_DIM // 2
    inv = ROPE_THETA ** (-jnp.arange(half, dtype=F32) / half)
    ang = pos.astype(F32)[:, None] * inv
    cos, sin = jnp.cos(ang), jnp.sin(ang)
    n = pos.shape[0]
    rest = HD_D - ROT_DIM
    cos_h = jnp.concatenate([cos, cos, jnp.ones((n, rest), F32)], axis=-1)
    sin_h = jnp.concatenate([-sin, sin, jnp.zeros((n, rest), F32)], axis=-1)
    perm = np.zeros((D_D, D_D), np.float32)
    for dst in range(D_D):
        j = dst % HD_D
        if j < half:
            perm[dst + half, dst] = 1.0
        elif j < ROT_DIM:
            perm[dst - half, dst] = 1.0
    return jnp.tile(cos_h, (1, H_D)), jnp.tile(sin_h, (1, H_D)), jnp.asarray(perm, BF16)


POOL_HDR = 16


def _pool_kernel(u_ref, b0_ref, wph_ref, wpl_ref, sc_ref, o_ref, bo_ref, ext_sc, *, tc, valid, start):
    t = pl.program_id(1)
    nt = pl.num_programs(1)

    @pl.when(t == 0)
    def _():
        ext_sc[0:POOL_HDR, :] = b0_ref[0]

    ext_sc[POOL_HDR:POOL_HDR + tc, :] = u_ref[...]
    u = u_ref[...]
    rowi = _iota((tc, POOL_HDR + tc), 0) + POOL_HDR
    colj = _iota((tc, POOL_HDR + tc), 1)
    pos = (start + t * tc + _iota((tc, POOL_GROUP), 0)).astype(F32)
    for gi, w in enumerate(POOL_WINDOWS):
        ls = slice(gi * POOL_GROUP, (gi + 1) * POOL_GROUP)
        band = jnp.where((colj <= rowi) & (colj > rowi - w), 1.0, 0.0).astype(BF16)
        win = _xdot(band, ext_sc[:, ls])
        cnt = jnp.minimum(float(w), pos + 1.0)
        dh, dl = _split2(win / cnt - u[:, ls])
        o_ref[:, ls] = _dot_hl(dh, dl, wph_ref[gi], wpl_ref[gi]) * sc_ref[:, ls]

    @pl.when(t == nt - 1)
    def _():
        bo_ref[0] = ext_sc[valid:valid + POOL_HDR, :]

    ext_sc[0:POOL_HDR, :] = ext_sc[tc:tc + POOL_HDR, :]


def _pool(u, buf, wph, wpl, scale, *, nb, t_len, tc, valid, start):
    nt = t_len // tc
    assert t_len % tc == 0 and (valid == tc or nt == 1)
    b0 = jnp.zeros((nb, POOL_HDR, D_C), F32).at[:, POOL_HDR - POOL_BUF:].set(buf)
    row = pl.BlockSpec((tc, D_C), lambda b, t: (b * nt + t, 0))
    st = pl.BlockSpec((1, POOL_HDR, D_C), lambda b, t: (b, 0, 0))
    pooled, bnew = pl.pallas_call(
        functools.partial(_pool_kernel, tc=tc, valid=valid, start=start),
        grid=(nb, nt),
        in_specs=[row, st, _const_spec(wph.shape), _const_spec(wpl.shape), _const_spec(scale.shape)],
        out_specs=[row, st],
        out_shape=[jax.ShapeDtypeStruct((nb * t_len, D_C), F32), jax.ShapeDtypeStruct((nb, POOL_HDR, D_C), F32)],
        scratch_shapes=[pltpu.VMEM((POOL_HDR + tc, D_C), F32)],
        compiler_params=_params(2),
        name="pool_mix",
    )(u, b0, wph, wpl, scale)
    return pooled, bnew[:, POOL_HDR - POOL_BUF:]


def _top_blocks(gate, lane_f, valid, n_pick):
    sel = jnp.zeros(gate.shape, jnp.bool_)
    g = jnp.where(valid, gate, -jnp.inf)
    picks = []
    for _ in range(n_pick):
        v = jnp.max(g, axis=-1, keepdims=True)
        idx = jnp.min(jnp.where(g == v, lane_f, float(V7X_LANES)), axis=-1, keepdims=True)
        hit = (lane_f == idx) & valid
        sel = sel | hit
        g = jnp.where(hit, -jnp.inf, g)
        picks.append(idx)
    return sel, picks


def _cat_pieces(a, b, c):
    return jnp.concatenate([a, b, c, jnp.zeros_like(a)], axis=1)


MOBA_TRIP = 4


def _top_rows(gate, row_f, valid, n_pick):
    sel = jnp.zeros(gate.shape, F32)
    g = jnp.where(valid, gate, -jnp.inf)
    for _ in range(n_pick):
        v = jnp.max(g, axis=0, keepdims=True)
        idx = jnp.min(jnp.where(g == v, row_f, float(V7X_LANES)), axis=0, keepdims=True)
        hit = (row_f == idx) & valid
        sel = jnp.where(hit, 1.0, sel)
        g = jnp.where(hit, -jnp.inf, g)
    return sel


def _moba_prompt_kernel(q_ref, k_ref, v_ref, o_ref, kmean_sc, kcat_sc, vt_sc, sel_sc):
    i = pl.program_id(2)
    blk = MOBA_BLOCK
    t_len = k_ref.shape[0]
    n_heads = V7X_LANES // HD_D
    g_rows = sel_sc.shape[1]
    scale = HD_D ** -0.5

    @pl.when(i == 0)
    def _():
        ind = jnp.where((_iota((V7X_LANES, t_len), 1) >> 8) == _iota((V7X_LANES, t_len), 0), 1.0, 0.0)
        kmean_sc[...] = _xdot(ind.astype(BF16), k_ref[...]) * (1.0 / blk)
        ones_row = jnp.where(_iota((HD_D, blk), 0) == 0, 1.0, 0.0).astype(BF16)
        for n in range(t_len // blk):
            rows = slice(n * blk, (n + 1) * blk)
            kh, kl = _split2(k_ref[rows, :])
            vt = jnp.transpose(v_ref[rows, :]).astype(BF16)
            for hh in range(n_heads):
                hs = slice(hh * HD_D, (hh + 1) * HD_D)
                kcat_sc[hh, rows, :] = _cat_pieces(kh[:, hs], kh[:, hs], kl[:, hs])
                vt_sc[hh, n, 0:HD_D, :] = vt[hs, :]
                vt_sc[hh, n, HD_D:2 * HD_D, :] = ones_row

    def scores_t(hh, qcat, j):
        return _mm(kcat_sc[hh, pl.ds(pl.multiple_of(j * blk, blk), blk), :], qcat, NT)

    def weighted_t(hh, p, j):
        return _mm(vt_sc[hh, j], p.astype(BF16))

    q = q_ref[...]
    key_le_query = _iota((blk, blk), 0) <= _iota((blk, blk), 1)
    brow = _iota((g_rows, blk), 0)
    qcats, state = [], []
    for hh in range(n_heads):
        hs = slice(hh * HD_D, (hh + 1) * HD_D)
        qh = q[:, hs]
        gate_t = _dot3_cat(kmean_sc[0:g_rows, hs], qh, NT)
        sel_sc[hh] = _top_rows(gate_t, brow.astype(F32), brow < i, MOBA_TOPK)
        q_hi, q_lo = _split2(qh * scale)
        qcats.append(_cat_pieces(q_hi, q_lo, q_hi))
        s = jnp.where(key_le_query, scores_t(hh, qcats[hh], i), -jnp.inf)
        m = jnp.max(s, axis=0, keepdims=True)
        state += [m, weighted_t(hh, jnp.exp(s - m), i)]

    def body(jj, carry):
        out = []
        for hh in range(n_heads):
            m, acc = carry[2 * hh:2 * hh + 2]
            js = [jnp.minimum(MOBA_TRIP * jj + t, i) for t in range(MOBA_TRIP)]
            ss = []
            for j in js:
                on = sel_sc[hh, pl.ds(j, 1), :] > 0.0
                ss.append(jnp.where(on, scores_t(hh, qcats[hh], j), -jnp.inf))
            top = functools.reduce(jnp.maximum, ss)
            m2 = jnp.maximum(m, jnp.max(top, axis=0, keepdims=True))
            new = functools.reduce(lambda x, y: x + y,
                                   [weighted_t(hh, jnp.exp(sj - m2), j) for sj, j in zip(ss, js)])
            out += [m2, jnp.exp(m - m2) * acc + new]
        return tuple(out)

    state = lax.fori_loop(0, (i + MOBA_TRIP - 1) >> (MOBA_TRIP.bit_length() - 1), body, tuple(state))
    for hh in range(n_heads):
        acc = jnp.transpose(state[2 * hh + 1])
        o_ref[:, hh * HD_D:(hh + 1) * HD_D] = acc[:, 0:HD_D] / acc[:, HD_D:HD_D + 1]


def _moba_prompt(q, k, v, *, nb, t_len):
    blk = MOBA_BLOCK
    nq = t_len // blk
    assert t_len % blk == 0 and nq <= V7X_LANES
    n_heads = V7X_LANES // HD_D
    g_rows = -(-nq // V7X_SUBLANES) * V7X_SUBLANES
    qspec = pl.BlockSpec((blk, V7X_LANES), lambda b, hp, i: (b * nq + i, hp))
    kspec = pl.BlockSpec((t_len, V7X_LANES), lambda b, hp, i: (b, hp))
    return pl.pallas_call(
        _moba_prompt_kernel,
        grid=(nb, D_D // V7X_LANES, nq),
        in_specs=[qspec, kspec, kspec],
        out_specs=qspec,
        out_shape=jax.ShapeDtypeStruct((nb * t_len, D_D), F32),
        scratch_shapes=[pltpu.VMEM((V7X_LANES, V7X_LANES), F32),
                        pltpu.VMEM((n_heads, t_len, 4 * HD_D), BF16),
                        pltpu.VMEM((n_heads, nq, 2 * HD_D, blk), BF16),
                        pltpu.VMEM((n_heads, g_rows, blk), F32)],
        compiler_params=_params(3),
        name="moba_prompt",
    )(q, k, v)


KM_PAGES = 32
PAGES_PER_BLOCK = MOBA_BLOCK // PAGE_SIZE


def _kmean_kernel(pt_ref, *refs):
    pages, o_ref = refs[:KM_PAGES], refs[KM_PAGES]
    c = pl.program_id(1)
    blocks_per_step = KM_PAGES // PAGES_PER_BLOCK

    @pl.when(c == 0)
    def _():
        o_ref[...] = jnp.zeros(o_ref.shape, F32)

    acc = o_ref[0]
    lane = _iota(acc.shape, 2)
    for blk in range(blocks_per_step):
        tot = pages[blk * PAGES_PER_BLOCK][0]
        for p in range(1, PAGES_PER_BLOCK):
            tot = tot + pages[blk * PAGES_PER_BLOCK + p][0]
        mean = jnp.sum(tot, axis=-1, keepdims=True) * (1.0 / MOBA_BLOCK)
        acc = jnp.where(lane == c * blocks_per_step + blk, mean, acc)
    o_ref[0] = acc


def _block_means(cache_kt, page_table):
    db, n_pages = page_table.shape
    n_blocks = n_pages // PAGES_PER_BLOCK
    assert n_pages % KM_PAGES == 0
    page_spec = lambda j: pl.BlockSpec((1, H_D, HD_D, PAGE_SIZE), lambda b, c, pt: (pt[b, c * KM_PAGES + j], 0, 0, 0))
    grid_spec = pltpu.PrefetchScalarGridSpec(
        num_scalar_prefetch=1,
        grid=(db, n_pages // KM_PAGES),
        in_specs=[page_spec(j) for j in range(KM_PAGES)],
        out_specs=pl.BlockSpec((1, H_D, HD_D, n_blocks), lambda b, c, pt: (b, 0, 0, 0)),
    )
    return pl.pallas_call(
        _kmean_kernel,
        grid_spec=grid_spec,
        out_shape=jax.ShapeDtypeStruct((db, H_D, HD_D, n_blocks), F32),
        compiler_params=_params(2),
        name="moba_block_means",
    )(page_table, *([cache_kt] * KM_PAGES))


def _select_kernel(q_ref, km_ref, o_ref):
    q = q_ref[...]
    n_blk = km_ref.shape[3]
    lane_f = _iota((q.shape[0], n_blk), 1).astype(F32)
    out_lane = _iota((q.shape[0], V7X_LANES), 1)
    for h in range(H_D):
        gate = _dot3(q[:, h * HD_D:(h + 1) * HD_D], km_ref[0, h])
        _, picks = _top_blocks(gate, lane_f, lane_f >= 0.0, MOBA_TOPK)
        res = jnp.zeros((q.shape[0], V7X_LANES), F32)
        for r, idx in enumerate(picks):
            res = jnp.where(out_lane == r, idx, res)
        o_ref[0, h] = res.astype(jnp.int32)


def _select_blocks(q, kmean_t):
    db = kmean_t.shape[0]
    return pl.pallas_call(
        _select_kernel,
        grid=(db,),
        in_specs=[pl.BlockSpec((SAMPLE_PAD, D_D), lambda b: (b, 0)),
                  pl.BlockSpec((1,) + kmean_t.shape[1:], lambda b: (b, 0, 0, 0))],
        out_specs=pl.BlockSpec((1, H_D, SAMPLE_PAD, V7X_LANES), lambda b: (b, 0, 0, 0)),
        out_shape=jax.ShapeDtypeStruct((db, H_D, SAMPLE_PAD, V7X_LANES), jnp.int32),
        compiler_params=_params(1),
        name="moba_select",
    )(q, kmean_t)


HEADS_PER_STEP = V7X_LANES // HD_D


def _attend_kernel(phys_ref, q_ref, kn_ref, vn_ref, ck_hbm, cv_hbm, o_ref, kbuf, vbuf, sem, *, n_q):
    b = pl.program_id(0)
    hp = pl.program_id(1)
    n_hp = pl.num_programs(1)
    step = b * n_hp + hp
    slot = lax.rem(step, 2)
    scale = HD_D ** -0.5

    def fetch(fb, fhp, fslot):
        for hh in range(HEADS_PER_STEP):
            h = fhp * HEADS_PER_STEP + hh
            for s in range(n_q):
                for r in range(MOBA_TOPK):
                    for pg in range(PAGES_PER_BLOCK):
                        flat = (((fb * H_D + h) * n_q + s) * MOBA_TOPK + r) * PAGES_PER_BLOCK + pg
                        cols = pl.ds((r * PAGES_PER_BLOCK + pg) * PAGE_SIZE, PAGE_SIZE)
                        page = phys_ref[flat]
                        for which, (src, dst) in enumerate(((ck_hbm, kbuf), (cv_hbm, vbuf))):
                            pltpu.make_async_copy(src.at[page, h], dst.at[fslot, hh, s, :, cols],
                                                  sem.at[fslot, which]).start()

    @pl.when(step == 0)
    def _():
        fetch(b, hp, slot)

    @pl.when(step + 1 < pl.num_programs(0) * n_hp)
    def _():
        wrap = hp + 1 == n_hp
        fetch(jnp.where(wrap, b + 1, b), jnp.where(wrap, 0, hp + 1), 1 - slot)

    pltpu.make_async_copy(vbuf.at[1 - slot], kbuf.at[slot], sem.at[slot, 0]).wait()
    pltpu.make_async_copy(kbuf.at[1 - slot], vbuf.at[slot], sem.at[slot, 1]).wait()

    q = q_ref[...]
    rows_n = q.shape[0]
    rowi = _iota((rows_n, rows_n), 0)
    coli = _iota((rows_n, rows_n), 1)
    own_ok = (coli <= rowi) & (coli < n_q)
    rsel = _iota((rows_n, HD_D), 0)
    for hh in range(HEADS_PER_STEP):
        hs = slice(hh * HD_D, (hh + 1) * HD_D)
        qh = q[:, hs]
        s_own = jnp.where(own_ok, _dot3(qh, kn_ref[...][:, hs], NT) * scale, -jnp.inf)
        m_own = jnp.max(s_own, axis=-1, keepdims=True)
        out = jnp.zeros((rows_n, HD_D), F32)
        for s in range(n_q):
            s_sel = _dot3_cat(qh, kbuf[slot, hh, s]) * scale
            m = jnp.maximum(m_own, jnp.max(s_sel, axis=-1, keepdims=True))
            p_sel = jnp.exp(s_sel - m)
            p_own = jnp.exp(s_own - m)
            l = jnp.sum(p_sel, axis=-1, keepdims=True) + jnp.sum(p_own, axis=-1, keepdims=True)
            o_s = (_dot3(p_sel, vbuf[slot, hh, s], NT) + _dot3(p_own, vn_ref[...][:, hs])) / l
            out = jnp.where(rsel == s, o_s, out)
        o_ref[:, hs] = out


def _moba_sample_attend(phys, q, k_new, v_new, cache_kt, cache_vt, *, db, n_q):
    row = pl.BlockSpec((SAMPLE_PAD, V7X_LANES), lambda b, hp, ph: (b, hp))
    any_spec = pl.BlockSpec(memory_space=pl.ANY)
    buf = pltpu.VMEM((2, HEADS_PER_STEP, n_q, HD_D, MOBA_TOPK * MOBA_BLOCK), F32)
    grid_spec = pltpu.PrefetchScalarGridSpec(
        num_scalar_prefetch=1,
        grid=(db, D_D // V7X_LANES),
        in_specs=[row, row, row, any_spec, any_spec],
        out_specs=row,
        scratch_shapes=[buf, buf, pltpu.SemaphoreType.DMA((2, 2))],
    )
    return pl.pallas_call(
        functools.partial(_attend_kernel, n_q=n_q),
        grid_spec=grid_spec,
        out_shape=jax.ShapeDtypeStruct((db * SAMPLE_PAD, D_D), F32),
        compiler_params=_params(2),
        name="moba_sample_attend",
    )(phys, q, k_new, v_new, cache_kt, cache_vt)


def _outproj2_ln_kernel(a1_ref, a2_ref, x_ref, w1h_ref, w1l_ref, w2h_ref, w2l_ref, g_ref, b_ref, o_ref):
    y = _dot_hl(*_split2(a1_ref[...]), w1h_ref[...], w1l_ref[...])
    y = y + _dot_hl(*_split2(a2_ref[...]), w2h_ref[...], w2l_ref[...])
    o_ref[...] = _layer_norm(DN_ALPHA * x_ref[...] + y, g_ref[...], b_ref[...])


def _outproj2_ln(a1, a2, x, w1, w2, g, b, *, n_rows, tm):
    d = x.shape[1]
    assert n_rows % tm == 0
    row = lambda c: pl.BlockSpec((tm, c), lambda i: (i, 0))
    consts = [w1[0], w1[1], w2[0], w2[1], g, b]
    return pl.pallas_call(
        _outproj2_ln_kernel,
        grid=(n_rows // tm,),
        in_specs=[row(a1.shape[1]), row(a2.shape[1]), row(d)] + [_const_spec(c.shape) for c in consts],
        out_specs=row(d),
        out_shape=jax.ShapeDtypeStruct((n_rows, d), F32),
        compiler_params=_params(1),
        name="outproj2_ln",
    )(a1, a2, x, *consts)


SAMPLE_PAD = 16


def _pad_rows(x, t_pad):
    nb, t, d = x.shape
    return jnp.zeros((nb, t_pad, d), x.dtype).at[:, :t].set(x).reshape(nb * t_pad, d)


def kernel(x_prompt, x_sample, state_gla, state_ssm, state_conv, state_pool, cache_k, cache_v, page_table, ab_w_in, ab_w_gate2, ab_b_gate, ab_gla_norm, ab_conv_w, ab_conv_b, ab_dt_bias, ab_a_log, ab_d_skip, ab_ssm_norm, ab_w_out, cd_w_in, cd_w_pool, cd_pool_scale, cd_w_out, moe_w_coarse, moe_b_coarse, moe_w_fine, moe_b_fine, moe_w_gate_up, moe_w_down, ln_g, ln_b):
    bp, tp, d = x_prompt.shape
    bs, ts, _ = x_sample.shape
    (wih, wil), ab_prm, (woh, wol) = _ab_prepare(ab_w_in, ab_w_gate2, ab_b_gate, ab_gla_norm, ab_conv_w,
                                                 ab_conv_b, ab_dt_bias, ab_a_log, ab_d_skip, ab_ssm_norm, ab_w_out)
    n_p, n_s = bp * tp, bs * ts
    n_pages = page_table.shape[1]
    past = n_pages * PAGE_SIZE
    assert past % MOBA_BLOCK == 0 and past // MOBA_BLOCK >= MOBA_TOPK and ts <= SAMPLE_PAD
    zeros = lambda *s: jnp.zeros(s, F32)
    ln = lambda l, j: (ln_g[l, j].reshape(1, d), ln_b[l, j].reshape(1, d))
    unpad = lambda rows: rows.reshape(bs, SAMPLE_PAD, -1)[:, :ts]
    assert n_s == V7X_LANES
    moe = lambda l, xa, xb: _moe_layer(xa, unpad(xb).reshape(n_s, d), moe_w_coarse[l], moe_b_coarse[l],
                                       moe_w_fine[l], moe_b_fine[l], moe_w_gate_up, moe_w_down, l, *ln(l, 1),
                                       tm=V7X_LANES,
                                       three_pass=l < DEPTH - 1)

    xp = x_prompt.reshape(n_p, d)
    xs = _pad_rows(x_sample, SAMPLE_PAD)
    proj_p = _proj(xp, wih, wil)
    oy_p, gla_p, ssm_p, conv_p = _scan_from_proj(
        proj_p, zeros(bp, H_A, DK_A, DV_A), zeros(bp, H_B, P_B, N_B), zeros(bp, CONV_W - 1, CONV_DIM), ab_prm,
        nb=bp, t_len=tp, tc=256, chunk=CHUNK, valid=256)
    xp = _outproj_ln(oy_p, xp, woh, wol, *ln(0, 0))
    proj_s = _proj(xs, wih, wil)
    oy_s, gla_s, ssm_s, conv_s = _scan_from_proj(
        proj_s, state_gla, state_ssm, state_conv, ab_prm,
        nb=bs, t_len=SAMPLE_PAD, tc=SAMPLE_PAD, chunk=SAMPLE_PAD, valid=ts)
    xs = _outproj_ln(oy_s, xs, woh, wol, *ln(0, 0))
    xp, xs = moe(0, xp, xs)

    wch, wcl = _split_w(cd_w_in)
    wph, wpl = _split_w(cd_w_pool)
    pscale = cd_pool_scale.reshape(1, D_C)
    wo_c, wo_d = _split_w(cd_w_out[:D_C]), _split_w(cd_w_out[D_C:])
    cos_p, sin_p, perm = _rope_tables(jnp.arange(tp))
    u, q, k, v = _cdproj(xp, wch, wcl, cos_p, sin_p, perm, n_rows=n_p, tm=256)
    pooled, pool_p = _pool(u, zeros(bp, POOL_BUF, D_C), wph, wpl, pscale, nb=bp, t_len=tp, tc=256, valid=256, start=0)
    att = _moba_prompt(q, k, v, nb=bp, t_len=tp)
    xp = _outproj2_ln(pooled, att, xp, wo_c, wo_d, *ln(1, 0), n_rows=n_p, tm=256)
    k_p, v_p = k.reshape(bp, tp, H_D, HD_D), v.reshape(bp, tp, H_D, HD_D)

    xs = _pad_rows(xs.reshape(bs, ts, d), SAMPLE_PAD)
    cos_s, sin_s, _ = _rope_tables(past + jnp.arange(SAMPLE_PAD))
    n_rows_s = bs * SAMPLE_PAD
    u, q, k, v = _cdproj(xs, wch, wcl, jnp.tile(cos_s, (bs, 1)), jnp.tile(sin_s, (bs, 1)), perm,
                         n_rows=n_rows_s, tm=256)
    pooled, pool_s = _pool(u, state_pool, wph, wpl, pscale, nb=bs, t_len=SAMPLE_PAD, tc=SAMPLE_PAD, valid=ts,
                           start=past)
    cache_kt, cache_vt = jnp.transpose(cache_k, (0, 2, 3, 1)), jnp.transpose(cache_v, (0, 2, 3, 1))
    picks = _select_blocks(q, _block_means(cache_kt, page_table))[:, :, :ts, :MOBA_TOPK]
    logical = picks[..., None] * PAGES_PER_BLOCK + jnp.arange(PAGES_PER_BLOCK)
    phys = page_table[jnp.arange(bs)[:, None, None, None, None], logical].reshape(-1)
    att = _moba_sample_attend(phys, q, k, v, cache_kt, cache_vt, db=bs, n_q=ts)
    xs = _outproj2_ln(pooled, att, xs, wo_c, wo_d, *ln(1, 0), n_rows=n_rows_s, tm=256)
    k_s, v_s = unpad(k).reshape(bs, ts, H_D, HD_D), unpad(v).reshape(bs, ts, H_D, HD_D)
    xp, xs = moe(1, xp, xs)

    return (xp.reshape(bp, tp, d), xs.reshape(bs, ts, d), gla_p, ssm_p, conv_p, pool_p, k_p, v_p,
            gla_s, ssm_s, conv_s, pool_s, k_s, v_s)
```
